```python
import math
import jax, jax.numpy as jnp
from jax import lax
import numpy as np

D_MODEL = 1024
BATCH = 4
SEQ = 4096
DEPTH = 4
DEC_BATCH = 16
DEC_SEQ = 2048
PAST_LEN = 128

N_EVEN = (DEPTH + 1) // 2
N_ODD = DEPTH // 2
D_A = D_MODEL // 2
HEAD_A = 128
H_A = D_A // HEAD_A
CHUNK_A = 32
D_B = D_MODEL // 2
HEAD_B = 128
H_B = D_B // HEAD_B
CHUNK_B = 128
ROPE_BASE = 10000.0
D_MIX_EVEN = D_A + D_B
D_IN_EVEN = 5 * D_A + 4 * D_B
HEAD_C = 64
H_C = D_MODEL // HEAD_C
LORA_W = 64
LORA_A = 64
LORA_G = 128
LNX_EPS = 64e-5
MEM_LEN = 256
H_CA = 4
HEAD_CA = D_MODEL // H_CA
D_CA = H_CA * HEAD_CA
N_EXPERTS = 16
D_EXPERT = 2048
CAP_FACTOR = 2
DN_ALPHA = (2.0 * DEPTH) ** 0.25
DN_BETA = (8.0 * DEPTH) ** -0.25
LN_EPS = 1e-5

kernel_name = "hybrid_hgrn2_retnet_rwkv7_ec_moe_encoder"


def _layer_norm(x, w, b):
    xf = x.astype(jnp.float32)
    mu = jnp.mean(xf, -1, keepdims=True)
    var = jnp.mean(jnp.square(xf - mu), -1, keepdims=True)
    return ((xf - mu) * lax.rsqrt(var + LN_EPS) * w + b).astype(x.dtype)


def _head_norm(o, gain, bias, eps, center):
    of = o.astype(jnp.float32)
    if center:
        of = of - jnp.mean(of, -1, keepdims=True)
    of = of * lax.rsqrt(jnp.mean(jnp.square(of), -1, keepdims=True) + eps)
    of = of.reshape(of.shape[:-2] + (-1,)) * gain
    if bias is not None:
        of = of + bias
    return of


def _rotary(t, pos):
    d = t.shape[-1]
    theta = 1.0 / jnp.power(ROPE_BASE, jnp.linspace(0.0, 1.0, d // 2, dtype=jnp.float32))
    ang = pos[:, None] * theta[None, :]
    cos = jnp.cos(ang)[None, :, None, :]
    sin = jnp.sin(ang)[None, :, None, :]
    tf = t.astype(jnp.float32)
    t1, t2 = tf[..., 0::2], tf[..., 1::2]
    out = jnp.stack([t1 * cos - t2 * sin, t2 * cos + t1 * sin], -1).reshape(t.shape)
    return out.astype(t.dtype)


def _bidir(fwd, bwd):
    return jnp.stack([fwd, jnp.flip(bwd, axis=1)])


def _merge(o):
    return o[0] + jnp.flip(o[1], axis=1)


def _to_chunks(t, L):
    N, B, T, H, d = t.shape
    return t.reshape(N, B, T // L, L, H, d).transpose(2, 0, 1, 4, 3, 5)


def _from_chunks(t):
    nC, N, B, H, L, d = t.shape
    return t.transpose(1, 2, 0, 4, 3, 5).reshape(N, B, nC * L, H, d)


def _hgrn2_chunkwise(q, k, v, log_f):
    N, B, T, H, dk = q.shape
    dv = v.shape[-1]
    causal = jnp.tri(CHUNK_A, dtype=bool)

    def step(S, inp):
        qc, kc, vc, lfc = [t.astype(jnp.float32) for t in inp]
        b = jnp.cumsum(lfc, axis=-2)
        diff = b[..., :, None, :] - b[..., None, :, :]
        decay = jnp.exp(jnp.where(causal[:, :, None], diff, -jnp.inf))
        scores = jnp.einsum('nbhtsk,nbhsk->nbhts', decay * qc[..., :, None, :], kc)
        o = jnp.einsum('nbhts,nbhsv->nbhtv', scores, vc) + \
            jnp.einsum('nbhtk,nbhkv->nbhtv', qc * jnp.exp(b), S)
        b_last = b[..., -1:, :]
        S = jnp.exp(b_last[..., 0, :])[..., None] * S + \
            jnp.einsum('nbhsk,nbhsv->nbhkv', kc * jnp.exp(b_last - b), vc)
        return S, o

    S0 = jnp.zeros((N, B, H, dk, dv), jnp.float32)
    xs = (_to_chunks(q, CHUNK_A), _to_chunks(k, CHUNK_A), _to_chunks(v, CHUNK_A), _to_chunks(log_f, CHUNK_A))
    _, o = lax.scan(step, S0, xs)
    return _from_chunks(o).astype(v.dtype)


def _retention_chunkwise(q, k, v, log_gamma, intra_mask):
    N, B, T, H, dk = q.shape
    dv = v.shape[-1]
    pos = jnp.arange(CHUNK_B, dtype=jnp.float32)
    lg = log_gamma.astype(jnp.float32)
    dist = jnp.maximum(pos[:, None] - pos[None, :], 0.0)
    d_intra = jnp.where(intra_mask[:, None], jnp.exp(lg[:, None, None] * dist), 0.0)
    q_decay = jnp.exp(lg[:, None] * (pos + 1.0))[:, :, None]
    k_decay = jnp.exp(lg[:, None] * (CHUNK_B - 1.0 - pos))[:, :, None]
    chunk_decay = jnp.exp(lg * CHUNK_B)[:, None, None]

    def step(R, inp):
        qc, kc, vc = [t.astype(jnp.float32) for t in inp]
        scores = jnp.einsum('nbhtk,nbhsk->nbhts', qc, kc) * d_intra[:, None]
        o = jnp.einsum('nbhts,nbhsv->nbhtv', scores, vc) + \
            jnp.einsum('nbhtk,nbhkv->nbhtv', qc * q_decay, R)
        R = chunk_decay * R + jnp.einsum('nbhsk,nbhsv->nbhkv', kc * k_decay, vc)
        return R, o

    R0 = jnp.zeros((N, B, H, dk, dv), jnp.float32)
    xs = (_to_chunks(q, CHUNK_B), _to_chunks(k, CHUNK_B), _to_chunks(v, CHUNK_B))
    _, o = lax.scan(step, R0, xs)
    return _from_chunks(o).astype(v.dtype)


def _even_mixer(x, w_in, lb, norm_a, norm_b, w_out):
    Bn, T, _ = x.shape
    cuts = [D_A * i for i in range(1, 6)] + [5 * D_A + D_B * j for j in range(1, 4)]
    a_q, a_i, a_ff, a_fb, a_g, b_q, b_k, b_v, b_g = jnp.split(x @ w_in, cuts, axis=-1)

    heads_a = lambda t: t.reshape(t.shape[:-1] + (H_A, HEAD_A))
    z = jnp.stack([a_ff, a_fb]).astype(jnp.float32)
    lbf = lb.astype(jnp.float32)[:, None, None, :]
    log_f = jnp.logaddexp(jnp.log(lbf), jnp.log1p(-lbf) + jax.nn.log_sigmoid(z))
    k_in = (1.0 - lbf) * jax.nn.sigmoid(-z)
    q_in = jax.nn.silu(a_q)
    o_a = _hgrn2_chunkwise(heads_a(_bidir(q_in, q_in)), heads_a(_bidir(k_in[0], k_in[1])),
                           heads_a(_bidir(a_i, a_i)), heads_a(_bidir(log_f[0], log_f[1])))
    o_a = _head_norm(_merge(o_a), norm_a, None, 1e-6, False) * jax.nn.silu(a_g)

    heads_b = lambda t: t.reshape(t.shape[:-1] + (H_B, HEAD_B))
    pos = jnp.arange(T, dtype=jnp.float32)
    qb = _rotary(heads_b(b_q), pos)
    kb = _rotary(heads_b(b_k), pos) * (HEAD_B ** -0.5)
    vb = heads_b(b_v)
    log_gamma = jnp.log1p(-jnp.power(2.0, -5.0 - jnp.arange(H_B, dtype=jnp.float32)))
    mask = jnp.stack([jnp.tri(CHUNK_B, dtype=bool), jnp.tri(CHUNK_B, k=-1, dtype=bool)])
    o_b = _retention_chunkwise(_bidir(qb, qb), _bidir(kb, kb), _bidir(vb, vb), log_gamma, mask)
    o_b = _head_norm(_merge(o_b), norm_b, None, 1e-6, True) * jax.nn.silu(b_g)

    y = jnp.concatenate([o_a, o_b.astype(o_a.dtype)], -1) @ w_out
    return y.astype(x.dtype)


def _odd_mixer(x, mu, w_rkv, w0, w1, w2, a0, a1, a2, g1, g2, k_k, k_a, r_k, lnx_w, lnx_b, w_out):
    Bn, T, D = x.shape
    zero = jnp.zeros_like(x[:, :1])
    x_prev = jnp.concatenate([zero, x[:, :-1]], 1)
    x_next = jnp.concatenate([x[:, 1:], zero], 1)
    xx = 0.5 * (x_prev + x_next) - x
    xm = x[None] + xx[None] * mu[:, None, None, :]
    rkv = jnp.einsum('pbtd,pde->pbte', xm[:3], w_rkv)
    r, k, v = rkv[0], rkv[1], rkv[2]
    xw, xa, xg = xm[3], xm[4], xm[5]
    lora_w = jnp.einsum('nbtl,nld->nbtd', jnp.tanh(jnp.einsum('btd,ndl->nbtl', xw, w1)), w2)
    w_log = -jax.nn.softplus(-(w0[:, None, None, :] + lora_w).astype(jnp.float32)) - 0.5
    decay = jnp.exp(-jnp.exp(w_log))
    a = jax.nn.sigmoid((a0[:, None, None, :] + jnp.einsum('nbtl,nld->nbtd', jnp.einsum('btd,ndl->nbtl', xa, a1), a2)).astype(jnp.float32))
    g = jax.nn.sigmoid(xg @ g1) @ g2

    heads = lambda t: t.reshape(t.shape[:-1] + (H_C, HEAD_C))
    kk = heads(k * k_k).astype(jnp.float32)
    kk = kk / jnp.maximum(jnp.sqrt(jnp.sum(jnp.square(kk), -1, keepdims=True)), 1e-12)
    a_h = heads(a)
    k_dir = heads(k).astype(jnp.float32)[None] * (1.0 + (a_h - 1.0) * heads(k_a))
    decay_h = heads(decay)
    r_h, v_h = heads(r), heads(v)
    tm = lambda t: jnp.moveaxis(t, 1, 0).astype(jnp.float32)
    r_s, v_s, kk_s = tm(r_h), tm(v_h), tm(kk)

    def run(n, reverse):
        def step(S, inp):
            r_t, w_t, k_t, v_t, kk_t, b_t = inp
            sa = jnp.einsum('bhvk,bhk->bhv', S, kk_t)
            S = S * w_t[:, :, None, :] - sa[..., None] * b_t[:, :, None, :] + v_t[..., None] * k_t[:, :, None, :]
            return S, jnp.einsum('bhvk,bhk->bhv', S, r_t)
        S0 = jnp.zeros((Bn, H_C, HEAD_C, HEAD_C), jnp.float32)
        xs = (r_s, tm(decay_h[n]), tm(k_dir[n]), v_s, kk_s, tm(kk * a_h[n]))
        _, ys = lax.scan(step, S0, xs, reverse=reverse)
        return ys

    o = jnp.moveaxis(run(0, False) + run(1, True), 0, 1)
    o = _head_norm(o, lnx_w, lnx_b, LNX_EPS, True)
    bonus = jnp.sum(r_h[None].astype(jnp.float32) * k_dir * heads(r_k), -1, keepdims=True) * v_h[None]
    bonus = (bonus[0] + bonus[1]).reshape(Bn, T, D)
    y = ((o + bonus) * g) @ w_out
    return y.astype(x.dtype)


def _cross_attn(x, mem, w_q, w_kv, w_out):
    Bn, T, _ = x.shape
    M = mem.shape[1]
    q = (x @ w_q).reshape(Bn, T, H_CA, HEAD_CA)
    kv = (mem @ w_kv).reshape(Bn, M, 2, H_CA, HEAD_CA)
    s = jnp.einsum('bthd,bmhd->bhtm', q, kv[:, :, 0]).astype(jnp.float32) * (HEAD_CA ** -0.5)
    p = jax.nn.softmax(s, axis=-1).astype(x.dtype)
    o = jnp.einsum('bhtm,bmhd->bthd', p, kv[:, :, 1]).reshape(Bn, T, D_CA)
    return (o @ w_out).astype(x.dtype)


def _expert_choice_ffn(x, w_router, w_in, w_out):
    Bn, T, D = x.shape
    n = Bn * T
    cap = (CAP_FACTOR * n) // N_EXPERTS
    xt = x.reshape(n, D)
    aff = jax.nn.softmax((xt @ w_router).astype(jnp.float32), axis=-1)
    gate, idx = lax.top_k(aff.T, cap)
    xe = xt[idx]
    h_gate, h_up = jnp.split(jnp.einsum('ecd,edf->ecf', xe, w_in), 2, axis=-1)
    ye = jnp.einsum('ecf,efd->ecd', jax.nn.silu(h_gate) * h_up, w_out) * gate[..., None].astype(x.dtype)
    y = jnp.zeros_like(xt).at[idx.reshape(-1)].add(ye.reshape(-1, D).astype(xt.dtype))
    return y.reshape(Bn, T, D)


def _trunk(x, mem, p):
    p_lb = jax.nn.softmax(p['ev_lb_logits'].astype(jnp.float32), axis=1)
    lb_all = jnp.clip(jnp.cumsum(p_lb, axis=1) - p_lb[:, :1], 0.0, 1.0)
    for layer in range(DEPTH):
        j = layer // 2
        if layer % 2 == 0:
            h = _even_mixer(x, p['ev_w_in'][j], lb_all[:, j], p['ev_norm_a'][j], p['ev_norm_b'][j], p['ev_w_out'][j])
        else:
            h = _odd_mixer(x, p['od_mu'][j], p['od_w_rkv'][j], p['od_w0'][j], p['od_w1'][j], p['od_w2'][j],
                           p['od_a0'][j], p['od_a1'][j], p['od_a2'][j], p['od_g1'][j], p['od_g2'][j],
                           p['od_k_k'][j], p['od_k_a'][j], p['od_r_k'][j], p['od_lnx_w'][j], p['od_lnx_b'][j],
                           p['od_w_out'][j])
        x = _layer_norm(DN_ALPHA * x + h, p['ln_w'][layer, 0], p['ln_b'][layer, 0])
        h = _cross_attn(x, mem, p['ca_w_q'][layer], p['ca_w_kv'][layer], p['ca_w_out'][layer])
        x = _layer_norm(DN_ALPHA * x + h, p['ln_w'][layer, 1], p['ln_b'][layer, 1])
        h = _expert_choice_ffn(x, p['moe_router'][layer], p['moe_w_in'][layer], p['moe_w_out'][layer])
        x = _layer_norm(DN_ALPHA * x + h, p['ln_w'][layer, 2], p['ln_b'][layer, 2])
    return x


def setup_inputs(seed: int = 0) -> dict:
    key = jax.random.key(seed)
    k = jax.random.split(key, 33)
    D = D_MODEL
    nrm = lambda i, shape, scale: jax.random.normal(k[i], shape, jnp.float32) * scale
    uni = lambda i, shape, lo, hi: jax.random.uniform(k[i], shape, jnp.float32, lo, hi)
    return {
        'x_prompt': nrm(0, (BATCH, SEQ, D), 1.0),
        'x_sample': nrm(1, (DEC_BATCH, DEC_SEQ, D), 1.0),
        'mem_prompt': nrm(2, (BATCH, MEM_LEN, D), 1.0),
        'mem_sample': nrm(3, (DEC_BATCH, MEM_LEN, D), 1.0),
        'ev_w_in': nrm(4, (N_EVEN, D, D_IN_EVEN), D ** -0.5),
        'ev_lb_logits': nrm(5, (2, N_EVEN, D_A), 0.5),
        'ev_norm_a': 1.0 + nrm(6, (N_EVEN, D_A), 0.02),
        'ev_norm_b': 1.0 + nrm(7, (N_EVEN, D_B), 0.02),
        'ev_w_out': nrm(8, (N_EVEN, D_MIX_EVEN, D), D_MIX_EVEN ** -0.5 * DN_BETA),
        'od_mu': uni(9, (N_ODD, 6, D), 0.0, 1.0),
        'od_w_rkv': nrm(10, (N_ODD, 3, D, D), D ** -0.5),
        'od_w0': uni(11, (N_ODD, 2, D), -4.0, -0.5),
        'od_w1': nrm(12, (N_ODD, 2, D, LORA_W), D ** -0.5),
        'od_w2': nrm(13, (N_ODD, 2, LORA_W, D), 0.1 * LORA_W ** -0.5),
        'od_a0': nrm(14, (N_ODD, 2, D), 0.1),
        'od_a1': nrm(15, (N_ODD, 2, D, LORA_A), D ** -0.5),
        'od_a2': nrm(16, (N_ODD, 2, LORA_A, D), 0.1 * LORA_A ** -0.5),
        'od_g1': nrm(17, (N_ODD, D, LORA_G), D ** -0.5),
        'od_g2': nrm(18, (N_ODD, LORA_G, D), LORA_G ** -0.5),
        'od_k_k': 0.85 + nrm(19, (N_ODD, D), 0.02),
        'od_k_a': 1.0 + nrm(20, (N_ODD, D), 0.02),
        'od_r_k': nrm(21, (N_ODD, D), 0.1),
        'od_lnx_w': 1.0 + nrm(22, (N_ODD, D), 0.02),
        'od_lnx_b': nrm(23, (N_ODD, D), 0.02),
        'od_w_out': nrm(24, (N_ODD, D, D), D ** -0.5 * DN_BETA),
        'ca_w_q': nrm(25, (DEPTH, D, D_CA), D ** -0.5),
        'ca_w_kv': nrm(26, (DEPTH, D, 2 * D_CA), D ** -0.5),
        'ca_w_out': nrm(27, (DEPTH, D_CA, D), D_CA ** -0.5 * DN_BETA),
        'moe_router': nrm(28, (DEPTH, D, N_EXPERTS), D ** -0.5),
        'moe_w_in': nrm(29, (DEPTH, N_EXPERTS, D, 2 * D_EXPERT), D ** -0.5),
        'moe_w_out': nrm(30, (DEPTH, N_EXPERTS, D_EXPERT, D), D_EXPERT ** -0.5 * DN_BETA),
        'ln_w': 1.0 + nrm(31, (DEPTH, 3, D), 0.02),
        'ln_b': nrm(32, (DEPTH, 3, D), 0.02),
    }


def reference(x_prompt, x_sample, mem_prompt, mem_sample, ev_w_in, ev_lb_logits, ev_norm_a, ev_norm_b, ev_w_out,
              od_mu, od_w_rkv, od_w0, od_w1, od_w2, od_a0, od_a1, od_a2, od_g1, od_g2, od_k_k, od_k_a, od_r_k,
              od_lnx_w, od_lnx_b, od_w_out, ca_w_q, ca_w_kv, ca_w_out, moe_router, moe_w_in, moe_w_out, ln_w, ln_b):
    params = dict(ev_w_in=ev_w_in, ev_lb_logits=ev_lb_logits, ev_norm_a=ev_norm_a, ev_norm_b=ev_norm_b,
                  ev_w_out=ev_w_out, od_mu=od_mu, od_w_rkv=od_w_rkv, od_w0=od_w0, od_w1=od_w1, od_w2=od_w2,
                  od_a0=od_a0, od_a1=od_a1, od_a2=od_a2, od_g1=od_g1, od_g2=od_g2, od_k_k=od_k_k,
                  od_k_a=od_k_a, od_r_k=od_r_k, od_lnx_w=od_lnx_w, od_lnx_b=od_lnx_b, od_w_out=od_w_out,
                  ca_w_q=ca_w_q, ca_w_kv=ca_w_kv, ca_w_out=ca_w_out, moe_router=moe_router,
                  moe_w_in=moe_w_in, moe_w_out=moe_w_out, ln_w=ln_w, ln_b=ln_b)
    y_prompt = _trunk(x_prompt, mem_prompt, params)
    y_sample = _trunk(x_sample, mem_sample, params)
    return (y_prompt, y_sample)
```

```python
import functools
import math

import jax
import jax.numpy as jnp
from jax import lax
from jax.experimental import pallas as pl
from jax.experimental.pallas import tpu as pltpu

F32 = jnp.float32
BF16 = jnp.bfloat16

D_MODEL = 1024
DEPTH = 4
D_A = D_MODEL // 2
HEAD_A = 128
H_A = D_A // HEAD_A
CHUNK_A = 32
D_B = D_MODEL // 2
HEAD_B = 128
H_B = D_B // HEAD_B
CHUNK_B = 128
ROPE_BASE = 10000.0
HEAD_C = 64
H_C = D_MODEL // HEAD_C
LNX_EPS = 64e-5
H_CA = 4
HEAD_CA = D_MODEL // H_CA
N_EXPERTS = 16
D_EXPERT = 2048
CAP_FACTOR = 2
DN_ALPHA = (2.0 * DEPTH) ** 0.25
LN_EPS = 1e-5

VMEM_LIMIT = 48 * 1024 * 1024


def _cparams(*sem):
    return pltpu.CompilerParams(dimension_semantics=sem, vmem_limit_bytes=VMEM_LIMIT)


def _mm_kernel(x_ref, w_ref, o_ref):
    o_ref[...] = jnp.dot(x_ref[...].astype(BF16), w_ref[...],
                         preferred_element_type=F32).astype(o_ref.dtype)


def _pick(n, pref):
    t = min(n, pref)
    while n % t:
        t //= 2
    return t


def pmm(x, w, tm=1024, tn=512):
    M, K = x.shape
    N = w.shape[1]
    tm = _pick(M, tm)
    tn = _pick(N, tn) if N % 128 == 0 else N
    return pl.pallas_call(
        _mm_kernel,
        grid=(M // tm, N // tn),
        in_specs=[pl.BlockSpec((tm, K), lambda i, j: (i, 0)),
                  pl.BlockSpec((K, tn), lambda i, j: (0, j))],
        out_specs=pl.BlockSpec((tm, tn), lambda i, j: (i, j)),
        out_shape=jax.ShapeDtypeStruct((M, N), F32),
        compiler_params=_cparams("parallel", "parallel"),
        name="dense_proj",
    )(x, w)


def _mm(x, w):
    lead = x.shape[:-1]
    return pmm(x.reshape(-1, x.shape[-1]), w).reshape(lead + (w.shape[1],))


def _ca_kernel(q_ref, k_ref, v_ref, o_ref):
    scale = HEAD_CA ** -0.5
    for h in range(H_CA):
        sl = slice(h * HEAD_CA, (h + 1) * HEAD_CA)
        q = q_ref[0, :, sl].astype(BF16)
        k = k_ref[0, :, sl].astype(BF16)
        v = v_ref[0, :, sl].astype(BF16)
        s = lax.dot_general(q, k, (((1,), (1,)), ((), ())), preferred_element_type=F32) * scale
        m = jnp.max(s, axis=-1, keepdims=True)
        e = jnp.exp(s - m)
        p = e / jnp.sum(e, axis=-1, keepdims=True)
        o_ref[0, :, sl] = jnp.dot(p.astype(BF16), v, preferred_element_type=F32)


def cross_attn_core(q, kv, tq=512):
    B, T, D = q.shape
    M = kv.shape[1]
    tq = _pick(T, tq)
    return pl.pallas_call(
        _ca_kernel,
        grid=(B, T // tq),
        in_specs=[pl.BlockSpec((1, tq, D), lambda b, i: (b, i, 0)),
                  pl.BlockSpec((1, M, D), lambda b, i: (b, 0, 0)),
                  pl.BlockSpec((1, M, D), lambda b, i: (b, 0, 1))],
        out_specs=pl.BlockSpec((1, tq, D), lambda b, i: (b, i, 0)),
        out_shape=jax.ShapeDtypeStruct((B, T, D), F32),
        compiler_params=_cparams("parallel", "parallel"),
        name="cross_attn",
    )(q, kv, kv)


def _moe_kernel(x_ref, wg_ref, wu_ref, wo_ref, gate_ref, o_ref, acc_ref, *, nf):
    f = pl.program_id(2)

    @pl.when(f == 0)
    def _():
        acc_ref[...] = jnp.zeros_like(acc_ref)

    x = x_ref[0].astype(BF16)
    hg = jnp.dot(x, wg_ref[0], preferred_element_type=F32)
    hu = jnp.dot(x, wu_ref[0], preferred_element_type=F32)
    a = (hg * jax.nn.sigmoid(hg) * hu).astype(BF16)
    acc_ref[...] += jnp.dot(a, wo_ref[0], preferred_element_type=F32)

    @pl.when(f == nf - 1)
    def _():
        o_ref[0] = acc_ref[...] * gate_ref[0]


def moe_ffn(xe, w_in, w_out, gate, tm=1024, tf=512):
    E, cap, D = xe.shape
    Fh = w_out.shape[1]
    tm = _pick(cap, tm)
    nf = Fh // tf
    return pl.pallas_call(
        functools.partial(_moe_kernel, nf=nf),
        grid=(E, cap // tm, nf),
        in_specs=[pl.BlockSpec((1, tm, D), lambda e, i, f: (e, i, 0)),
                  pl.BlockSpec((1, D, tf), lambda e, i, f: (e, 0, f)),
                  pl.BlockSpec((1, D, tf), lambda e, i, f: (e, 0, nf + f)),
                  pl.BlockSpec((1, tf, D), lambda e, i, f: (e, f, 0)),
                  pl.BlockSpec((1, tm, 1), lambda e, i, f: (e, i, 0))],
        out_specs=pl.BlockSpec((1, tm, D), lambda e, i, f: (e, i, 0)),
        out_shape=jax.ShapeDtypeStruct((E, cap, D), F32),
        scratch_shapes=[pltpu.VMEM((tm, D), F32)],
        compiler_params=_cparams("parallel", "parallel", "arbitrary"),
        name="moe_ffn",
    )(xe, w_in, w_in, w_out, gate)


def _layer_norm(x, w, b):
    mu = jnp.mean(x, -1, keepdims=True)
    var = jnp.mean(jnp.square(x - mu), -1, keepdims=True)
    return (x - mu) * lax.rsqrt(var + LN_EPS) * w + b


def _head_norm(o, gain, bias, eps, center):
    of = o
    if center:
        of = of - jnp.mean(of, -1, keepdims=True)
    of = of * lax.rsqrt(jnp.mean(jnp.square(of), -1, keepdims=True) + eps)
    of = of.reshape(of.shape[:-2] + (-1,)) * gain
    if bias is not None:
        of = of + bias
    return of


def _rotary(t, pos):
    d = t.shape[-1]
    theta = 1.0 / jnp.power(ROPE_BASE, jnp.linspace(0.0, 1.0, d // 2, dtype=F32))
    ang = pos[:, None] * theta[None, :]
    cos = jnp.cos(ang)[None, :, None, :]
    sin = jnp.sin(ang)[None, :, None, :]
    t1, t2 = t[..., 0::2], t[..., 1::2]
    return jnp.stack([t1 * cos - t2 * sin, t2 * cos + t1 * sin], -1).reshape(t.shape)


def _bidir(fwd, bwd):
    return jnp.stack([fwd, jnp.flip(bwd, axis=1)])


def _merge(o):
    return o[0] + jnp.flip(o[1], axis=1)


def _to_chunks(t, L):
    N, B, T, H, d = t.shape
    return t.reshape(N, B, T // L, L, H, d).transpose(2, 0, 1, 4, 3, 5)


def _from_chunks(t):
    nC, N, B, H, L, d = t.shape
    return t.transpose(1, 2, 0, 4, 3, 5).reshape(N, B, nC * L, H, d)


def _hgrn2_chunkwise(q, k, v, log_f):
    N, B, T, H, dk = q.shape
    dv = v.shape[-1]
    causal = jnp.tri(CHUNK_A, dtype=bool)

    def step(S, inp):
        qc, kc, vc, lfc = inp
        b = jnp.cumsum(lfc, axis=-2)
        diff = b[..., :, None, :] - b[..., None, :, :]
        decay = jnp.exp(jnp.where(causal[:, :, None], diff, -jnp.inf))
        scores = jnp.einsum('nbhtsk,nbhsk->nbhts', decay * qc[..., :, None, :], kc)
        o = jnp.einsum('nbhts,nbhsv->nbhtv', scores, vc) + \
            jnp.einsum('nbhtk,nbhkv->nbhtv', qc * jnp.exp(b), S)
        b_last = b[..., -1:, :]
        S = jnp.exp(b_last[..., 0, :])[..., None] * S + \
            jnp.einsum('nbhsk,nbhsv->nbhkv', kc * jnp.exp(b_last - b), vc)
        return S, o

    S0 = jnp.zeros((N, B, H, dk, dv), F32)
    xs = (_to_chunks(q, CHUNK_A), _to_chunks(k, CHUNK_A), _to_chunks(v, CHUNK_A), _to_chunks(log_f, CHUNK_A))
    _, o = lax.scan(step, S0, xs)
    return _from_chunks(o)


def _retention_chunkwise(q, k, v, log_gamma, intra_mask):
    N, B, T, H, dk = q.shape
    dv = v.shape[-1]
    pos = jnp.arange(CHUNK_B, dtype=F32)
    lg = log_gamma
    dist = jnp.maximum(pos[:, None] - pos[None, :], 0.0)
    d_intra = jnp.where(intra_mask[:, None], jnp.exp(lg[:, None, None] * dist), 0.0)
    q_decay = jnp.exp(lg[:, None] * (pos + 1.0))[:, :, None]
    k_decay = jnp.exp(lg[:, None] * (CHUNK_B - 1.0 - pos))[:, :, None]
    chunk_decay = jnp.exp(lg * CHUNK_B)[:, None, None]

    def step(R, inp):
        qc, kc, vc = inp
        scores = jnp.einsum('nbhtk,nbhsk->nbhts', qc, kc) * d_intra[:, None]
        o = jnp.einsum('nbhts,nbhsv->nbhtv', scores, vc) + \
            jnp.einsum('nbhtk,nbhkv->nbhtv', qc * q_decay, R)
        R = chunk_decay * R + jnp.einsum('nbhsk,nbhsv->nbhkv', kc * k_decay, vc)
        return R, o

    R0 = jnp.zeros((N, B, H, dk, dv), F32)
    xs = (_to_chunks(q, CHUNK_B), _to_chunks(k, CHUNK_B), _to_chunks(v, CHUNK_B))
    _, o = lax.scan(step, R0, xs)
    return _from_chunks(o)


def _even_mixer(x, w_in, lb, norm_a, norm_b, w_out):
    Bn, T, _ = x.shape
    cuts = [D_A * i for i in range(1, 6)] + [5 * D_A + D_B * j for j in range(1, 4)]
    a_q, a_i, a_ff, a_fb, a_g, b_q, b_k, b_v, b_g = jnp.split(_mm(x, w_in), cuts, axis=-1)

    heads_a = lambda t: t.reshape(t.shape[:-1] + (H_A, HEAD_A))
    z = jnp.stack([a_ff, a_fb])
    lbf = lb[:, None, None, :]
    log_f = jnp.logaddexp(jnp.log(lbf), jnp.log1p(-lbf) + jax.nn.log_sigmoid(z))
    k_in = (1.0 - lbf) * jax.nn.sigmoid(-z)
    q_in = jax.nn.silu(a_q)
    o_a = _hgrn2_chunkwise(heads_a(_bidir(q_in, q_in)), heads_a(_bidir(k_in[0], k_in[1])),
                           heads_a(_bidir(a_i, a_i)), heads_a(_bidir(log_f[0], log_f[1])))
    o_a = _head_norm(_merge(o_a), norm_a, None, 1e-6, False) * jax.nn.silu(a_g)

    heads_b = lambda t: t.reshape(t.shape[:-1] + (H_B, HEAD_B))
    pos = jnp.arange(T, dtype=F32)
    qb = _rotary(heads_b(b_q), pos)
    kb = _rotary(heads_b(b_k), pos) * (HEAD_B ** -0.5)
    vb = heads_b(b_v)
    log_gamma = jnp.log1p(-jnp.power(2.0, -5.0 - jnp.arange(H_B, dtype=F32)))
    mask = jnp.stack([jnp.tri(CHUNK_B, dtype=bool), jnp.tri(CHUNK_B, k=-1, dtype=bool)])
    o_b = _retention_chunkwise(_bidir(qb, qb), _bidir(kb, kb), _bidir(vb, vb), log_gamma, mask)
    o_b = _head_norm(_merge(o_b), norm_b, None, 1e-6, True) * jax.nn.silu(b_g)

    return _mm(jnp.concatenate([o_a, o_b], -1), w_out)


def _odd_mixer(x, mu, w_rkv, w0, w1, w2, a0, a1, a2, g1, g2, k_k, k_a, r_k, lnx_w, lnx_b, w_out):
    Bn, T, D = x.shape
    zero = jnp.zeros_like(x[:, :1])
    x_prev = jnp.concatenate([zero, x[:, :-1]], 1)
    x_next = jnp.concatenate([x[:, 1:], zero], 1)
    xx = 0.5 * (x_prev + x_next) - x
    xm = x[None] + xx[None] * mu[:, None, None, :]
    r, k, v = _mm(xm[0], w_rkv[0]), _mm(xm[1], w_rkv[1]), _mm(xm[2], w_rkv[2])
    xw, xa, xg = xm[3], xm[4], xm[5]
    lora_w = jnp.stack([_mm(jnp.tanh(_mm(xw, w1[n])), w2[n]) for n in range(2)])
    w_log = -jax.nn.softplus(-(w0[:, None, None, :] + lora_w)) - 0.5
    decay = jnp.exp(-jnp.exp(w_log))
    a = jax.nn.sigmoid(a0[:, None, None, :] + jnp.stack([_mm(_mm(xa, a1[n]), a2[n]) for n in range(2)]))
    g = _mm(jax.nn.sigmoid(_mm(xg, g1)), g2)

    heads = lambda t: t.reshape(t.shape[:-1] + (H_C, HEAD_C))
    kk = heads(k * k_k)
    kk = kk / jnp.maximum(jnp.sqrt(jnp.sum(jnp.square(kk), -1, keepdims=True)), 1e-12)
    a_h = heads(a)
    k_dir = heads(k)[None] * (1.0 + (a_h - 1.0) * heads(k_a))
    decay_h = heads(decay)
    r_h, v_h = heads(r), heads(v)
    tm = lambda t: jnp.moveaxis(t, 1, 0)
    r_s, v_s, kk_s = tm(r_h), tm(v_h), tm(kk)

    def run(n, reverse):
        def step(S, inp):
            r_t, w_t, k_t, v_t, kk_t, b_t = inp
            sa = jnp.einsum('bhvk,bhk->bhv', S, kk_t)
            S = S * w_t[:, :, None, :] - sa[..., None] * b_t[:, :, None, :] + v_t[..., None] * k_t[:, :, None, :]
            return S, jnp.einsum('bhvk,bhk->bhv', S, r_t)
        S0 = jnp.zeros((Bn, H_C, HEAD_C, HEAD_C), F32)
        xs = (r_s, tm(decay_h[n]), tm(k_dir[n]), v_s, kk_s, tm(kk * a_h[n]))
        _, ys = lax.scan(step, S0, xs, reverse=reverse)
        return ys

    o = jnp.moveaxis(run(0, False) + run(1, True), 0, 1)
    o = _head_norm(o, lnx_w, lnx_b, LNX_EPS, True)
    bonus = jnp.sum(r_h[None] * k_dir * heads(r_k), -1, keepdims=True) * v_h[None]
    bonus = (bonus[0] + bonus[1]).reshape(Bn, T, D)
    return _mm((o + bonus) * g, w_out)


def _cross_attn(x, mem, w_q, w_kv, w_out):
    q = _mm(x, w_q)
    kv = _mm(mem, w_kv)
    return _mm(cross_attn_core(q, kv), w_out)


def _expert_choice_ffn(x, w_router, w_in, w_out):
    Bn, T, D = x.shape
    n = Bn * T
    cap = (CAP_FACTOR * n) // N_EXPERTS
    xt = x.reshape(n, D)
    aff = jax.nn.softmax(pmm(xt, w_router), axis=-1)
    gate, idx = lax.top_k(aff.T, cap)
    xe = xt[idx]
    ye = moe_ffn(xe, w_in, w_out, gate[..., None])
    y = jnp.zeros_like(xt).at[idx.reshape(-1)].add(ye.reshape(-1, D))
    return y.reshape(Bn, T, D)


def _trunk(x, mem, p):
    p_lb = jax.nn.softmax(p['ev_lb_logits'], axis=1)
    lb_all = jnp.clip(jnp.cumsum(p_lb, axis=1) - p_lb[:, :1], 0.0, 1.0)
    for layer in range(DEPTH):
        j = layer // 2
        if layer % 2 == 0:
            h = _even_mixer(x, p['ev_w_in'][j], lb_all[:, j], p['ev_norm_a'][j], p['ev_norm_b'][j], p['ev_w_out'][j])
        else:
            h = _odd_mixer(x, p['od_mu'][j], p['od_w_rkv'][j], p['od_w0'][j], p['od_w1'][j], p['od_w2'][j],
                           p['od_a0'][j], p['od_a1'][j], p['od_a2'][j], p['od_g1'][j], p['od_g2'][j],
                           p['od_k_k'][j], p['od_k_a'][j], p['od_r_k'][j], p['od_lnx_w'][j], p['od_lnx_b'][j],
                           p['od_w_out'][j])
        x = _layer_norm(DN_ALPHA * x + h, p['ln_w'][layer, 0], p['ln_b'][layer, 0])
        h = _cross_attn(x, mem, p['ca_w_q'][layer], p['ca_w_kv'][layer], p['ca_w_out'][layer])
        x = _layer_norm(DN_ALPHA * x + h, p['ln_w'][layer, 1], p['ln_b'][layer, 1])
        h = _expert_choice_ffn(x, p['moe_router'][layer], p['moe_w_in'][layer], p['moe_w_out'][layer])
        x = _layer_norm(DN_ALPHA * x + h, p['ln_w'][layer, 2], p['ln_b'][layer, 2])
    return x


_MATMUL_WEIGHTS = ('ev_w_in', 'ev_w_out', 'od_w_rkv', 'od_w1', 'od_w2', 'od_a1', 'od_a2', 'od_g1', 'od_g2',
                   'od_w_out', 'ca_w_q', 'ca_w_kv', 'ca_w_out', 'moe_router', 'moe_w_in', 'moe_w_out')


def kernel(x_prompt, x_sample, mem_prompt, mem_sample, ev_w_in, ev_lb_logits, ev_norm_a, ev_norm_b, ev_w_out, od_mu, od_w_rkv, od_w0, od_w1, od_w2, od_a0, od_a1, od_a2, od_g1, od_g2, od_k_k, od_k_a, od_r_k, od_lnx_w, od_lnx_b, od_w_out, ca_w_q, ca_w_kv, ca_w_out, moe_router, moe_w_in, moe_w_out, ln_w, ln_b):
    params = dict(ev_w_in=ev_w_in, ev_lb_logits=ev_lb_logits, ev_norm_a=ev_norm_a, ev_norm_b=ev_norm_b,
                  ev_w_out=ev_w_out, od_mu=od_mu, od_w_rkv=od_w_rkv, od_w0=od_w0, od_w1=od_w1, od_w2=od_w2,
                  od_a0=od_a0, od_a1=od_a1, od_a2=od_a2, od_g1=od_g1, od_g2=od_g2, od_k_k=od_k_k,
                  od_k_a=od_k_a, od_r_k=od_r_k, od_lnx_w=od_lnx_w, od_lnx_b=od_lnx_b, od_w_out=od_w_out,
                  ca_w_q=ca_w_q, ca_w_kv=ca_w_kv, ca_w_out=ca_w_out, moe_router=moe_router,
                  moe_w_in=moe_w_in, moe_w_out=moe_w_out, ln_w=ln_w, ln_b=ln_b)
    for name in _MATMUL_WEIGHTS:
        params[name] = params[name].astype(BF16)
    y_prompt = _trunk(x_prompt, mem_prompt, params)
    y_sample = _trunk(x_sample, mem_sample, params)
    return (y_prompt, y_sample)
```

```python
import functools
import math

import jax
import jax.numpy as jnp
from jax import lax
from jax.experimental import pallas as pl
from jax.experimental.pallas import tpu as pltpu

F32 = jnp.float32
BF16 = jnp.bfloat16

D_MODEL = 1024
DEPTH = 4
D_A = D_MODEL // 2
HEAD_A = 128
H_A = D_A // HEAD_A
CHUNK_A = 32
D_B = D_MODEL // 2
HEAD_B = 128
H_B = D_B // HEAD_B
CHUNK_B = 128
ROPE_BASE = 10000.0
HEAD_C = 64
H_C = D_MODEL // HEAD_C
LNX_EPS = 64e-5
H_CA = 4
HEAD_CA = D_MODEL // H_CA
N_EXPERTS = 16
D_EXPERT = 2048
CAP_FACTOR = 2
DN_ALPHA = (2.0 * DEPTH) ** 0.25
LN_EPS = 1e-5

VMEM_LIMIT = 48 * 1024 * 1024


def _cparams(*sem):
    return pltpu.CompilerParams(dimension_semantics=sem, vmem_limit_bytes=VMEM_LIMIT)


def _mm_kernel(x_ref, w_ref, o_ref):
    o_ref[...] = jnp.dot(x_ref[...].astype(BF16), w_ref[...],
                         preferred_element_type=F32).astype(o_ref.dtype)


def _pick(n, pref):
    t = min(n, pref)
    while n % t:
        t //= 2
    return t


def pmm(x, w, tm=1024, tn=512):
    M, K = x.shape
    N = w.shape[1]
    tm = _pick(M, tm)
    tn = _pick(N, tn) if N % 128 == 0 else N
    return pl.pallas_call(
        _mm_kernel,
        grid=(M // tm, N // tn),
        in_specs=[pl.BlockSpec((tm, K), lambda i, j: (i, 0)),
                  pl.BlockSpec((K, tn), lambda i, j: (0, j))],
        out_specs=pl.BlockSpec((tm, tn), lambda i, j: (i, j)),
        out_shape=jax.ShapeDtypeStruct((M, N), F32),
        compiler_params=_cparams("parallel", "parallel"),
        name="dense_proj",
    )(x, w)


def _mm(x, w):
    lead = x.shape[:-1]
    return pmm(x.reshape(-1, x.shape[-1]), w).reshape(lead + (w.shape[1],))


def _ca_kernel(q_ref, k_ref, v_ref, o_ref):
    scale = HEAD_CA ** -0.5
    for h in range(H_CA):
        sl = slice(h * HEAD_CA, (h + 1) * HEAD_CA)
        q = q_ref[0, :, sl].astype(BF16)
        k = k_ref[0, :, sl].astype(BF16)
        v = v_ref[0, :, sl].astype(BF16)
        s = lax.dot_general(q, k, (((1,), (1,)), ((), ())), preferred_element_type=F32) * scale
        m = jnp.max(s, axis=-1, keepdims=True)
        e = jnp.exp(s - m)
        p = e / jnp.sum(e, axis=-1, keepdims=True)
        o_ref[0, :, sl] = jnp.dot(p.astype(BF16), v, preferred_element_type=F32)


def cross_attn_core(q, kv, tq=512):
    B, T, D = q.shape
    M = kv.shape[1]
    tq = _pick(T, tq)
    return pl.pallas_call(
        _ca_kernel,
        grid=(B, T // tq),
        in_specs=[pl.BlockSpec((1, tq, D), lambda b, i: (b, i, 0)),
                  pl.BlockSpec((1, M, D), lambda b, i: (b, 0, 0)),
                  pl.BlockSpec((1, M, D), lambda b, i: (b, 0, 1))],
        out_specs=pl.BlockSpec((1, tq, D), lambda b, i: (b, i, 0)),
        out_shape=jax.ShapeDtypeStruct((B, T, D), F32),
        compiler_params=_cparams("parallel", "parallel"),
        name="cross_attn",
    )(q, kv, kv)


def _moe_kernel(x_ref, wg_ref, wu_ref, wo_ref, gate_ref, o_ref, acc_ref, *, nf):
    f = pl.program_id(2)

    @pl.when(f == 0)
    def _():
        acc_ref[...] = jnp.zeros_like(acc_ref)

    x = x_ref[0].astype(BF16)
    hg = jnp.dot(x, wg_ref[0], preferred_element_type=F32)
    hu = jnp.dot(x, wu_ref[0], preferred_element_type=F32)
    a = (hg * jax.nn.sigmoid(hg) * hu).astype(BF16)
    acc_ref[...] += jnp.dot(a, wo_ref[0], preferred_element_type=F32)

    @pl.when(f == nf - 1)
    def _():
        o_ref[0] = acc_ref[...] * gate_ref[0]


def moe_ffn(xe, w_in, w_out, gate, tm=1024, tf=512):
    E, cap, D = xe.shape
    Fh = w_out.shape[1]
    tm = _pick(cap, tm)
    nf = Fh // tf
    return pl.pallas_call(
        functools.partial(_moe_kernel, nf=nf),
        grid=(E, cap // tm, nf),
        in_specs=[pl.BlockSpec((1, tm, D), lambda e, i, f: (e, i, 0)),
                  pl.BlockSpec((1, D, tf), lambda e, i, f: (e, 0, f)),
                  pl.BlockSpec((1, D, tf), lambda e, i, f: (e, 0, nf + f)),
                  pl.BlockSpec((1, tf, D), lambda e, i, f: (e, f, 0)),
                  pl.BlockSpec((1, tm, 1), lambda e, i, f: (e, i, 0))],
        out_specs=pl.BlockSpec((1, tm, D), lambda e, i, f: (e, i, 0)),
        out_shape=jax.ShapeDtypeStruct((E, cap, D), F32),
        scratch_shapes=[pltpu.VMEM((tm, D), F32)],
        compiler_params=_cparams("parallel", "parallel", "arbitrary"),
        name="moe_ffn",
    )(xe, w_in, w_in, w_out, gate)


_NN = (((1,), (0,)), ((), ()))
_NT = (((1,), (1,)), ((), ()))
_TN = (((0,), (0,)), ((), ()))


def _bdot(a, b, dims=_NN):
    return lax.dot_general(a.astype(BF16), b.astype(BF16), dims, preferred_element_type=F32)


def _split3(x):
    hi = x.astype(BF16)
    r1 = x - hi.astype(F32)
    mid = r1.astype(BF16)
    lo = (r1 - mid.astype(F32)).astype(BF16)
    return hi, mid, lo


def _cumdot(cum, x):
    hi, mid, lo = _split3(x)
    return (lax.dot_general(cum, hi, _NN, preferred_element_type=F32)
            + lax.dot_general(cum, mid, _NN, preferred_element_type=F32)
            + lax.dot_general(cum, lo, _NN, preferred_element_type=F32))


RW_L = 64
RW_TT = 512
RW_W = 128


def _rwkv_kernel(r_ref, v_ref, kk_ref, lw_ref, kd_ref, bb_ref, o_ref, m_ref, om_ref, th_ref, *, reverse):
    L = RW_L
    nc = RW_TT // L
    S = 2 * L

    @pl.when(pl.program_id(2) == 0)
    def _():
        m_ref[...] = jnp.zeros_like(m_ref)

    row = lax.broadcasted_iota(jnp.int32, (S, S), 0)
    col = lax.broadcasted_iota(jnp.int32, (S, S), 1)
    if reverse:
        strict, incl = col > row, col >= row
    else:
        strict, incl = col < row, col <= row
    incl2 = jnp.concatenate([incl, incl], axis=1)
    eye = (row == col).astype(F32)
    cr = lax.broadcasted_iota(jnp.int32, (L, L), 0)
    cc = lax.broadcasted_iota(jnp.int32, (L, L), 1)
    cum = ((cc >= cr) if reverse else (cc <= cr)).astype(BF16)
    lane = lax.broadcasted_iota(jnp.int32, (L, RW_W), 1)
    m0 = lane < HEAD_C

    def stack(x):
        return jnp.concatenate([jnp.where(m0, x, 0.0), jnp.where(m0, 0.0, x)], axis=0)

    cs = range(nc)
    ld = lambda ref, c: ref[0, pl.ds(c * L, L), :]
    lws = [ld(lw_ref, c) for c in cs]
    gs = [_cumdot(cum, lws[c]) for c in cs]
    gls = [(g[0:1, :] if reverse else g[L - 1:L, :]) for g in gs]
    lhs, rhs_a, kbl, vss, rgs, ags = [], [], [], [], [], []
    for c in cs:
        g, gl = gs[c], gls[c]
        eg, eng, egl = jnp.exp(g), jnp.exp(-g), jnp.exp(gl - g)
        kd, bb = ld(kd_ref, c), ld(bb_ref, c)
        rg_s, ag_s = stack(ld(r_ref, c) * eg), stack(ld(kk_ref, c) * jnp.exp(g - lws[c]))
        rgs.append(rg_s)
        ags.append(ag_s)
        lhs.append(jnp.concatenate([ag_s, rg_s], axis=0))
        rhs_a.append(jnp.concatenate([stack(kd * eng), stack(bb * eng)], axis=0))
        kbl.append(jnp.concatenate([stack(kd * egl), stack(bb * egl)], axis=0))
        vss.append(stack(ld(v_ref, c)))
    amats = [_bdot(lhs[c], rhs_a[c], _NT) for c in cs]
    aaks = [jnp.where(strict, a[0:S, 0:S], 0.0) for a in amats]
    ns = [jnp.where(strict, a[0:S, S:2 * S], 0.0) for a in amats]
    arkb = [jnp.where(incl2, a[S:2 * S, :], 0.0) for a in amats]
    ts = [eye + n for n in ns]
    ps = [_bdot(n, n) for n in ns]
    avs = [_bdot(aaks[c], vss[c]) for c in cs]
    for _ in range(4):
        pps = [_bdot(ps[c], jnp.concatenate([ps[c], ts[c]], axis=1)) for c in cs]
        ts = [ts[c] + pps[c][:, S:2 * S] for c in cs]
        ps = [pps[c][:, 0:S] for c in cs]
    ts = [ts[c] + _bdot(ps[c], ts[c]) for c in cs]
    w12s = [_bdot(ts[c], jnp.concatenate([ags[c], avs[c]], axis=1)) for c in cs]
    rhss = [jnp.concatenate([jnp.concatenate([jnp.zeros_like(vss[c]), vss[c]], axis=1), w12s[c]], axis=0) for c in cs]
    sts = [_bdot(kbl[c], rhss[c], _TN) for c in cs]
    ots = [_bdot(arkb[c], rhss[c]) for c in cs]
    for c in cs:
        phi = sts[c][:, 0:RW_W] + eye * jnp.exp(gls[c])
        om_ref[c] = jnp.concatenate([ots[c][:, 0:RW_W] + rgs[c], phi], axis=0)
        th_ref[c] = jnp.concatenate([ots[c][:, RW_W:2 * RW_W], sts[c][:, RW_W:2 * RW_W]], axis=0)

    m = m_ref[...]
    for j in range(nc):
        c = nc - 1 - j if reverse else j
        res = _bdot(om_ref[c], m) + th_ref[c]
        o_ref[0, pl.ds(c * L, L), :] = res[0:L] + res[L:S]
        m = res[S:2 * S]
    m_ref[...] = m


def rwkv7_scan(r, v, kk, lw, kd, bb, reverse):
    B, T, D = r.shape
    nt = T // RW_TT
    tmap = (lambda b, h, i: (b, nt - 1 - i, h)) if reverse else (lambda b, h, i: (b, i, h))
    spec = pl.BlockSpec((1, RW_TT, RW_W), tmap)
    nc = RW_TT // RW_L
    return pl.pallas_call(
        functools.partial(_rwkv_kernel, reverse=reverse),
        grid=(B, D // RW_W, nt),
        in_specs=[spec] * 6,
        out_specs=spec,
        out_shape=jax.ShapeDtypeStruct((B, T, D), F32),
        scratch_shapes=[pltpu.VMEM((RW_W, RW_W), F32),
                        pltpu.VMEM((nc, 4 * RW_L, RW_W), F32),
                        pltpu.VMEM((nc, 4 * RW_L, RW_W), F32)],
        compiler_params=_cparams("parallel", "parallel", "arbitrary"),
        name="rwkv7_bwd" if reverse else "rwkv7_fwd",
    )(r, v, kk, lw, kd, bb)


HG_SUB = 16
HG_GRP = 128
HG_TT = 512


def _hgrn2_kernel(q_ref, v_ref, z_ref, lb_ref, o_ref, st_ref, *, reverse):
    C, G = HG_SUB, HG_GRP

    @pl.when(pl.program_id(2) == 0)
    def _():
        st_ref[...] = jnp.zeros_like(st_ref)

    log_lb, log_1mlb, one_mlb = lb_ref[0, 0:1, :], lb_ref[0, 1:2, :], lb_ref[0, 2:3, :]
    row = lax.broadcasted_iota(jnp.int32, (G, G), 0)
    col = lax.broadcasted_iota(jnp.int32, (G, G), 1)
    same = (row // C) == (col // C)
    cum = (same & ((col >= row) if reverse else (col <= row))).astype(BF16)
    sub_t = lax.broadcasted_iota(jnp.int32, (C, HEAD_A), 0)

    def group(gi, carry):
        g0 = (HG_TT // G - 1 - gi) if reverse else gi
        rows = pl.ds(pl.multiple_of(g0 * G, G), G)
        z = z_ref[0, rows, :]
        a_q = q_ref[0, rows, :]
        vv = v_ref[0, rows, :]
        ls = jnp.minimum(z, 0.0) - jnp.log1p(jnp.exp(-jnp.abs(z)))
        t2 = log_1mlb + ls
        mx = jnp.maximum(log_lb, t2)
        lf = mx + jnp.log1p(jnp.exp(-jnp.abs(log_lb - t2)))
        kk = one_mlb * (1.0 / (1.0 + jnp.exp(z)))
        qq = a_q * (1.0 / (1.0 + jnp.exp(-a_q)))
        b = _cumdot(cum, lf)
        st = st_ref[...]
        order = range(G // C - 1, -1, -1) if reverse else range(G // C)
        outs = [None] * (G // C)
        for j in order:
            sl = slice(j * C, (j + 1) * C)
            bj, qj, kj, vj = b[sl], qq[sl], kk[sl], vv[sl]
            e_in = 0 if reverse else C - 1
            b_end = bj[e_in:e_in + 1, :]
            o = _bdot(qj * jnp.exp(bj), st, _NT)
            for s in range(C):
                keep = (sub_t <= s) if reverse else (sub_t >= s)
                dec = jnp.exp(jnp.where(keep, bj - bj[s:s + 1, :], -1e30))
                sc = jnp.sum(dec * (qj * kj[s:s + 1, :]), axis=-1, keepdims=True)
                o = o + sc * vj[s:s + 1, :]
            outs[j] = o
            st = st * jnp.exp(b_end) + _bdot(vj, kj * jnp.exp(b_end - bj), _TN)
        st_ref[...] = st
        o_ref[0, rows, :] = jnp.concatenate(outs, axis=0)
        return carry

    lax.fori_loop(0, HG_TT // G, group, 0)


def hgrn2_scan(proj, lbp, reverse):
    B, T, _ = proj.shape
    nt = T // HG_TT
    tix = (lambda i: nt - 1 - i) if reverse else (lambda i: i)
    zoff = 3 * H_A if reverse else 2 * H_A
    d = 1 if reverse else 0
    blk = (1, HG_TT, HEAD_A)
    return pl.pallas_call(
        functools.partial(_hgrn2_kernel, reverse=reverse),
        grid=(B, H_A, nt),
        in_specs=[pl.BlockSpec(blk, lambda b, h, i: (b, tix(i), h)),
                  pl.BlockSpec(blk, lambda b, h, i: (b, tix(i), H_A + h)),
                  pl.BlockSpec(blk, lambda b, h, i: (b, tix(i), zoff + h)),
                  pl.BlockSpec((1, 3, HEAD_A), lambda b, h, i: (d, 0, h))],
        out_specs=pl.BlockSpec(blk, lambda b, h, i: (b, tix(i), h)),
        out_shape=jax.ShapeDtypeStruct((B, T, D_A), F32),
        scratch_shapes=[pltpu.VMEM((HEAD_A, HEAD_A), F32)],
        compiler_params=_cparams("parallel", "parallel", "arbitrary"),
        name="hgrn2_bwd" if reverse else "hgrn2_fwd",
    )(proj, proj, proj, lbp)


RT_TT = 256


def _ret_fwd_kernel(lg_ref, q_ref, k_ref, v_ref, o_ref, st_ref):
    Tt = RT_TT

    @pl.when(pl.program_id(2) == 0)
    def _():
        st_ref[...] = jnp.zeros_like(st_ref)

    lgw = lg_ref[0]
    lg = lgw[:, 0:HEAD_B]
    q = q_ref[0]
    k = k_ref[0]
    v = v_ref[0]
    row = lax.broadcasted_iota(jnp.int32, (Tt, Tt), 0)
    col = lax.broadcasted_iota(jnp.int32, (Tt, Tt), 1)
    dmat = jnp.exp(lgw * jnp.abs(row - col).astype(F32))
    pos = lax.broadcasted_iota(jnp.int32, (Tt, 1), 0).astype(F32)
    st = st_ref[...]
    o = _bdot(_bdot(q, k, _NT) * dmat, v)
    o_ref[0] = o + _bdot(q * jnp.exp(lg * (pos + 1.0)), st)
    st_ref[...] = st * jnp.exp(lg * Tt) + _bdot(k * jnp.exp(lg * (Tt - 1.0 - pos)), v, _TN)


def _ret_bwd_kernel(lg_ref, q_ref, k_ref, v_ref, oin_ref, o_ref, st_ref):
    Tt = RT_TT

    @pl.when(pl.program_id(2) == 0)
    def _():
        st_ref[...] = jnp.zeros_like(st_ref)

    lg = lg_ref[0][:, 0:HEAD_B]
    pos = lax.broadcasted_iota(jnp.int32, (Tt, 1), 0).astype(F32)
    st = st_ref[...]
    o_ref[0] = oin_ref[0] + _bdot(q_ref[0] * jnp.exp(lg * (Tt - pos)), st)
    st_ref[...] = st * jnp.exp(lg * Tt) + _bdot(k_ref[0] * jnp.exp(lg * pos), v_ref[0], _TN)


def retention(q, k, proj):
    B, T, D = q.shape
    nt = T // RT_TT
    blk = (1, RT_TT, HEAD_B)
    voff = (5 * D_A + 2 * D_B) // HEAD_B
    fmap = lambda b, h, i: (b, i, h)
    rmap = lambda b, h, i: (b, nt - 1 - i, h)
    lgam = jnp.log1p(-jnp.power(2.0, -5.0 - jnp.arange(H_B, dtype=F32)))
    lgam = jnp.broadcast_to(lgam[:, None, None], (H_B, 1, RT_TT))
    lspec = pl.BlockSpec((1, 1, RT_TT), lambda b, h, i: (h, 0, 0))
    o1 = pl.pallas_call(
        _ret_fwd_kernel, grid=(B, H_B, nt),
        in_specs=[lspec, pl.BlockSpec(blk, fmap), pl.BlockSpec(blk, fmap),
                  pl.BlockSpec(blk, lambda b, h, i: (b, i, voff + h))],
        out_specs=pl.BlockSpec(blk, fmap),
        out_shape=jax.ShapeDtypeStruct((B, T, D), F32),
        scratch_shapes=[pltpu.VMEM((HEAD_B, HEAD_B), F32)],
        compiler_params=_cparams("parallel", "parallel", "arbitrary"), name="retention_fwd")(lgam, q, k, proj)
    return pl.pallas_call(
        _ret_bwd_kernel, grid=(B, H_B, nt),
        in_specs=[lspec, pl.BlockSpec(blk, rmap), pl.BlockSpec(blk, rmap),
                  pl.BlockSpec(blk, lambda b, h, i: (b, nt - 1 - i, voff + h)), pl.BlockSpec(blk, rmap)],
        out_specs=pl.BlockSpec(blk, rmap),
        out_shape=jax.ShapeDtypeStruct((B, T, D), F32),
        scratch_shapes=[pltpu.VMEM((HEAD_B, HEAD_B), F32)],
        input_output_aliases={4: 0},
        compiler_params=_cparams("parallel", "parallel", "arbitrary"), name="retention_bwd")(lgam, q, k, proj, o1)


def _layer_norm(x, w, b):
    mu = jnp.mean(x, -1, keepdims=True)
    var = jnp.mean(jnp.square(x - mu), -1, keepdims=True)
    return (x - mu) * lax.rsqrt(var + LN_EPS) * w + b


def _head_norm(o, gain, bias, eps, center):
    of = o
    if center:
        of = of - jnp.mean(of, -1, keepdims=True)
    of = of * lax.rsqrt(jnp.mean(jnp.square(of), -1, keepdims=True) + eps)
    of = of.reshape(of.shape[:-2] + (-1,)) * gain
    if bias is not None:
        of = of + bias
    return of


def _rotary(t, pos):
    d = t.shape[-1]
    theta = 1.0 / jnp.power(ROPE_BASE, jnp.linspace(0.0, 1.0, d // 2, dtype=F32))
    ang = pos[:, None] * theta[None, :]
    cos = jnp.cos(ang)[None, :, None, :]
    sin = jnp.sin(ang)[None, :, None, :]
    t1, t2 = t[..., 0::2], t[..., 1::2]
    return jnp.stack([t1 * cos - t2 * sin, t2 * cos + t1 * sin], -1).reshape(t.shape)


def _even_mixer(x, w_in, lb, norm_a, norm_b, w_out):
    Bn, T, _ = x.shape
    proj = _mm(x, w_in)
    a_g = proj[..., 4 * D_A:5 * D_A]
    b_q = proj[..., 5 * D_A:5 * D_A + D_B]
    b_k = proj[..., 5 * D_A + D_B:5 * D_A + 2 * D_B]
    b_g = proj[..., 5 * D_A + 3 * D_B:]

    lbp = jnp.stack([jnp.log(lb), jnp.log1p(-lb), 1.0 - lb], axis=1)
    o_a = hgrn2_scan(proj, lbp, False) + hgrn2_scan(proj, lbp, True)
    o_a = _head_norm(o_a.reshape(Bn, T, H_A, HEAD_A), norm_a, None, 1e-6, False) * jax.nn.silu(a_g)

    heads_b = lambda t: t.reshape(t.shape[:-1] + (H_B, HEAD_B))
    pos = jnp.arange(T, dtype=F32)
    qb = _rotary(heads_b(b_q), pos).reshape(Bn, T, D_B)
    kb = (_rotary(heads_b(b_k), pos) * (HEAD_B ** -0.5)).reshape(Bn, T, D_B)
    o_b = retention(qb, kb, proj)
    o_b = _head_norm(heads_b(o_b), norm_b, None, 1e-6, True) * jax.nn.silu(b_g)

    return _mm(jnp.concatenate([o_a, o_b], -1), w_out)


def _odd_mixer(x, mu, w_rkv, w0, w1, w2, a0, a1, a2, g1, g2, k_k, k_a, r_k, lnx_w, lnx_b, w_out):
    Bn, T, D = x.shape
    zero = jnp.zeros_like(x[:, :1])
    x_prev = jnp.concatenate([zero, x[:, :-1]], 1)
    x_next = jnp.concatenate([x[:, 1:], zero], 1)
    xx = 0.5 * (x_prev + x_next) - x
    xm = x[None] + xx[None] * mu[:, None, None, :]
    r, k, v = _mm(xm[0], w_rkv[0]), _mm(xm[1], w_rkv[1]), _mm(xm[2], w_rkv[2])
    xw, xa, xg = xm[3], xm[4], xm[5]
    lora_w = jnp.stack([_mm(jnp.tanh(_mm(xw, w1[n])), w2[n]) for n in range(2)])
    w_log = -jax.nn.softplus(-(w0[:, None, None, :] + lora_w)) - 0.5
    lw = -jnp.exp(w_log)
    a = jax.nn.sigmoid(a0[:, None, None, :] + jnp.stack([_mm(_mm(xa, a1[n]), a2[n]) for n in range(2)]))
    g = _mm(jax.nn.sigmoid(_mm(xg, g1)), g2)

    heads = lambda t: t.reshape(t.shape[:-1] + (H_C, HEAD_C))
    kk = heads(k * k_k)
    kk = (kk / jnp.maximum(jnp.sqrt(jnp.sum(jnp.square(kk), -1, keepdims=True)), 1e-12)).reshape(Bn, T, D)
    k_dir = k[None] * (1.0 + (a - 1.0) * k_a)
    o = rwkv7_scan(r, v, kk, lw[0], k_dir[0], -(kk * a[0]), False) + \
        rwkv7_scan(r, v, kk, lw[1], k_dir[1], -(kk * a[1]), True)
    o = _head_norm(heads(o), lnx_w, lnx_b, LNX_EPS, True)
    r_h, v_h = heads(r), heads(v)
    bonus = jnp.sum(r_h[None] * heads(k_dir) * heads(r_k), -1, keepdims=True) * v_h[None]
    bonus = (bonus[0] + bonus[1]).reshape(Bn, T, D)
    return _mm((o + bonus) * g, w_out)


def _cross_attn(x, mem, w_q, w_kv, w_out):
    q = _mm(x, w_q)
    kv = _mm(mem, w_kv)
    return _mm(cross_attn_core(q, kv), w_out)


def _expert_choice_ffn(x, w_router, w_in, w_out):
    Bn, T, D = x.shape
    n = Bn * T
    cap = (CAP_FACTOR * n) // N_EXPERTS
    xt = x.reshape(n, D)
    aff = jax.nn.softmax(pmm(xt, w_router), axis=-1)
    gate, idx = lax.top_k(aff.T, cap)
    xe = xt[idx]
    ye = moe_ffn(xe, w_in, w_out, gate[..., None])
    y = jnp.zeros_like(xt).at[idx.reshape(-1)].add(ye.reshape(-1, D))
    return y.reshape(Bn, T, D)


def _trunk(x, mem, p):
    p_lb = jax.nn.softmax(p['ev_lb_logits'], axis=1)
    lb_all = jnp.clip(jnp.cumsum(p_lb, axis=1) - p_lb[:, :1], 0.0, 1.0)
    for layer in range(DEPTH):
        j = layer // 2
        if layer % 2 == 0:
            h = _even_mixer(x, p['ev_w_in'][j], lb_all[:, j], p['ev_norm_a'][j], p['ev_norm_b'][j], p['ev_w_out'][j])
        else:
            h = _odd_mixer(x, p['od_mu'][j], p['od_w_rkv'][j], p['od_w0'][j], p['od_w1'][j], p['od_w2'][j],
                           p['od_a0'][j], p['od_a1'][j], p['od_a2'][j], p['od_g1'][j], p['od_g2'][j],
                           p['od_k_k'][j], p['od_k_a'][j], p['od_r_k'][j], p['od_lnx_w'][j], p['od_lnx_b'][j],
                           p['od_w_out'][j])
        x = _layer_norm(DN_ALPHA * x + h, p['ln_w'][layer, 0], p['ln_b'][layer, 0])
        h = _cross_attn(x, mem, p['ca_w_q'][layer], p['ca_w_kv'][layer], p['ca_w_out'][layer])
        x = _layer_norm(DN_ALPHA * x + h, p['ln_w'][layer, 1], p['ln_b'][layer, 1])
        h = _expert_choice_ffn(x, p['moe_router'][layer], p['moe_w_in'][layer], p['moe_w_out'][layer])
        x = _layer_norm(DN_ALPHA * x + h, p['ln_w'][layer, 2], p['ln_b'][layer, 2])
    return x


_MATMUL_WEIGHTS = ('ev_w_in', 'ev_w_out', 'od_w_rkv', 'od_w1', 'od_w2', 'od_a1', 'od_a2', 'od_g1', 'od_g2',
                   'od_w_out', 'ca_w_q', 'ca_w_kv', 'ca_w_out', 'moe_router', 'moe_w_in', 'moe_w_out')


def kernel(x_prompt, x_sample, mem_prompt, mem_sample, ev_w_in, ev_lb_logits, ev_norm_a, ev_norm_b, ev_w_out, od_mu, od_w_rkv, od_w0, od_w1, od_w2, od_a0, od_a1, od_a2, od_g1, od_g2, od_k_k, od_k_a, od_r_k, od_lnx_w, od_lnx_b, od_w_out, ca_w_q, ca_w_kv, ca_w_out, moe_router, moe_w_in, moe_w_out, ln_w, ln_b):
    params = dict(ev_w_in=ev_w_in, ev_lb_logits=ev_lb_logits, ev_norm_a=ev_norm_a, ev_norm_b=ev_norm_b,
                  ev_w_out=ev_w_out, od_mu=od_mu, od_w_rkv=od_w_rkv, od_w0=od_w0, od_w1=od_w1, od_w2=od_w2,
                  od_a0=od_a0, od_a1=od_a1, od_a2=od_a2, od_g1=od_g1, od_g2=od_g2, od_k_k=od_k_k,
                  od_k_a=od_k_a, od_r_k=od_r_k, od_lnx_w=od_lnx_w, od_lnx_b=od_lnx_b, od_w_out=od_w_out,
                  ca_w_q=ca_w_q, ca_w_kv=ca_w_kv, ca_w_out=ca_w_out, moe_router=moe_router,
                  moe_w_in=moe_w_in, moe_w_out=moe_w_out, ln_w=ln_w, ln_b=ln_b)
    for name in _MATMUL_WEIGHTS:
        params[name] = params[name].astype(BF16)
    y_prompt = _trunk(x_prompt, mem_prompt, params)
    y_sample = _trunk(x_sample, mem_sample, params)
    return (y_prompt, y_sample)
```

```python
import functools
import math

import jax
import jax.numpy as jnp
from jax import lax
from jax.experimental import pallas as pl
from jax.experimental.pallas import tpu as pltpu

F32 = jnp.float32
BF16 = jnp.bfloat16

D_MODEL = 1024
DEPTH = 4
D_A = D_MODEL // 2
HEAD_A = 128
H_A = D_A // HEAD_A
CHUNK_A = 32
D_B = D_MODEL // 2
HEAD_B = 128
H_B = D_B // HEAD_B
CHUNK_B = 128
ROPE_BASE = 10000.0
HEAD_C = 64
H_C = D_MODEL // HEAD_C
LNX_EPS = 64e-5
H_CA = 4
HEAD_CA = D_MODEL // H_CA
N_EXPERTS = 16
D_EXPERT = 2048
CAP_FACTOR = 2
DN_ALPHA = (2.0 * DEPTH) ** 0.25
LN_EPS = 1e-5

VMEM_LIMIT = 48 * 1024 * 1024


def _cparams(*sem):
    return pltpu.CompilerParams(dimension_semantics=sem, vmem_limit_bytes=VMEM_LIMIT)


def _mm_kernel(x_ref, w_ref, o_ref):
    o_ref[...] = jnp.dot(x_ref[...].astype(BF16), w_ref[...],
                         preferred_element_type=F32).astype(o_ref.dtype)


def _pick(n, pref):
    t = min(n, pref)
    while n % t:
        t //= 2
    return t


def pmm(x, w, tm=1024, tn=512):
    M, K = x.shape
    N = w.shape[1]
    tm = _pick(M, tm)
    tn = _pick(N, tn) if N % 128 == 0 else N
    return pl.pallas_call(
        _mm_kernel,
        grid=(M // tm, N // tn),
        in_specs=[pl.BlockSpec((tm, K), lambda i, j: (i, 0)),
                  pl.BlockSpec((K, tn), lambda i, j: (0, j))],
        out_specs=pl.BlockSpec((tm, tn), lambda i, j: (i, j)),
        out_shape=jax.ShapeDtypeStruct((M, N), F32),
        compiler_params=_cparams("parallel", "parallel"),
        name="dense_proj",
    )(x, w)


def _mm(x, w):
    lead = x.shape[:-1]
    return pmm(x.reshape(-1, x.shape[-1]), w).reshape(lead + (w.shape[1],))


def _ca_kernel(q_ref, k_ref, v_ref, o_ref):
    scale = HEAD_CA ** -0.5
    for h in range(H_CA):
        sl = slice(h * HEAD_CA, (h + 1) * HEAD_CA)
        q = q_ref[0, :, sl].astype(BF16)
        k = k_ref[0, :, sl].astype(BF16)
        v = v_ref[0, :, sl].astype(BF16)
        s = lax.dot_general(q, k, (((1,), (1,)), ((), ())), preferred_element_type=F32) * scale
        m = jnp.max(s, axis=-1, keepdims=True)
        e = jnp.exp(s - m)
        p = e / jnp.sum(e, axis=-1, keepdims=True)
        o_ref[0, :, sl] = jnp.dot(p.astype(BF16), v, preferred_element_type=F32)


def cross_attn_core(q, kv, tq=512):
    B, T, D = q.shape
    M = kv.shape[1]
    tq = _pick(T, tq)
    return pl.pallas_call(
        _ca_kernel,
        grid=(B, T // tq),
        in_specs=[pl.BlockSpec((1, tq, D), lambda b, i: (b, i, 0)),
                  pl.BlockSpec((1, M, D), lambda b, i: (b, 0, 0)),
                  pl.BlockSpec((1, M, D), lambda b, i: (b, 0, 1))],
        out_specs=pl.BlockSpec((1, tq, D), lambda b, i: (b, i, 0)),
        out_shape=jax.ShapeDtypeStruct((B, T, D), F32),
        compiler_params=_cparams("parallel", "parallel"),
        name="cross_attn",
    )(q, kv, kv)


def _moe_kernel(x_ref, wg_ref, wu_ref, wo_ref, gate_ref, o_ref, acc_ref, *, nf):
    f = pl.program_id(2)

    @pl.when(f == 0)
    def _():
        acc_ref[...] = jnp.zeros_like(acc_ref)

    x = x_ref[0].astype(BF16)
    hg = jnp.dot(x, wg_ref[0], preferred_element_type=F32)
    hu = jnp.dot(x, wu_ref[0], preferred_element_type=F32)
    a = (hg * jax.nn.sigmoid(hg) * hu).astype(BF16)
    acc_ref[...] += jnp.dot(a, wo_ref[0], preferred_element_type=F32)

    @pl.when(f == nf - 1)
    def _():
        o_ref[0] = acc_ref[...] * gate_ref[0]


def moe_ffn(xe, w_in, w_out, gate, tm=1024, tf=512):
    E, cap, D = xe.shape
    Fh = w_out.shape[1]
    tm = _pick(cap, tm)
    nf = Fh // tf
    return pl.pallas_call(
        functools.partial(_moe_kernel, nf=nf),
        grid=(E, cap // tm, nf),
        in_specs=[pl.BlockSpec((1, tm, D), lambda e, i, f: (e, i, 0)),
                  pl.BlockSpec((1, D, tf), lambda e, i, f: (e, 0, f)),
                  pl.BlockSpec((1, D, tf), lambda e, i, f: (e, 0, nf + f)),
                  pl.BlockSpec((1, tf, D), lambda e, i, f: (e, f, 0)),
                  pl.BlockSpec((1, tm, 1), lambda e, i, f: (e, i, 0))],
        out_specs=pl.BlockSpec((1, tm, D), lambda e, i, f: (e, i, 0)),
        out_shape=jax.ShapeDtypeStruct((E, cap, D), F32),
        scratch_shapes=[pltpu.VMEM((tm, D), F32)],
        compiler_params=_cparams("parallel", "parallel", "arbitrary"),
        name="moe_ffn",
    )(xe, w_in, w_in, w_out, gate)


_NN = (((1,), (0,)), ((), ()))
_NT = (((1,), (1,)), ((), ()))
_TN = (((0,), (0,)), ((), ()))


def _bdot(a, b, dims=_NN):
    return lax.dot_general(a.astype(BF16), b.astype(BF16), dims, preferred_element_type=F32)


def _split3(x):
    hi = x.astype(BF16)
    r1 = x - hi.astype(F32)
    mid = r1.astype(BF16)
    lo = (r1 - mid.astype(F32)).astype(BF16)
    return hi, mid, lo


def _cumdot(cum, x):
    hi, mid, lo = _split3(x)
    return (lax.dot_general(cum, hi, _NN, preferred_element_type=F32)
            + lax.dot_general(cum, mid, _NN, preferred_element_type=F32)
            + lax.dot_general(cum, lo, _NN, preferred_element_type=F32))


RW_L = 64
RW_TT = 512
RW_W = 128


def _rwkv_kernel(r_ref, v_ref, kk_ref, lw_ref, kd_ref, bb_ref, o_ref, m_ref, om_ref, th_ref, *, reverse):
    L = RW_L
    nc = RW_TT // L
    S = 2 * L

    @pl.when(pl.program_id(2) == 0)
    def _():
        m_ref[...] = jnp.zeros_like(m_ref)

    row = lax.broadcasted_iota(jnp.int32, (S, S), 0)
    col = lax.broadcasted_iota(jnp.int32, (S, S), 1)
    if reverse:
        strict, incl = col > row, col >= row
    else:
        strict, incl = col < row, col <= row
    incl2 = jnp.concatenate([incl, incl], axis=1)
    eye = (row == col).astype(F32)
    cr = lax.broadcasted_iota(jnp.int32, (L, L), 0)
    cc = lax.broadcasted_iota(jnp.int32, (L, L), 1)
    cum = ((cc >= cr) if reverse else (cc <= cr)).astype(BF16)
    lane = lax.broadcasted_iota(jnp.int32, (L, RW_W), 1)
    m0 = lane < HEAD_C

    def stack(x):
        return jnp.concatenate([jnp.where(m0, x, 0.0), jnp.where(m0, 0.0, x)], axis=0)

    cs = range(nc)
    ld = lambda ref, c: ref[0, pl.ds(c * L, L), :]
    lws = [ld(lw_ref, c) for c in cs]
    gs = [_cumdot(cum, lws[c]) for c in cs]
    gls = [(g[0:1, :] if reverse else g[L - 1:L, :]) for g in gs]
    lhs, rhs_a, kbl, vss, rgs, ags = [], [], [], [], [], []
    for c in cs:
        g, gl = gs[c], gls[c]
        eg, eng, egl = jnp.exp(g), jnp.exp(-g), jnp.exp(gl - g)
        kd, bb = ld(kd_ref, c), ld(bb_ref, c)
        rg_s, ag_s = stack(ld(r_ref, c) * eg), stack(ld(kk_ref, c) * jnp.exp(g - lws[c]))
        rgs.append(rg_s)
        ags.append(ag_s)
        lhs.append(jnp.concatenate([ag_s, rg_s], axis=0))
        rhs_a.append(jnp.concatenate([stack(kd * eng), stack(bb * eng)], axis=0))
        kbl.append(jnp.concatenate([stack(kd * egl), stack(bb * egl)], axis=0))
        vss.append(stack(ld(v_ref, c)))
    amats = [_bdot(lhs[c], rhs_a[c], _NT) for c in cs]
    aaks = [jnp.where(strict, a[0:S, 0:S], 0.0) for a in amats]
    ns = [jnp.where(strict, a[0:S, S:2 * S], 0.0) for a in amats]
    arkb = [jnp.where(incl2, a[S:2 * S, :], 0.0) for a in amats]
    ts = [eye + n for n in ns]
    ps = [_bdot(n, n) for n in ns]
    avs = [_bdot(aaks[c], vss[c]) for c in cs]
    for _ in range(4):
        pps = [_bdot(ps[c], jnp.concatenate([ps[c], ts[c]], axis=1)) for c in cs]
        ts = [ts[c] + pps[c][:, S:2 * S] for c in cs]
        ps = [pps[c][:, 0:S] for c in cs]
    ts = [ts[c] + _bdot(ps[c], ts[c]) for c in cs]
    w12s = [_bdot(ts[c], jnp.concatenate([ags[c], avs[c]], axis=1)) for c in cs]
    rhss = [jnp.concatenate([jnp.concatenate([jnp.zeros_like(vss[c]), vss[c]], axis=1), w12s[c]], axis=0) for c in cs]
    sts = [_bdot(kbl[c], rhss[c], _TN) for c in cs]
    ots = [_bdot(arkb[c], rhss[c]) for c in cs]
    for c in cs:
        phi = sts[c][:, 0:RW_W] + eye * jnp.exp(gls[c])
        om_ref[c] = jnp.concatenate([ots[c][:, 0:RW_W] + rgs[c], phi], axis=0)
        th_ref[c] = jnp.concatenate([ots[c][:, RW_W:2 * RW_W], sts[c][:, RW_W:2 * RW_W]], axis=0)

    m = m_ref[...]
    for j in range(nc):
        c = nc - 1 - j if reverse else j
        res = _bdot(om_ref[c], m) + th_ref[c]
        o_ref[0, pl.ds(c * L, L), :] = res[0:L] + res[L:S]
        m = res[S:2 * S]
    m_ref[...] = m


def rwkv7_scan(r, v, kk, lw, kd, bb, reverse):
    B, T, D = r.shape
    nt = T // RW_TT
    tmap = (lambda b, h, i: (b, nt - 1 - i, h)) if reverse else (lambda b, h, i: (b, i, h))
    spec = pl.BlockSpec((1, RW_TT, RW_W), tmap)
    nc = RW_TT // RW_L
    return pl.pallas_call(
        functools.partial(_rwkv_kernel, reverse=reverse),
        grid=(B, D // RW_W, nt),
        in_specs=[spec] * 6,
        out_specs=spec,
        out_shape=jax.ShapeDtypeStruct((B, T, D), F32),
        scratch_shapes=[pltpu.VMEM((RW_W, RW_W), F32),
                        pltpu.VMEM((nc, 4 * RW_L, RW_W), F32),
                        pltpu.VMEM((nc, 4 * RW_L, RW_W), F32)],
        compiler_params=_cparams("parallel", "parallel", "arbitrary"),
        name="rwkv7_bwd" if reverse else "rwkv7_fwd",
    )(r, v, kk, lw, kd, bb)


HG_SUB = 16
HG_GRP = 128
HG_TT = 512


def _hgrn2_kernel(q_ref, v_ref, z_ref, lb_ref, o_ref, st_ref, *, reverse):
    C, G = HG_SUB, HG_GRP

    @pl.when(pl.program_id(2) == 0)
    def _():
        st_ref[...] = jnp.zeros_like(st_ref)

    log_lb, log_1mlb, one_mlb = lb_ref[0, 0:1, :], lb_ref[0, 1:2, :], lb_ref[0, 2:3, :]
    row = lax.broadcasted_iota(jnp.int32, (G, G), 0)
    col = lax.broadcasted_iota(jnp.int32, (G, G), 1)
    same = (row // C) == (col // C)
    cum = (same & ((col >= row) if reverse else (col <= row))).astype(BF16)
    sub_t = lax.broadcasted_iota(jnp.int32, (C, HEAD_A), 0)

    def group(gi, carry):
        g0 = (HG_TT // G - 1 - gi) if reverse else gi
        rows = pl.ds(pl.multiple_of(g0 * G, G), G)
        z = z_ref[0, rows, :]
        a_q = q_ref[0, rows, :]
        vv = v_ref[0, rows, :]
        ls = jnp.minimum(z, 0.0) - jnp.log1p(jnp.exp(-jnp.abs(z)))
        t2 = log_1mlb + ls
        mx = jnp.maximum(log_lb, t2)
        lf = mx + jnp.log1p(jnp.exp(-jnp.abs(log_lb - t2)))
        kk = one_mlb * (1.0 / (1.0 + jnp.exp(z)))
        qq = a_q * (1.0 / (1.0 + jnp.exp(-a_q)))
        b = _cumdot(cum, lf)
        st = st_ref[...]
        order = range(G // C - 1, -1, -1) if reverse else range(G // C)
        outs = [None] * (G // C)
        for j in order:
            sl = slice(j * C, (j + 1) * C)
            bj, qj, kj, vj = b[sl], qq[sl], kk[sl], vv[sl]
            e_in = 0 if reverse else C - 1
            b_end = bj[e_in:e_in + 1, :]
            o = _bdot(qj * jnp.exp(bj), st, _NT)
            for s in range(C):
                keep = (sub_t <= s) if reverse else (sub_t >= s)
                dec = jnp.exp(jnp.where(keep, bj - bj[s:s + 1, :], -1e30))
                sc = jnp.sum(dec * (qj * kj[s:s + 1, :]), axis=-1, keepdims=True)
                o = o + sc * vj[s:s + 1, :]
            outs[j] = o
            st = st * jnp.exp(b_end) + _bdot(vj, kj * jnp.exp(b_end - bj), _TN)
        st_ref[...] = st
        o_ref[0, rows, :] = jnp.concatenate(outs, axis=0)
        return carry

    lax.fori_loop(0, HG_TT // G, group, 0)


def hgrn2_scan(proj, lbp, reverse):
    B, T, _ = proj.shape
    nt = T // HG_TT
    tix = (lambda i: nt - 1 - i) if reverse else (lambda i: i)
    zoff = 3 * H_A if reverse else 2 * H_A
    d = 1 if reverse else 0
    blk = (1, HG_TT, HEAD_A)
    return pl.pallas_call(
        functools.partial(_hgrn2_kernel, reverse=reverse),
        grid=(B, H_A, nt),
        in_specs=[pl.BlockSpec(blk, lambda b, h, i: (b, tix(i), h)),
                  pl.BlockSpec(blk, lambda b, h, i: (b, tix(i), H_A + h)),
                  pl.BlockSpec(blk, lambda b, h, i: (b, tix(i), zoff + h)),
                  pl.BlockSpec((1, 3, HEAD_A), lambda b, h, i: (d, 0, h))],
        out_specs=pl.BlockSpec(blk, lambda b, h, i: (b, tix(i), h)),
        out_shape=jax.ShapeDtypeStruct((B, T, D_A), F32),
        scratch_shapes=[pltpu.VMEM((HEAD_A, HEAD_A), F32)],
        compiler_params=_cparams("parallel", "parallel", "arbitrary"),
        name="hgrn2_bwd" if reverse else "hgrn2_fwd",
    )(proj, proj, proj, lbp)


RT_TT = 256


def _ret_fwd_kernel(lg_ref, q_ref, k_ref, v_ref, o_ref, st_ref):
    Tt = RT_TT

    @pl.when(pl.program_id(2) == 0)
    def _():
        st_ref[...] = jnp.zeros_like(st_ref)

    lgw = lg_ref[0]
    lg = lgw[:, 0:HEAD_B]
    q = q_ref[0]
    k = k_ref[0]
    v = v_ref[0]
    row = lax.broadcasted_iota(jnp.int32, (Tt, Tt), 0)
    col = lax.broadcasted_iota(jnp.int32, (Tt, Tt), 1)
    dmat = jnp.exp(lgw * jnp.abs(row - col).astype(F32))
    pos = lax.broadcasted_iota(jnp.int32, (Tt, 1), 0).astype(F32)
    st = st_ref[...]
    o = _bdot(_bdot(q, k, _NT) * dmat, v)
    o_ref[0] = o + _bdot(q * jnp.exp(lg * (pos + 1.0)), st)
    st_ref[...] = st * jnp.exp(lg * Tt) + _bdot(k * jnp.exp(lg * (Tt - 1.0 - pos)), v, _TN)


def _ret_bwd_kernel(lg_ref, q_ref, k_ref, v_ref, oin_ref, o_ref, st_ref):
    Tt = RT_TT

    @pl.when(pl.program_id(2) == 0)
    def _():
        st_ref[...] = jnp.zeros_like(st_ref)

    lg = lg_ref[0][:, 0:HEAD_B]
    pos = lax.broadcasted_iota(jnp.int32, (Tt, 1), 0).astype(F32)
    st = st_ref[...]
    o_ref[0] = oin_ref[0] + _bdot(q_ref[0] * jnp.exp(lg * (Tt - pos)), st)
    st_ref[...] = st * jnp.exp(lg * Tt) + _bdot(k_ref[0] * jnp.exp(lg * pos), v_ref[0], _TN)


def retention(q, k, proj):
    B, T, D = q.shape
    nt = T // RT_TT
    blk = (1, RT_TT, HEAD_B)
    voff = (5 * D_A + 2 * D_B) // HEAD_B
    fmap = lambda b, h, i: (b, i, h)
    rmap = lambda b, h, i: (b, nt - 1 - i, h)
    lgam = jnp.log1p(-jnp.power(2.0, -5.0 - jnp.arange(H_B, dtype=F32)))
    lgam = jnp.broadcast_to(lgam[:, None, None], (H_B, 1, RT_TT))
    lspec = pl.BlockSpec((1, 1, RT_TT), lambda b, h, i: (h, 0, 0))
    o1 = pl.pallas_call(
        _ret_fwd_kernel, grid=(B, H_B, nt),
        in_specs=[lspec, pl.BlockSpec(blk, fmap), pl.BlockSpec(blk, fmap),
                  pl.BlockSpec(blk, lambda b, h, i: (b, i, voff + h))],
        out_specs=pl.BlockSpec(blk, fmap),
        out_shape=jax.ShapeDtypeStruct((B, T, D), F32),
        scratch_shapes=[pltpu.VMEM((HEAD_B, HEAD_B), F32)],
        compiler_params=_cparams("parallel", "parallel", "arbitrary"), name="retention_fwd")(lgam, q, k, proj)
    return pl.pallas_call(
        _ret_bwd_kernel, grid=(B, H_B, nt),
        in_specs=[lspec, pl.BlockSpec(blk, rmap), pl.BlockSpec(blk, rmap),
                  pl.BlockSpec(blk, lambda b, h, i: (b, nt - 1 - i, voff + h)), pl.BlockSpec(blk, rmap)],
        out_specs=pl.BlockSpec(blk, rmap),
        out_shape=jax.ShapeDtypeStruct((B, T, D), F32),
        scratch_shapes=[pltpu.VMEM((HEAD_B, HEAD_B), F32)],
        input_output_aliases={4: 0},
        compiler_params=_cparams("parallel", "parallel", "arbitrary"), name="retention_bwd")(lgam, q, k, proj, o1)


LANES = 128


def _head_ones():
    r = lax.broadcasted_iota(jnp.int32, (LANES, LANES), 0) // HEAD_C
    c = lax.broadcasted_iota(jnp.int32, (LANES, LANES), 1) // HEAD_C
    return (r == c).astype(BF16)


def _head_sum(x, bd):
    hi = x.astype(BF16)
    lo = (x - hi.astype(F32)).astype(BF16)
    return (lax.dot_general(hi, bd, _NN, preferred_element_type=F32)
            + lax.dot_general(lo, bd, _NN, preferred_element_type=F32))


def _layer_norm_rows(y, w, b):
    mu = jnp.mean(y, axis=-1, keepdims=True)
    yc = y - mu
    var = jnp.mean(yc * yc, axis=-1, keepdims=True)
    return yc * lax.rsqrt(var + LN_EPS) * w + b


def _full_spec(a):
    return pl.BlockSpec(a.shape, lambda i: (0,) * a.ndim)


def _odd_pre_kernel(x_ref, xx_ref, mu_ref, wrkv_ref, w1_ref, w2_ref, a1_ref, a2_ref, g1_ref, g2_ref,
                    vec_ref, r_ref, v_ref, kk_ref, g_ref, kd_ref, bb_ref, lw_ref):
    D = D_MODEL
    x = x_ref[...]
    xx = xx_ref[...]
    mix = lambda p: (x + xx * mu_ref[p:p + 1, :]).astype(BF16)
    r = jnp.dot(mix(0), wrkv_ref[0], preferred_element_type=F32)
    k = jnp.dot(mix(1), wrkv_ref[1], preferred_element_type=F32)
    v = jnp.dot(mix(2), wrkv_ref[2], preferred_element_type=F32)
    hw = jnp.tanh(jnp.dot(mix(3), w1_ref[...], preferred_element_type=F32))
    lora_w = _bdot(hw, w2_ref[...])
    lora_a = _bdot(jnp.dot(mix(4), a1_ref[...], preferred_element_type=F32), a2_ref[...])
    hg = jnp.dot(mix(5), g1_ref[...], preferred_element_type=F32)
    g = _bdot(1.0 / (1.0 + jnp.exp(-hg)), g2_ref[...])
    k_k, k_a = vec_ref[4:5, :], vec_ref[5:6, :]
    bd = _head_ones()
    kq = k * k_k
    kks = []
    for s in range(D // LANES):
        ks = kq[:, s * LANES:(s + 1) * LANES]
        nrm = jnp.sqrt(_head_sum(ks * ks, bd))
        kks.append(ks / jnp.maximum(nrm, 1e-12))
    kk = jnp.concatenate(kks, axis=1)
    r_ref[...] = r.astype(r_ref.dtype)
    v_ref[...] = v.astype(v_ref.dtype)
    kk_ref[...] = kk.astype(kk_ref.dtype)
    g_ref[...] = g.astype(g_ref.dtype)
    for n in range(2):
        u = vec_ref[n:n + 1, :] + lora_w[:, n * D:(n + 1) * D]
        sp = jnp.maximum(-u, 0.0) + jnp.log1p(jnp.exp(-jnp.abs(u)))
        lw_ref[n] = -jnp.exp(-sp - 0.5)
        a = 1.0 / (1.0 + jnp.exp(-(vec_ref[2 + n:3 + n, :] + lora_a[:, n * D:(n + 1) * D])))
        kd_ref[n] = (k * (1.0 + (a - 1.0) * k_a)).astype(kd_ref.dtype)
        bb_ref[n] = (-(kk * a)).astype(bb_ref.dtype)


def odd_pre(x, xx, mu, wrkv, w1c, w2bd, a1c, a2bd, g1, g2, vec, tm=256):
    n, D = x.shape
    row = pl.BlockSpec((tm, D), lambda i: (i, 0))
    row2 = pl.BlockSpec((2, tm, D), lambda i: (0, i, 0))
    o1 = jax.ShapeDtypeStruct((n, D), BF16)
    o2 = jax.ShapeDtypeStruct((2, n, D), BF16)
    return pl.pallas_call(
        _odd_pre_kernel, grid=(n // tm,),
        in_specs=[row, row] + [_full_spec(a) for a in (mu, wrkv, w1c, w2bd, a1c, a2bd, g1, g2, vec)],
        out_specs=[row, row, row, row, row2, row2, row2],
        out_shape=[o1, o1, o1, o1, o2, o2, jax.ShapeDtypeStruct((2, n, D), F32)],
        compiler_params=_cparams("parallel"), name="rwkv7_pre",
    )(x, xx, mu, wrkv, w1c, w2bd, a1c, a2bd, g1, g2, vec)


def _odd_post_kernel(of_ref, ob_ref, r_ref, v_ref, g_ref, kd_ref, x_ref, vec_ref, wout_ref, o_ref):
    D = D_MODEL
    bd = _head_ones()
    lnx_w, lnx_b, r_k = vec_ref[0:1, :], vec_ref[1:2, :], vec_ref[2:3, :]
    o = of_ref[...] + ob_ref[...]
    rk = r_ref[...].astype(F32) * r_k * (kd_ref[0].astype(F32) + kd_ref[1].astype(F32))
    v = v_ref[...].astype(F32)
    ys = []
    for s in range(D // LANES):
        sl = slice(s * LANES, (s + 1) * LANES)
        os_ = o[:, sl]
        oc = os_ - _head_sum(os_, bd) * (1.0 / HEAD_C)
        var = _head_sum(oc * oc, bd) * (1.0 / HEAD_C)
        on = oc * lax.rsqrt(var + LNX_EPS) * lnx_w[:, sl] + lnx_b[:, sl]
        ys.append(on + _head_sum(rk[:, sl], bd) * v[:, sl])
    y = jnp.concatenate(ys, axis=1) * g_ref[...].astype(F32)
    h = jnp.dot(y.astype(BF16), wout_ref[...], preferred_element_type=F32)
    o_ref[...] = _layer_norm_rows(DN_ALPHA * x_ref[...] + h, vec_ref[3:4, :], vec_ref[4:5, :])


def odd_post(o_f, o_b, r, v, g, kd, x, vec, w_out, tm=256):
    n, D = x.shape
    row = pl.BlockSpec((tm, D), lambda i: (i, 0))
    row2 = pl.BlockSpec((2, tm, D), lambda i: (0, i, 0))
    return pl.pallas_call(
        _odd_post_kernel, grid=(n // tm,),
        in_specs=[row, row, row, row, row, row2, row, _full_spec(vec), _full_spec(w_out)],
        out_specs=row,
        out_shape=jax.ShapeDtypeStruct((n, D), F32),
        compiler_params=_cparams("parallel"), name="rwkv7_post",
    )(o_f, o_b, r, v, g, kd, x, vec, w_out)


def _proj_ln_kernel(a_ref, w_ref, x_ref, ln_ref, o_ref):
    h = jnp.dot(a_ref[...].astype(BF16), w_ref[...], preferred_element_type=F32)
    o_ref[...] = _layer_norm_rows(DN_ALPHA * x_ref[...] + h, ln_ref[0:1, :], ln_ref[1:2, :])


def proj_ln(a, w, x, ln, tm=512):
    n, K = a.shape
    D = w.shape[1]
    return pl.pallas_call(
        _proj_ln_kernel, grid=(n // tm,),
        in_specs=[pl.BlockSpec((tm, K), lambda i: (i, 0)), pl.BlockSpec((K, D), lambda i: (0, 0)),
                  pl.BlockSpec((tm, D), lambda i: (i, 0)), pl.BlockSpec((2, D), lambda i: (0, 0))],
        out_specs=pl.BlockSpec((tm, D), lambda i: (i, 0)),
        out_shape=jax.ShapeDtypeStruct((n, D), F32),
        compiler_params=_cparams("parallel"), name="proj_res_ln",
    )(a, w, x, ln)


def _add_ln_kernel(h_ref, x_ref, ln_ref, o_ref):
    o_ref[...] = _layer_norm_rows(DN_ALPHA * x_ref[...] + h_ref[...], ln_ref[0:1, :], ln_ref[1:2, :])


def add_ln(h, x, ln, tm=512):
    n, D = x.shape
    row = pl.BlockSpec((tm, D), lambda i: (i, 0))
    return pl.pallas_call(
        _add_ln_kernel, grid=(n // tm,),
        in_specs=[row, row, pl.BlockSpec((2, D), lambda i: (0, 0))],
        out_specs=row, out_shape=jax.ShapeDtypeStruct((n, D), F32),
        compiler_params=_cparams("parallel"), name="res_ln",
    )(h, x, ln)


def _head_norm(o, gain, bias, eps, center):
    of = o
    if center:
        of = of - jnp.mean(of, -1, keepdims=True)
    of = of * lax.rsqrt(jnp.mean(jnp.square(of), -1, keepdims=True) + eps)
    of = of.reshape(of.shape[:-2] + (-1,)) * gain
    if bias is not None:
        of = of + bias
    return of


def _rotary(t, pos):
    d = t.shape[-1]
    theta = 1.0 / jnp.power(ROPE_BASE, jnp.linspace(0.0, 1.0, d // 2, dtype=F32))
    ang = pos[:, None] * theta[None, :]
    cos = jnp.cos(ang)[None, :, None, :]
    sin = jnp.sin(ang)[None, :, None, :]
    t1, t2 = t[..., 0::2], t[..., 1::2]
    return jnp.stack([t1 * cos - t2 * sin, t2 * cos + t1 * sin], -1).reshape(t.shape)


def _even_layer(xf, Bn, T, w_in, lb, norm_a, norm_b, w_out, ln):
    proj = pmm(xf, w_in).reshape(Bn, T, -1)
    a_g = proj[..., 4 * D_A:5 * D_A]
    b_q = proj[..., 5 * D_A:5 * D_A + D_B]
    b_k = proj[..., 5 * D_A + D_B:5 * D_A + 2 * D_B]
    b_g = proj[..., 5 * D_A + 3 * D_B:]

    lbp = jnp.stack([jnp.log(lb), jnp.log1p(-lb), 1.0 - lb], axis=1)
    o_a = hgrn2_scan(proj, lbp, False) + hgrn2_scan(proj, lbp, True)
    o_a = _head_norm(o_a.reshape(Bn, T, H_A, HEAD_A), norm_a, None, 1e-6, False) * jax.nn.silu(a_g)

    heads_b = lambda t: t.reshape(t.shape[:-1] + (H_B, HEAD_B))
    pos = jnp.arange(T, dtype=F32)
    qb = _rotary(heads_b(b_q), pos).reshape(Bn, T, D_B)
    kb = (_rotary(heads_b(b_k), pos) * (HEAD_B ** -0.5)).reshape(Bn, T, D_B)
    o_b = retention(qb, kb, proj)
    o_b = _head_norm(heads_b(o_b), norm_b, None, 1e-6, True) * jax.nn.silu(b_g)

    return proj_ln(jnp.concatenate([o_a, o_b], -1).reshape(Bn * T, -1), w_out, xf, ln)


def _odd_weights(p, j):
    w1, w2, a1, a2 = p['od_w1'][j], p['od_w2'][j], p['od_a1'][j], p['od_a2'][j]
    z = jnp.zeros_like(w2[0])
    w2bd = jnp.concatenate([jnp.concatenate([w2[0], z], 1), jnp.concatenate([z, w2[1]], 1)], 0)
    a2bd = jnp.concatenate([jnp.concatenate([a2[0], z], 1), jnp.concatenate([z, a2[1]], 1)], 0)
    w0, a0 = p['od_w0'][j], p['od_a0'][j]
    zero = jnp.zeros_like(w0[0])
    vec_pre = jnp.stack([w0[0], w0[1], a0[0], a0[1], p['od_k_k'][j], p['od_k_a'][j], zero, zero])
    vec_post = jnp.stack([p['od_lnx_w'][j], p['od_lnx_b'][j], p['od_r_k'][j]])
    pre = (p['od_mu'][j], p['od_w_rkv'][j], jnp.concatenate([w1[0], w1[1]], 1), w2bd,
           jnp.concatenate([a1[0], a1[1]], 1), a2bd, p['od_g1'][j], p['od_g2'][j], vec_pre)
    return pre, vec_post


def _odd_layer(xf, Bn, T, pre, vec_post, w_out, ln):
    D = xf.shape[-1]
    x = xf.reshape(Bn, T, D)
    zero = jnp.zeros_like(x[:, :1])
    xx = 0.5 * (jnp.concatenate([zero, x[:, :-1]], 1) + jnp.concatenate([x[:, 1:], zero], 1)) - x
    r, v, kk, g, kd, bb, lw = odd_pre(xf, xx.reshape(-1, D), *pre)
    sh = lambda t: t.reshape(Bn, T, D)
    o_f = rwkv7_scan(sh(r), sh(v), sh(kk), sh(lw[0]), sh(kd[0]), sh(bb[0]), False)
    o_b = rwkv7_scan(sh(r), sh(v), sh(kk), sh(lw[1]), sh(kd[1]), sh(bb[1]), True)
    vec = jnp.concatenate([vec_post, ln, jnp.zeros((3, D), F32)], 0)
    return odd_post(o_f.reshape(-1, D), o_b.reshape(-1, D), r, v, g, kd, xf, vec, w_out)


def _cross_attn_layer(xf, Bn, T, mem, w_q, w_kv, w_out, ln):
    D = xf.shape[-1]
    q = pmm(xf, w_q).reshape(Bn, T, D)
    kv = _mm(mem, w_kv)
    return proj_ln(cross_attn_core(q, kv).reshape(-1, D), w_out, xf, ln)


def _moe_layer(xt, w_router, w_in, w_out, ln):
    n, D = xt.shape
    cap = (CAP_FACTOR * n) // N_EXPERTS
    aff = jax.nn.softmax(pmm(xt, w_router), axis=-1)
    gate, idx = lax.top_k(aff.T, cap)
    xe = xt[idx]
    ye = moe_ffn(xe, w_in, w_out, gate[..., None])
    y = jnp.zeros_like(xt).at[idx.reshape(-1)].add(ye.reshape(-1, D))
    return add_ln(y, xt, ln)


def _trunk(x, mem, p):
    Bn, T, D = x.shape
    xf = x.reshape(Bn * T, D)
    p_lb = jax.nn.softmax(p['ev_lb_logits'], axis=1)
    lb_all = jnp.clip(jnp.cumsum(p_lb, axis=1) - p_lb[:, :1], 0.0, 1.0)
    for layer in range(DEPTH):
        j = layer // 2
        ln = lambda s: jnp.stack([p['ln_w'][layer, s], p['ln_b'][layer, s]])
        if layer % 2 == 0:
            xf = _even_layer(xf, Bn, T, p['ev_w_in'][j], lb_all[:, j], p['ev_norm_a'][j], p['ev_norm_b'][j],
                             p['ev_w_out'][j], ln(0))
        else:
            pre, vec_post = _odd_weights(p, j)
            xf = _odd_layer(xf, Bn, T, pre, vec_post, p['od_w_out'][j], ln(0))
        xf = _cross_attn_layer(xf, Bn, T, mem, p['ca_w_q'][layer], p['ca_w_kv'][layer], p['ca_w_out'][layer], ln(1))
        xf = _moe_layer(xf, p['moe_router'][layer], p['moe_w_in'][layer], p['moe_w_out'][layer], ln(2))
    return xf.reshape(Bn, T, D)


_MATMUL_WEIGHTS = ('ev_w_in', 'ev_w_out', 'od_w_rkv', 'od_w1', 'od_w2', 'od_a1', 'od_a2', 'od_g1', 'od_g2',
                   'od_w_out', 'ca_w_q', 'ca_w_kv', 'ca_w_out', 'moe_router', 'moe_w_in', 'moe_w_out')


def kernel(x_prompt, x_sample, mem_prompt, mem_sample, ev_w_in, ev_lb_logits, ev_norm_a, ev_norm_b, ev_w_out, od_mu, od_w_rkv, od_w0, od_w1, od_w2, od_a0, od_a1, od_a2, od_g1, od_g2, od_k_k, od_k_a, od_r_k, od_lnx_w, od_lnx_b, od_w_out, ca_w_q, ca_w_kv, ca_w_out, moe_router, moe_w_in, moe_w_out, ln_w, ln_b):
    params = dict(ev_w_in=ev_w_in, ev_lb_logits=ev_lb_logits, ev_norm_a=ev_norm_a, ev_norm_b=ev_norm_b,
                  ev_w_out=ev_w_out, od_mu=od_mu, od_w_rkv=od_w_rkv, od_w0=od_w0, od_w1=od_w1, od_w2=od_w2,
                  od_a0=od_a0, od_a1=od_a1, od_a2=od_a2, od_g1=od_g1, od_g2=od_g2, od_k_k=od_k_k,
                  od_k_a=od_k_a, od_r_k=od_r_k, od_lnx_w=od_lnx_w, od_lnx_b=od_lnx_b, od_w_out=od_w_out,
                  ca_w_q=ca_w_q, ca_w_kv=ca_w_kv, ca_w_out=ca_w_out, moe_router=moe_router,
                  moe_w_in=moe_w_in, moe_w_out=moe_w_out, ln_w=ln_w, ln_b=ln_b)
    for name in _MATMUL_WEIGHTS:
        params[name] = params[name].astype(BF16)
    y_prompt = _trunk(x_prompt, mem_prompt, params)
    y_sample = _trunk(x_sample, mem_sample, params)
    return (y_prompt, y_sample)
```

```python
import functools
import math

import jax
import jax.numpy as jnp
from jax import lax
from jax.experimental import pallas as pl
from jax.experimental.pallas import tpu as pltpu

F32 = jnp.float32
BF16 = jnp.bfloat16

D_MODEL = 1024
DEPTH = 4
D_A = D_MODEL // 2
HEAD_A = 128
H_A = D_A // HEAD_A
CHUNK_A = 32
D_B = D_MODEL // 2
HEAD_B = 128
H_B = D_B // HEAD_B
CHUNK_B = 128
ROPE_BASE = 10000.0
HEAD_C = 64
H_C = D_MODEL // HEAD_C
LNX_EPS = 64e-5
H_CA = 4
HEAD_CA = D_MODEL // H_CA
N_EXPERTS = 16
D_EXPERT = 2048
CAP_FACTOR = 2
DN_ALPHA = (2.0 * DEPTH) ** 0.25
LN_EPS = 1e-5

VMEM_LIMIT = 48 * 1024 * 1024


def _cparams(*sem):
    return pltpu.CompilerParams(dimension_semantics=sem, vmem_limit_bytes=VMEM_LIMIT)


def _mm_kernel(x_ref, w_ref, o_ref):
    o_ref[...] = jnp.dot(x_ref[...].astype(BF16), w_ref[...],
                         preferred_element_type=F32).astype(o_ref.dtype)


def _pick(n, pref):
    t = min(n, pref)
    while n % t:
        t //= 2
    return t


def pmm(x, w, tm=1024, tn=512):
    M, K = x.shape
    N = w.shape[1]
    tm = _pick(M, tm)
    tn = _pick(N, tn) if N % 128 == 0 else N
    return pl.pallas_call(
        _mm_kernel,
        grid=(M // tm, N // tn),
        in_specs=[pl.BlockSpec((tm, K), lambda i, j: (i, 0)),
                  pl.BlockSpec((K, tn), lambda i, j: (0, j))],
        out_specs=pl.BlockSpec((tm, tn), lambda i, j: (i, j)),
        out_shape=jax.ShapeDtypeStruct((M, N), F32),
        compiler_params=_cparams("parallel", "parallel"),
        name="dense_proj",
    )(x, w)


def _mm(x, w):
    lead = x.shape[:-1]
    return pmm(x.reshape(-1, x.shape[-1]), w).reshape(lead + (w.shape[1],))


def _ca_kernel(x_ref, wq_ref, k_ref, v_ref, wo_ref, ln_ref, o_ref):
    scale = HEAD_CA ** -0.5
    x = x_ref[...]
    q_all = jnp.dot(x.astype(BF16), wq_ref[...], preferred_element_type=F32).astype(BF16)
    outs = []
    for h in range(H_CA):
        sl = slice(h * HEAD_CA, (h + 1) * HEAD_CA)
        k = k_ref[0, :, sl].astype(BF16)
        v = v_ref[0, :, sl].astype(BF16)
        s = lax.dot_general(q_all[:, sl], k, (((1,), (1,)), ((), ())), preferred_element_type=F32) * scale
        m = jnp.max(s, axis=-1, keepdims=True)
        e = jnp.exp(s - m)
        p = e / jnp.sum(e, axis=-1, keepdims=True)
        outs.append(jnp.dot(p.astype(BF16), v, preferred_element_type=F32).astype(BF16))
    h_ = jnp.dot(jnp.concatenate(outs, axis=1), wo_ref[...], preferred_element_type=F32)
    y = DN_ALPHA * x + h_
    mu = jnp.mean(y, axis=-1, keepdims=True)
    yc = y - mu
    var = jnp.mean(yc * yc, axis=-1, keepdims=True)
    o_ref[...] = yc * lax.rsqrt(var + LN_EPS) * ln_ref[0:1, :] + ln_ref[1:2, :]


def cross_attn_layer(xf, T, kv, w_q, w_out, ln, tq=512):
    n, D = xf.shape
    M = kv.shape[1]
    tq = _pick(T, tq)
    per = T // tq
    return pl.pallas_call(
        _ca_kernel,
        grid=(n // tq,),
        in_specs=[pl.BlockSpec((tq, D), lambda i: (i, 0)),
                  pl.BlockSpec((D, D), lambda i: (0, 0)),
                  pl.BlockSpec((1, M, D), lambda i: (i // per, 0, 0)),
                  pl.BlockSpec((1, M, D), lambda i: (i // per, 0, 1)),
                  pl.BlockSpec((D, D), lambda i: (0, 0)),
                  pl.BlockSpec((2, D), lambda i: (0, 0))],
        out_specs=pl.BlockSpec((tq, D), lambda i: (i, 0)),
        out_shape=jax.ShapeDtypeStruct((n, D), F32),
        compiler_params=_cparams("parallel"),
        name="cross_attn",
    )(xf, w_q, kv, kv, w_out, ln)


def _moe_kernel(x_ref, wg_ref, wu_ref, wo_ref, gate_ref, o_ref, acc_ref, *, nf):
    f = pl.program_id(2)

    @pl.when(f == 0)
    def _():
        acc_ref[...] = jnp.zeros_like(acc_ref)

    x = x_ref[0].astype(BF16)
    hg = jnp.dot(x, wg_ref[0], preferred_element_type=F32)
    hu = jnp.dot(x, wu_ref[0], preferred_element_type=F32)
    a = (hg * jax.nn.sigmoid(hg) * hu).astype(BF16)
    acc_ref[...] += jnp.dot(a, wo_ref[0], preferred_element_type=F32)

    @pl.when(f == nf - 1)
    def _():
        o_ref[0] = acc_ref[...] * gate_ref[0]


def moe_ffn(xe, w_in, w_out, gate, tm=1024, tf=512):
    E, cap, D = xe.shape
    Fh = w_out.shape[1]
    tm = _pick(cap, tm)
    nf = Fh // tf
    return pl.pallas_call(
        functools.partial(_moe_kernel, nf=nf),
        grid=(E, cap // tm, nf),
        in_specs=[pl.BlockSpec((1, tm, D), lambda e, i, f: (e, i, 0)),
                  pl.BlockSpec((1, D, tf), lambda e, i, f: (e, 0, f)),
                  pl.BlockSpec((1, D, tf), lambda e, i, f: (e, 0, nf + f)),
                  pl.BlockSpec((1, tf, D), lambda e, i, f: (e, f, 0)),
                  pl.BlockSpec((1, tm, 1), lambda e, i, f: (e, i, 0))],
        out_specs=pl.BlockSpec((1, tm, D), lambda e, i, f: (e, i, 0)),
        out_shape=jax.ShapeDtypeStruct((E, cap, D), F32),
        scratch_shapes=[pltpu.VMEM((tm, D), F32)],
        compiler_params=_cparams("parallel", "parallel", "arbitrary"),
        name="moe_ffn",
    )(xe, w_in, w_in, w_out, gate)


_NN = (((1,), (0,)), ((), ()))
_NT = (((1,), (1,)), ((), ()))
_TN = (((0,), (0,)), ((), ()))


def _bdot(a, b, dims=_NN):
    return lax.dot_general(a.astype(BF16), b.astype(BF16), dims, preferred_element_type=F32)


def _split3(x):
    hi = x.astype(BF16)
    r1 = x - hi.astype(F32)
    mid = r1.astype(BF16)
    lo = (r1 - mid.astype(F32)).astype(BF16)
    return hi, mid, lo


def _cumdot(cum, x):
    hi, mid, lo = _split3(x)
    return (lax.dot_general(cum, hi, _NN, preferred_element_type=F32)
            + lax.dot_general(cum, mid, _NN, preferred_element_type=F32)
            + lax.dot_general(cum, lo, _NN, preferred_element_type=F32))


RW_L = 64
RW_TT = 512
RW_W = 128


def _rwkv_kernel(r_ref, v_ref, kk_ref, lw_ref, kd_ref, bb_ref, o_ref, m_ref, om_ref, th_ref, *, reverse):
    L = RW_L
    nc = RW_TT // L
    S = 2 * L

    @pl.when(pl.program_id(2) == 0)
    def _():
        m_ref[...] = jnp.zeros_like(m_ref)

    row = lax.broadcasted_iota(jnp.int32, (S, S), 0)
    col = lax.broadcasted_iota(jnp.int32, (S, S), 1)
    if reverse:
        strict, incl = col > row, col >= row
    else:
        strict, incl = col < row, col <= row
    incl2 = jnp.concatenate([incl, incl], axis=1)
    eye = (row == col).astype(F32)
    cr = lax.broadcasted_iota(jnp.int32, (L, L), 0)
    cc = lax.broadcasted_iota(jnp.int32, (L, L), 1)
    cum = ((cc >= cr) if reverse else (cc <= cr)).astype(BF16)
    lane = lax.broadcasted_iota(jnp.int32, (L, RW_W), 1)
    m0 = lane < HEAD_C

    def stack(x):
        return jnp.concatenate([jnp.where(m0, x, 0.0), jnp.where(m0, 0.0, x)], axis=0)

    cs = range(nc)
    ld = lambda ref, c: ref[0, pl.ds(c * L, L), :]
    lws = [ld(lw_ref, c) for c in cs]
    gs = [_cumdot(cum, lws[c]) for c in cs]
    gls = [(g[0:1, :] if reverse else g[L - 1:L, :]) for g in gs]
    lhs, rhs_a, kbl, vss, rgs, ags = [], [], [], [], [], []
    for c in cs:
        g, gl = gs[c], gls[c]
        eg, eng, egl = jnp.exp(g), jnp.exp(-g), jnp.exp(gl - g)
        kd, bb = ld(kd_ref, c), ld(bb_ref, c)
        rg_s, ag_s = stack(ld(r_ref, c) * eg), stack(ld(kk_ref, c) * jnp.exp(g - lws[c]))
        rgs.append(rg_s)
        ags.append(ag_s)
        lhs.append(jnp.concatenate([ag_s, rg_s], axis=0))
        rhs_a.append(jnp.concatenate([stack(kd * eng), stack(bb * eng)], axis=0))
        kbl.append(jnp.concatenate([stack(kd * egl), stack(bb * egl)], axis=0))
        vss.append(stack(ld(v_ref, c)))
    amats = [_bdot(lhs[c], rhs_a[c], _NT) for c in cs]
    aaks = [jnp.where(strict, a[0:S, 0:S], 0.0) for a in amats]
    ns = [jnp.where(strict, a[0:S, S:2 * S], 0.0) for a in amats]
    arkb = [jnp.where(incl2, a[S:2 * S, :], 0.0) for a in amats]
    ts = [eye + n for n in ns]
    ps = [_bdot(n, n) for n in ns]
    avs = [_bdot(aaks[c], vss[c]) for c in cs]
    for _ in range(4):
        pps = [_bdot(ps[c], jnp.concatenate([ps[c], ts[c]], axis=1)) for c in cs]
        ts = [ts[c] + pps[c][:, S:2 * S] for c in cs]
        ps = [pps[c][:, 0:S] for c in cs]
    ts = [ts[c] + _bdot(ps[c], ts[c]) for c in cs]
    w12s = [_bdot(ts[c], jnp.concatenate([ags[c], avs[c]], axis=1)) for c in cs]
    rhss = [jnp.concatenate([jnp.concatenate([jnp.zeros_like(vss[c]), vss[c]], axis=1), w12s[c]], axis=0) for c in cs]
    sts = [_bdot(kbl[c], rhss[c], _TN) for c in cs]
    ots = [_bdot(arkb[c], rhss[c]) for c in cs]
    for c in cs:
        phi = sts[c][:, 0:RW_W] + eye * jnp.exp(gls[c])
        om_ref[c] = jnp.concatenate([ots[c][:, 0:RW_W] + rgs[c], phi], axis=0)
        th_ref[c] = jnp.concatenate([ots[c][:, RW_W:2 * RW_W], sts[c][:, RW_W:2 * RW_W]], axis=0)

    m = m_ref[...]
    for j in range(nc):
        c = nc - 1 - j if reverse else j
        res = _bdot(om_ref[c], m) + th_ref[c]
        o_ref[0, pl.ds(c * L, L), :] = res[0:L] + res[L:S]
        m = res[S:2 * S]
    m_ref[...] = m


def rwkv7_scan(r, v, kk, lw, kd, bb, reverse):
    B, T, D = r.shape
    nt = T // RW_TT
    tmap = (lambda b, h, i: (b, nt - 1 - i, h)) if reverse else (lambda b, h, i: (b, i, h))
    spec = pl.BlockSpec((1, RW_TT, RW_W), tmap)
    nc = RW_TT // RW_L
    return pl.pallas_call(
        functools.partial(_rwkv_kernel, reverse=reverse),
        grid=(B, D // RW_W, nt),
        in_specs=[spec] * 6,
        out_specs=spec,
        out_shape=jax.ShapeDtypeStruct((B, T, D), F32),
        scratch_shapes=[pltpu.VMEM((RW_W, RW_W), F32),
                        pltpu.VMEM((nc, 4 * RW_L, RW_W), F32),
                        pltpu.VMEM((nc, 4 * RW_L, RW_W), F32)],
        compiler_params=_cparams("parallel", "parallel", "arbitrary"),
        name="rwkv7_bwd" if reverse else "rwkv7_fwd",
    )(r, v, kk, lw, kd, bb)


HG_SUB = 16
HG_GRP = 128
HG_TT = 512


def _hgrn2_kernel(q_ref, v_ref, z_ref, lb_ref, o_ref, st_ref, *, reverse):
    C, G = HG_SUB, HG_GRP

    @pl.when(pl.program_id(2) == 0)
    def _():
        st_ref[...] = jnp.zeros_like(st_ref)

    log_lb, log_1mlb, one_mlb = lb_ref[0, 0:1, :], lb_ref[0, 1:2, :], lb_ref[0, 2:3, :]
    row = lax.broadcasted_iota(jnp.int32, (G, G), 0)
    col = lax.broadcasted_iota(jnp.int32, (G, G), 1)
    same = (row // C) == (col // C)
    cum = (same & ((col >= row) if reverse else (col <= row))).astype(BF16)
    sub_t = lax.broadcasted_iota(jnp.int32, (C, HEAD_A), 0)

    def group(gi, carry):
        g0 = (HG_TT // G - 1 - gi) if reverse else gi
        rows = pl.ds(pl.multiple_of(g0 * G, G), G)
        z = z_ref[0, rows, :]
        a_q = q_ref[0, rows, :]
        vv = v_ref[0, rows, :]
        ls = jnp.minimum(z, 0.0) - jnp.log1p(jnp.exp(-jnp.abs(z)))
        t2 = log_1mlb + ls
        mx = jnp.maximum(log_lb, t2)
        lf = mx + jnp.log1p(jnp.exp(-jnp.abs(log_lb - t2)))
        kk = one_mlb * (1.0 / (1.0 + jnp.exp(z)))
        qq = a_q * (1.0 / (1.0 + jnp.exp(-a_q)))
        b = _cumdot(cum, lf)
        st = st_ref[...]
        order = range(G // C - 1, -1, -1) if reverse else range(G // C)
        outs = [None] * (G // C)
        for j in order:
            sl = slice(j * C, (j + 1) * C)
            bj, qj, kj, vj = b[sl], qq[sl], kk[sl], vv[sl]
            e_in = 0 if reverse else C - 1
            b_end = bj[e_in:e_in + 1, :]
            o = _bdot(qj * jnp.exp(bj), st, _NT)
            for s in range(C):
                keep = (sub_t <= s) if reverse else (sub_t >= s)
                dec = jnp.exp(jnp.where(keep, bj - bj[s:s + 1, :], -1e30))
                sc = jnp.sum(dec * (qj * kj[s:s + 1, :]), axis=-1, keepdims=True)
                o = o + sc * vj[s:s + 1, :]
            outs[j] = o
            st = st * jnp.exp(b_end) + _bdot(vj, kj * jnp.exp(b_end - bj), _TN)
        st_ref[...] = st
        o_ref[0, rows, :] = jnp.concatenate(outs, axis=0)
        return carry

    lax.fori_loop(0, HG_TT // G, group, 0)


def hgrn2_scan(proj, lbp, reverse):
    B, T, _ = proj.shape
    nt = T // HG_TT
    tix = (lambda i: nt - 1 - i) if reverse else (lambda i: i)
    zoff = 3 * H_A if reverse else 2 * H_A
    d = 1 if reverse else 0
    blk = (1, HG_TT, HEAD_A)
    return pl.pallas_call(
        functools.partial(_hgrn2_kernel, reverse=reverse),
        grid=(B, H_A, nt),
        in_specs=[pl.BlockSpec(blk, lambda b, h, i: (b, tix(i), h)),
                  pl.BlockSpec(blk, lambda b, h, i: (b, tix(i), H_A + h)),
                  pl.BlockSpec(blk, lambda b, h, i: (b, tix(i), zoff + h)),
                  pl.BlockSpec((1, 3, HEAD_A), lambda b, h, i: (d, 0, h))],
        out_specs=pl.BlockSpec(blk, lambda b, h, i: (b, tix(i), h)),
        out_shape=jax.ShapeDtypeStruct((B, T, D_A), F32),
        scratch_shapes=[pltpu.VMEM((HEAD_A, HEAD_A), F32)],
        compiler_params=_cparams("parallel", "parallel", "arbitrary"),
        name="hgrn2_bwd" if reverse else "hgrn2_fwd",
    )(proj, proj, proj, lbp)


RT_TT = 256


def rope_tables(T):
    theta = 1.0 / jnp.power(ROPE_BASE, jnp.linspace(0.0, 1.0, HEAD_B // 2, dtype=F32))
    ang = jnp.arange(T, dtype=F32)[:, None] * theta[None, :]
    cos = jnp.repeat(jnp.cos(ang), 2, axis=1)
    sin = jnp.repeat(jnp.sin(ang), 2, axis=1)
    sign = jnp.where(jnp.arange(HEAD_B) % 2 == 0, -1.0, 1.0).astype(F32)
    return cos, sin * sign[None, :]


def _rope(t, cos, sin):
    lane = lax.broadcasted_iota(jnp.int32, t.shape, 1)
    nxt = pltpu.roll(t, t.shape[1] - 1, axis=1)
    prv = pltpu.roll(t, 1, axis=1)
    return t * cos + jnp.where(lane % 2 == 0, nxt, prv) * sin


def _ret_fwd_kernel(lg_ref, cos_ref, sin_ref, q_ref, k_ref, v_ref, o_ref, st_ref):
    Tt = RT_TT

    @pl.when(pl.program_id(2) == 0)
    def _():
        st_ref[...] = jnp.zeros_like(st_ref)

    lgw = lg_ref[0]
    lg = lgw[:, 0:HEAD_B]
    cos, sin = cos_ref[...], sin_ref[...]
    q = _rope(q_ref[0], cos, sin)
    k = _rope(k_ref[0], cos, sin) * (HEAD_B ** -0.5)
    v = v_ref[0]
    row = lax.broadcasted_iota(jnp.int32, (Tt, Tt), 0)
    col = lax.broadcasted_iota(jnp.int32, (Tt, Tt), 1)
    dmat = jnp.exp(lgw * jnp.abs(row - col).astype(F32))
    pos = lax.broadcasted_iota(jnp.int32, (Tt, 1), 0).astype(F32)
    st = st_ref[...]
    o = _bdot(_bdot(q, k, _NT) * dmat, v)
    o_ref[0] = o + _bdot(q * jnp.exp(lg * (pos + 1.0)), st)
    st_ref[...] = st * jnp.exp(lg * Tt) + _bdot(k * jnp.exp(lg * (Tt - 1.0 - pos)), v, _TN)


def _ret_bwd_kernel(lg_ref, cos_ref, sin_ref, q_ref, k_ref, v_ref, oin_ref, o_ref, st_ref):
    Tt = RT_TT

    @pl.when(pl.program_id(2) == 0)
    def _():
        st_ref[...] = jnp.zeros_like(st_ref)

    lg = lg_ref[0][:, 0:HEAD_B]
    cos, sin = cos_ref[...], sin_ref[...]
    q = _rope(q_ref[0], cos, sin)
    k = _rope(k_ref[0], cos, sin) * (HEAD_B ** -0.5)
    pos = lax.broadcasted_iota(jnp.int32, (Tt, 1), 0).astype(F32)
    st = st_ref[...]
    o_ref[0] = oin_ref[0] + _bdot(q * jnp.exp(lg * (Tt - pos)), st)
    st_ref[...] = st * jnp.exp(lg * Tt) + _bdot(k * jnp.exp(lg * pos), v_ref[0], _TN)


def retention(proj):
    B, T, _ = proj.shape
    nt = T // RT_TT
    blk = (1, RT_TT, HEAD_B)
    qoff = 5 * D_A // HEAD_B
    koff, voff = qoff + H_B, qoff + 2 * H_B
    cos, sin = rope_tables(T)
    lgam = jnp.log1p(-jnp.power(2.0, -5.0 - jnp.arange(H_B, dtype=F32)))
    lgam = jnp.broadcast_to(lgam[:, None, None], (H_B, 1, RT_TT))
    lspec = pl.BlockSpec((1, 1, RT_TT), lambda b, h, i: (h, 0, 0))

    def specs(tix):
        tab = pl.BlockSpec((RT_TT, HEAD_B), lambda b, h, i: (tix(i), 0))
        return [lspec, tab, tab] + [pl.BlockSpec(blk, (lambda off: lambda b, h, i: (b, tix(i), off + h))(off))
                                    for off in (qoff, koff, voff)]
    fwd = lambda i: i
    rev = lambda i: nt - 1 - i
    o1 = pl.pallas_call(
        _ret_fwd_kernel, grid=(B, H_B, nt),
        in_specs=specs(fwd),
        out_specs=pl.BlockSpec(blk, lambda b, h, i: (b, i, h)),
        out_shape=jax.ShapeDtypeStruct((B, T, D_B), F32),
        scratch_shapes=[pltpu.VMEM((HEAD_B, HEAD_B), F32)],
        compiler_params=_cparams("parallel", "parallel", "arbitrary"), name="retention_fwd",
    )(lgam, cos, sin, proj, proj, proj)
    return pl.pallas_call(
        _ret_bwd_kernel, grid=(B, H_B, nt),
        in_specs=specs(rev) + [pl.BlockSpec(blk, lambda b, h, i: (b, nt - 1 - i, h))],
        out_specs=pl.BlockSpec(blk, lambda b, h, i: (b, nt - 1 - i, h)),
        out_shape=jax.ShapeDtypeStruct((B, T, D_B), F32),
        scratch_shapes=[pltpu.VMEM((HEAD_B, HEAD_B), F32)],
        input_output_aliases={6: 0},
        compiler_params=_cparams("parallel", "parallel", "arbitrary"), name="retention_bwd",
    )(lgam, cos, sin, proj, proj, proj, o1)


LANES = 128


def _head_ones():
    r = lax.broadcasted_iota(jnp.int32, (LANES, LANES), 0) // HEAD_C
    c = lax.broadcasted_iota(jnp.int32, (LANES, LANES), 1) // HEAD_C
    return (r == c).astype(BF16)


def _head_sum(x, bd):
    hi = x.astype(BF16)
    lo = (x - hi.astype(F32)).astype(BF16)
    return (lax.dot_general(hi, bd, _NN, preferred_element_type=F32)
            + lax.dot_general(lo, bd, _NN, preferred_element_type=F32))


def _layer_norm_rows(y, w, b):
    mu = jnp.mean(y, axis=-1, keepdims=True)
    yc = y - mu
    var = jnp.mean(yc * yc, axis=-1, keepdims=True)
    return yc * lax.rsqrt(var + LN_EPS) * w + b


def _full_spec(a):
    return pl.BlockSpec(a.shape, lambda i: (0,) * a.ndim)


def _odd_pre_kernel(x_ref, xx_ref, mu_ref, wrkv_ref, w1_ref, w2_ref, a1_ref, a2_ref, g1_ref, g2_ref,
                    vec_ref, r_ref, v_ref, kk_ref, g_ref, kd_ref, bb_ref, lw_ref):
    D = D_MODEL
    x = x_ref[...]
    xx = xx_ref[...]
    mix = lambda p: (x + xx * mu_ref[p:p + 1, :]).astype(BF16)
    r = jnp.dot(mix(0), wrkv_ref[0], preferred_element_type=F32)
    k = jnp.dot(mix(1), wrkv_ref[1], preferred_element_type=F32)
    v = jnp.dot(mix(2), wrkv_ref[2], preferred_element_type=F32)
    hw = jnp.tanh(jnp.dot(mix(3), w1_ref[...], preferred_element_type=F32))
    lora_w = _bdot(hw, w2_ref[...])
    lora_a = _bdot(jnp.dot(mix(4), a1_ref[...], preferred_element_type=F32), a2_ref[...])
    hg = jnp.dot(mix(5), g1_ref[...], preferred_element_type=F32)
    g = _bdot(1.0 / (1.0 + jnp.exp(-hg)), g2_ref[...])
    k_k, k_a = vec_ref[4:5, :], vec_ref[5:6, :]
    bd = _head_ones()
    kq = k * k_k
    kks = []
    for s in range(D // LANES):
        ks = kq[:, s * LANES:(s + 1) * LANES]
        nrm = jnp.sqrt(_head_sum(ks * ks, bd))
        kks.append(ks / jnp.maximum(nrm, 1e-12))
    kk = jnp.concatenate(kks, axis=1)
    r_ref[...] = r.astype(r_ref.dtype)
    v_ref[...] = v.astype(v_ref.dtype)
    kk_ref[...] = kk.astype(kk_ref.dtype)
    g_ref[...] = g.astype(g_ref.dtype)
    for n in range(2):
        u = vec_ref[n:n + 1, :] + lora_w[:, n * D:(n + 1) * D]
        sp = jnp.maximum(-u, 0.0) + jnp.log1p(jnp.exp(-jnp.abs(u)))
        lw_ref[n] = -jnp.exp(-sp - 0.5)
        a = 1.0 / (1.0 + jnp.exp(-(vec_ref[2 + n:3 + n, :] + lora_a[:, n * D:(n + 1) * D])))
        kd_ref[n] = (k * (1.0 + (a - 1.0) * k_a)).astype(kd_ref.dtype)
        bb_ref[n] = (-(kk * a)).astype(bb_ref.dtype)


def odd_pre(x, xx, mu, wrkv, w1c, w2bd, a1c, a2bd, g1, g2, vec, tm=256):
    n, D = x.shape
    row = pl.BlockSpec((tm, D), lambda i: (i, 0))
    row2 = pl.BlockSpec((2, tm, D), lambda i: (0, i, 0))
    o1 = jax.ShapeDtypeStruct((n, D), BF16)
    o2 = jax.ShapeDtypeStruct((2, n, D), BF16)
    return pl.pallas_call(
        _odd_pre_kernel, grid=(n // tm,),
        in_specs=[row, row] + [_full_spec(a) for a in (mu, wrkv, w1c, w2bd, a1c, a2bd, g1, g2, vec)],
        out_specs=[row, row, row, row, row2, row2, row2],
        out_shape=[o1, o1, o1, o1, o2, o2, jax.ShapeDtypeStruct((2, n, D), F32)],
        compiler_params=_cparams("parallel"), name="rwkv7_pre",
    )(x, xx, mu, wrkv, w1c, w2bd, a1c, a2bd, g1, g2, vec)


def _odd_post_kernel(of_ref, ob_ref, r_ref, v_ref, g_ref, kd_ref, x_ref, vec_ref, wout_ref, o_ref):
    D = D_MODEL
    bd = _head_ones()
    lnx_w, lnx_b, r_k = vec_ref[0:1, :], vec_ref[1:2, :], vec_ref[2:3, :]
    o = of_ref[...] + ob_ref[...]
    rk = r_ref[...].astype(F32) * r_k * (kd_ref[0].astype(F32) + kd_ref[1].astype(F32))
    v = v_ref[...].astype(F32)
    ys = []
    for s in range(D // LANES):
        sl = slice(s * LANES, (s + 1) * LANES)
        os_ = o[:, sl]
        oc = os_ - _head_sum(os_, bd) * (1.0 / HEAD_C)
        var = _head_sum(oc * oc, bd) * (1.0 / HEAD_C)
        on = oc * lax.rsqrt(var + LNX_EPS) * lnx_w[:, sl] + lnx_b[:, sl]
        ys.append(on + _head_sum(rk[:, sl], bd) * v[:, sl])
    y = jnp.concatenate(ys, axis=1) * g_ref[...].astype(F32)
    h = jnp.dot(y.astype(BF16), wout_ref[...], preferred_element_type=F32)
    o_ref[...] = _layer_norm_rows(DN_ALPHA * x_ref[...] + h, vec_ref[3:4, :], vec_ref[4:5, :])


def odd_post(o_f, o_b, r, v, g, kd, x, vec, w_out, tm=256):
    n, D = x.shape
    row = pl.BlockSpec((tm, D), lambda i: (i, 0))
    row2 = pl.BlockSpec((2, tm, D), lambda i: (0, i, 0))
    return pl.pallas_call(
        _odd_post_kernel, grid=(n // tm,),
        in_specs=[row, row, row, row, row, row2, row, _full_spec(vec), _full_spec(w_out)],
        out_specs=row,
        out_shape=jax.ShapeDtypeStruct((n, D), F32),
        compiler_params=_cparams("parallel"), name="rwkv7_post",
    )(o_f, o_b, r, v, g, kd, x, vec, w_out)


def _proj_ln_kernel(a_ref, w_ref, x_ref, ln_ref, o_ref):
    h = jnp.dot(a_ref[...].astype(BF16), w_ref[...], preferred_element_type=F32)
    o_ref[...] = _layer_norm_rows(DN_ALPHA * x_ref[...] + h, ln_ref[0:1, :], ln_ref[1:2, :])


def proj_ln(a, w, x, ln, tm=512):
    n, K = a.shape
    D = w.shape[1]
    return pl.pallas_call(
        _proj_ln_kernel, grid=(n // tm,),
        in_specs=[pl.BlockSpec((tm, K), lambda i: (i, 0)), pl.BlockSpec((K, D), lambda i: (0, 0)),
                  pl.BlockSpec((tm, D), lambda i: (i, 0)), pl.BlockSpec((2, D), lambda i: (0, 0))],
        out_specs=pl.BlockSpec((tm, D), lambda i: (i, 0)),
        out_shape=jax.ShapeDtypeStruct((n, D), F32),
        compiler_params=_cparams("parallel"), name="proj_res_ln",
    )(a, w, x, ln)


def _add_ln_kernel(h_ref, x_ref, ln_ref, o_ref):
    o_ref[...] = _layer_norm_rows(DN_ALPHA * x_ref[...] + h_ref[...], ln_ref[0:1, :], ln_ref[1:2, :])


def add_ln(h, x, ln, tm=512):
    n, D = x.shape
    row = pl.BlockSpec((tm, D), lambda i: (i, 0))
    return pl.pallas_call(
        _add_ln_kernel, grid=(n // tm,),
        in_specs=[row, row, pl.BlockSpec((2, D), lambda i: (0, 0))],
        out_specs=row, out_shape=jax.ShapeDtypeStruct((n, D), F32),
        compiler_params=_cparams("parallel"), name="res_ln",
    )(h, x, ln)


def _even_post_kernel(oaf_ref, oab_ref, ob_ref, ga_ref, gb_ref, x_ref, vec_ref, wout_ref, o_ref):
    silu = lambda t: t * (1.0 / (1.0 + jnp.exp(-t)))
    oa = oaf_ref[...] + oab_ref[...]
    ob = ob_ref[...]
    ga, gb = ga_ref[...], gb_ref[...]
    na, nb = vec_ref[0:1, :], vec_ref[1:2, :]
    ys = []
    for h in range(H_A):
        sl = slice(h * HEAD_A, (h + 1) * HEAD_A)
        t = oa[:, sl]
        t = t * lax.rsqrt(jnp.mean(t * t, axis=-1, keepdims=True) + 1e-6)
        ys.append(t * na[:, sl] * silu(ga[:, sl]))
    for h in range(H_B):
        sl = slice(h * HEAD_B, (h + 1) * HEAD_B)
        t = ob[:, sl]
        t = t - jnp.mean(t, axis=-1, keepdims=True)
        t = t * lax.rsqrt(jnp.mean(t * t, axis=-1, keepdims=True) + 1e-6)
        ys.append(t * nb[:, sl] * silu(gb[:, sl]))
    y = jnp.concatenate(ys, axis=1)
    h_ = jnp.dot(y.astype(BF16), wout_ref[...], preferred_element_type=F32)
    o_ref[...] = _layer_norm_rows(DN_ALPHA * x_ref[...] + h_, vec_ref[2:3, :], vec_ref[3:4, :])


def even_post(oa_f, oa_b, o_b, proj, x, vec, w_out, tm=256):
    n, D = x.shape
    half = pl.BlockSpec((tm, D_A), lambda i: (i, 0))
    row = pl.BlockSpec((tm, D), lambda i: (i, 0))
    return pl.pallas_call(
        _even_post_kernel, grid=(n // tm,),
        in_specs=[half, half, half, pl.BlockSpec((tm, D_A), lambda i: (i, 4)),
                  pl.BlockSpec((tm, D_B), lambda i: (i, 8)), row,
                  pl.BlockSpec(vec.shape, lambda i: (0, 0)), pl.BlockSpec(w_out.shape, lambda i: (0, 0))],
        out_specs=row,
        out_shape=jax.ShapeDtypeStruct((n, D), F32),
        compiler_params=_cparams("parallel"), name="even_post",
    )(oa_f, oa_b, o_b, proj, proj, x, vec, w_out)


def _even_layer(xf, Bn, T, w_in, lb, norm_a, norm_b, w_out, ln):
    n, D = xf.shape
    proj = pmm(xf, w_in).reshape(Bn, T, -1)
    lbp = jnp.stack([jnp.log(lb), jnp.log1p(-lb), 1.0 - lb], axis=1)
    oa_f = hgrn2_scan(proj, lbp, False)
    oa_b = hgrn2_scan(proj, lbp, True)
    o_b = retention(proj)
    z = jnp.zeros((D - D_A,), F32)
    vec = jnp.concatenate([jnp.stack([jnp.concatenate([norm_a, z]), jnp.concatenate([norm_b, z])]), ln,
                           jnp.zeros((4, D), F32)], 0)
    fl = lambda t: t.reshape(n, -1)
    return even_post(fl(oa_f), fl(oa_b), fl(o_b), fl(proj), xf, vec, w_out)


def _odd_weights(p, j):
    w1, w2, a1, a2 = p['od_w1'][j], p['od_w2'][j], p['od_a1'][j], p['od_a2'][j]
    z = jnp.zeros_like(w2[0])
    w2bd = jnp.concatenate([jnp.concatenate([w2[0], z], 1), jnp.concatenate([z, w2[1]], 1)], 0)
    a2bd = jnp.concatenate([jnp.concatenate([a2[0], z], 1), jnp.concatenate([z, a2[1]], 1)], 0)
    w0, a0 = p['od_w0'][j], p['od_a0'][j]
    zero = jnp.zeros_like(w0[0])
    vec_pre = jnp.stack([w0[0], w0[1], a0[0], a0[1], p['od_k_k'][j], p['od_k_a'][j], zero, zero])
    vec_post = jnp.stack([p['od_lnx_w'][j], p['od_lnx_b'][j], p['od_r_k'][j]])
    pre = (p['od_mu'][j], p['od_w_rkv'][j], jnp.concatenate([w1[0], w1[1]], 1), w2bd,
           jnp.concatenate([a1[0], a1[1]], 1), a2bd, p['od_g1'][j], p['od_g2'][j], vec_pre)
    return pre, vec_post


def _odd_layer(xf, Bn, T, pre, vec_post, w_out, ln):
    D = xf.shape[-1]
    x = xf.reshape(Bn, T, D)
    zero = jnp.zeros_like(x[:, :1])
    xx = 0.5 * (jnp.concatenate([zero, x[:, :-1]], 1) + jnp.concatenate([x[:, 1:], zero], 1)) - x
    r, v, kk, g, kd, bb, lw = odd_pre(xf, xx.reshape(-1, D), *pre)
    sh = lambda t: t.reshape(Bn, T, D)
    o_f = rwkv7_scan(sh(r), sh(v), sh(kk), sh(lw[0]), sh(kd[0]), sh(bb[0]), False)
    o_b = rwkv7_scan(sh(r), sh(v), sh(kk), sh(lw[1]), sh(kd[1]), sh(bb[1]), True)
    vec = jnp.concatenate([vec_post, ln, jnp.zeros((3, D), F32)], 0)
    return odd_post(o_f.reshape(-1, D), o_b.reshape(-1, D), r, v, g, kd, xf, vec, w_out)


def _cross_attn_layer(xf, Bn, T, mem, w_q, w_kv, w_out, ln):
    return cross_attn_layer(xf, T, _mm(mem, w_kv), w_q, w_out, ln)


def _moe_layer(xt, w_router, w_in, w_out, ln):
    n, D = xt.shape
    cap = (CAP_FACTOR * n) // N_EXPERTS
    aff = jax.nn.softmax(pmm(xt, w_router), axis=-1)
    gate, idx = lax.top_k(aff.T, cap)
    xe = xt[idx]
    ye = moe_ffn(xe, w_in, w_out, gate[..., None])
    y = jnp.zeros_like(xt).at[idx.reshape(-1)].add(ye.reshape(-1, D))
    return add_ln(y, xt, ln)


def _trunk(x, mem, p):
    Bn, T, D = x.shape
    xf = x.reshape(Bn * T, D)
    p_lb = jax.nn.softmax(p['ev_lb_logits'], axis=1)
    lb_all = jnp.clip(jnp.cumsum(p_lb, axis=1) - p_lb[:, :1], 0.0, 1.0)
    for layer in range(DEPTH):
        j = layer // 2
        ln = lambda s: jnp.stack([p['ln_w'][layer, s], p['ln_b'][layer, s]])
        if layer % 2 == 0:
            xf = _even_layer(xf, Bn, T, p['ev_w_in'][j], lb_all[:, j], p['ev_norm_a'][j], p['ev_norm_b'][j],
                             p['ev_w_out'][j], ln(0))
        else:
            pre, vec_post = _odd_weights(p, j)
            xf = _odd_layer(xf, Bn, T, pre, vec_post, p['od_w_out'][j], ln(0))
        xf = _cross_attn_layer(xf, Bn, T, mem, p['ca_w_q'][layer], p['ca_w_kv'][layer], p['ca_w_out'][layer], ln(1))
        xf = _moe_layer(xf, p['moe_router'][layer], p['moe_w_in'][layer], p['moe_w_out'][layer], ln(2))
    return xf.reshape(Bn, T, D)


_MATMUL_WEIGHTS = ('ev_w_in', 'ev_w_out', 'od_w_rkv', 'od_w1', 'od_w2', 'od_a1', 'od_a2', 'od_g1', 'od_g2',
                   'od_w_out', 'ca_w_q', 'ca_w_kv', 'ca_w_out', 'moe_router', 'moe_w_in', 'moe_w_out')


def kernel(x_prompt, x_sample, mem_prompt, mem_sample, ev_w_in, ev_lb_logits, ev_norm_a, ev_norm_b, ev_w_out, od_mu, od_w_rkv, od_w0, od_w1, od_w2, od_a0, od_a1, od_a2, od_g1, od_g2, od_k_k, od_k_a, od_r_k, od_lnx_w, od_lnx_b, od_w_out, ca_w_q, ca_w_kv, ca_w_out, moe_router, moe_w_in, moe_w_out, ln_w, ln_b):
    params = dict(ev_w_in=ev_w_in, ev_lb_logits=ev_lb_logits, ev_norm_a=ev_norm_a, ev_norm_b=ev_norm_b,
                  ev_w_out=ev_w_out, od_mu=od_mu, od_w_rkv=od_w_rkv, od_w0=od_w0, od_w1=od_w1, od_w2=od_w2,
                  od_a0=od_a0, od_a1=od_a1, od_a2=od_a2, od_g1=od_g1, od_g2=od_g2, od_k_k=od_k_k,
                  od_k_a=od_k_a, od_r_k=od_r_k, od_lnx_w=od_lnx_w, od_lnx_b=od_lnx_b, od_w_out=od_w_out,
                  ca_w_q=ca_w_q, ca_w_kv=ca_w_kv, ca_w_out=ca_w_out, moe_router=moe_router,
                  moe_w_in=moe_w_in, moe_w_out=moe_w_out, ln_w=ln_w, ln_b=ln_b)
    for name in _MATMUL_WEIGHTS:
        params[name] = params[name].astype(BF16)
    y_prompt = _trunk(x_prompt, mem_prompt, params)
    y_sample = _trunk(x_sample, mem_sample, params)
    return (y_prompt, y_sample)
```

```python
import functools
import math

import jax
import jax.numpy as jnp
from jax import lax
from jax.experimental import pallas as pl
from jax.experimental.pallas import tpu as pltpu

F32 = jnp.float32
BF16 = jnp.bfloat16

D_MODEL = 1024
DEPTH = 4
D_A = D_MODEL // 2
HEAD_A = 128
H_A = D_A // HEAD_A
CHUNK_A = 32
D_B = D_MODEL // 2
HEAD_B = 128
H_B = D_B // HEAD_B
CHUNK_B = 128
ROPE_BASE = 10000.0
HEAD_C = 64
H_C = D_MODEL // HEAD_C
LNX_EPS = 64e-5
H_CA = 4
HEAD_CA = D_MODEL // H_CA
N_EXPERTS = 16
D_EXPERT = 2048
CAP_FACTOR = 2
DN_ALPHA = (2.0 * DEPTH) ** 0.25
LN_EPS = 1e-5

VMEM_LIMIT = 48 * 1024 * 1024


def _cparams(*sem):
    return pltpu.CompilerParams(dimension_semantics=sem, vmem_limit_bytes=VMEM_LIMIT)


def _mm_kernel(x_ref, w_ref, o_ref):
    o_ref[...] = jnp.dot(x_ref[...].astype(BF16), w_ref[...],
                         preferred_element_type=F32).astype(o_ref.dtype)


def _pick(n, pref):
    t = min(n, pref)
    while n % t:
        t //= 2
    return t


def pmm(x, w, tm=1024, tn=512):
    M, K = x.shape
    N = w.shape[1]
    tm = _pick(M, tm)
    tn = _pick(N, tn) if N % 128 == 0 else N
    return pl.pallas_call(
        _mm_kernel,
        grid=(M // tm, N // tn),
        in_specs=[pl.BlockSpec((tm, K), lambda i, j: (i, 0)),
                  pl.BlockSpec((K, tn), lambda i, j: (0, j))],
        out_specs=pl.BlockSpec((tm, tn), lambda i, j: (i, j)),
        out_shape=jax.ShapeDtypeStruct((M, N), F32),
        compiler_params=_cparams("parallel", "parallel"),
        name="dense_proj",
    )(x, w)


def _mm(x, w):
    lead = x.shape[:-1]
    return pmm(x.reshape(-1, x.shape[-1]), w).reshape(lead + (w.shape[1],))


def _ca_kernel(x_ref, wq_ref, k_ref, v_ref, wo_ref, ln_ref, o_ref):
    scale = HEAD_CA ** -0.5
    x = x_ref[...]
    q_all = jnp.dot(x.astype(BF16), wq_ref[...], preferred_element_type=F32).astype(BF16)
    outs = []
    for h in range(H_CA):
        sl = slice(h * HEAD_CA, (h + 1) * HEAD_CA)
        k = k_ref[0, :, sl].astype(BF16)
        v = v_ref[0, :, sl].astype(BF16)
        s = lax.dot_general(q_all[:, sl], k, (((1,), (1,)), ((), ())), preferred_element_type=F32) * scale
        m = jnp.max(s, axis=-1, keepdims=True)
        e = jnp.exp(s - m)
        p = e / jnp.sum(e, axis=-1, keepdims=True)
        outs.append(jnp.dot(p.astype(BF16), v, preferred_element_type=F32).astype(BF16))
    h_ = jnp.dot(jnp.concatenate(outs, axis=1), wo_ref[...], preferred_element_type=F32)
    y = DN_ALPHA * x + h_
    mu = jnp.mean(y, axis=-1, keepdims=True)
    yc = y - mu
    var = jnp.mean(yc * yc, axis=-1, keepdims=True)
    o_ref[...] = yc * lax.rsqrt(var + LN_EPS) * ln_ref[0:1, :] + ln_ref[1:2, :]


def cross_attn_layer(xf, T, kv, w_q, w_out, ln, tq=512):
    n, D = xf.shape
    M = kv.shape[1]
    tq = _pick(T, tq)
    per = T // tq
    return pl.pallas_call(
        _ca_kernel,
        grid=(n // tq,),
        in_specs=[pl.BlockSpec((tq, D), lambda i: (i, 0)),
                  pl.BlockSpec((D, D), lambda i: (0, 0)),
                  pl.BlockSpec((1, M, D), lambda i: (i // per, 0, 0)),
                  pl.BlockSpec((1, M, D), lambda i: (i // per, 0, 1)),
                  pl.BlockSpec((D, D), lambda i: (0, 0)),
                  pl.BlockSpec((2, D), lambda i: (0, 0))],
        out_specs=pl.BlockSpec((tq, D), lambda i: (i, 0)),
        out_shape=jax.ShapeDtypeStruct((n, D), F32),
        compiler_params=_cparams("parallel"),
        name="cross_attn",
    )(xf, w_q, kv, kv, w_out, ln)


def _moe_kernel(x_ref, wg_ref, wu_ref, wo_ref, gate_ref, o_ref, acc_ref, *, nf):
    f = pl.program_id(2)

    @pl.when(f == 0)
    def _():
        acc_ref[...] = jnp.zeros_like(acc_ref)

    x = x_ref[0].astype(BF16)
    hg = jnp.dot(x, wg_ref[0], preferred_element_type=F32)
    hu = jnp.dot(x, wu_ref[0], preferred_element_type=F32)
    a = (hg * jax.nn.sigmoid(hg) * hu).astype(BF16)
    acc_ref[...] += jnp.dot(a, wo_ref[0], preferred_element_type=F32)

    @pl.when(f == nf - 1)
    def _():
        o_ref[0] = acc_ref[...] * gate_ref[0]


def moe_ffn(xe, w_in, w_out, gate, tm=1024, tf=512):
    E, cap, D = xe.shape
    Fh = w_out.shape[1]
    tm = _pick(cap, tm)
    nf = Fh // tf
    return pl.pallas_call(
        functools.partial(_moe_kernel, nf=nf),
        grid=(E, cap // tm, nf),
        in_specs=[pl.BlockSpec((1, tm, D), lambda e, i, f: (e, i, 0)),
                  pl.BlockSpec((1, D, tf), lambda e, i, f: (e, 0, f)),
                  pl.BlockSpec((1, D, tf), lambda e, i, f: (e, 0, nf + f)),
                  pl.BlockSpec((1, tf, D), lambda e, i, f: (e, f, 0)),
                  pl.BlockSpec((1, tm, 1), lambda e, i, f: (e, i, 0))],
        out_specs=pl.BlockSpec((1, tm, D), lambda e, i, f: (e, i, 0)),
        out_shape=jax.ShapeDtypeStruct((E, cap, D), F32),
        scratch_shapes=[pltpu.VMEM((tm, D), F32)],
        compiler_params=_cparams("parallel", "parallel", "arbitrary"),
        name="moe_ffn",
    )(xe, w_in, w_in, w_out, gate)


_NN = (((1,), (0,)), ((), ()))
_NT = (((1,), (1,)), ((), ()))
_TN = (((0,), (0,)), ((), ()))


def _bdot(a, b, dims=_NN):
    return lax.dot_general(a.astype(BF16), b.astype(BF16), dims, preferred_element_type=F32)


def _split3(x):
    hi = x.astype(BF16)
    r1 = x - hi.astype(F32)
    mid = r1.astype(BF16)
    lo = (r1 - mid.astype(F32)).astype(BF16)
    return hi, mid, lo


def _cumdot(cum, x):
    hi, mid, lo = _split3(x)
    return (lax.dot_general(cum, hi, _NN, preferred_element_type=F32)
            + lax.dot_general(cum, mid, _NN, preferred_element_type=F32)
            + lax.dot_general(cum, lo, _NN, preferred_element_type=F32))


RW_L = 64
RW_TT = 512
RW_W = 128


def _rwkv_kernel(rf, vf, kkf, lwf, kdf, bbf, rb, vb, kkb, lwb, kdb, bbb, of_ref, ob_ref,
                 mf_ref, mb_ref, om_ref, th_ref):
    L = RW_L
    nc = RW_TT // L
    S = 2 * L

    @pl.when(pl.program_id(2) == 0)
    def _():
        mf_ref[...] = jnp.zeros_like(mf_ref)
        mb_ref[...] = jnp.zeros_like(mb_ref)

    row = lax.broadcasted_iota(jnp.int32, (S, S), 0)
    col = lax.broadcasted_iota(jnp.int32, (S, S), 1)
    strict = {False: col < row, True: col > row}
    incl = {False: col <= row, True: col >= row}
    incl2 = {d: jnp.concatenate([incl[d], incl[d]], axis=1) for d in (False, True)}
    eye = (row == col).astype(F32)
    cr = lax.broadcasted_iota(jnp.int32, (L, L), 0)
    cc = lax.broadcasted_iota(jnp.int32, (L, L), 1)
    cum = {False: (cc <= cr).astype(BF16), True: (cc >= cr).astype(BF16)}
    lane = lax.broadcasted_iota(jnp.int32, (L, RW_W), 1)
    m0 = lane < HEAD_C

    def stack(x):
        return jnp.concatenate([jnp.where(m0, x, 0.0), jnp.where(m0, 0.0, x)], axis=0)

    refs = {False: (rf, vf, kkf, lwf, kdf, bbf), True: (rb, vb, kkb, lwb, kdb, bbb)}
    items = [(d, c) for c in range(nc) for d in (False, True)]
    ni = len(items)
    cs = range(ni)
    ld = lambda i, which: refs[items[i][0]][which][0, pl.ds(items[i][1] * L, L), :]
    lws = [ld(i, 3) for i in cs]
    gs = [_cumdot(cum[items[i][0]], lws[i]) for i in cs]
    gls = [(gs[i][0:1, :] if items[i][0] else gs[i][L - 1:L, :]) for i in cs]
    lhs, rhs_a, kbl, vss, rgs, ags = [], [], [], [], [], []
    for i in cs:
        g, gl = gs[i], gls[i]
        eg, eng, egl = jnp.exp(g), jnp.exp(-g), jnp.exp(gl - g)
        kd, bb = ld(i, 4), ld(i, 5)
        rg_s, ag_s = stack(ld(i, 0) * eg), stack(ld(i, 2) * jnp.exp(g - lws[i]))
        rgs.append(rg_s)
        ags.append(ag_s)
        lhs.append(jnp.concatenate([ag_s, rg_s], axis=0))
        rhs_a.append(jnp.concatenate([stack(kd * eng), stack(bb * eng)], axis=0))
        kbl.append(jnp.concatenate([stack(kd * egl), stack(bb * egl)], axis=0))
        vss.append(stack(ld(i, 1)))
    amats = [_bdot(lhs[i], rhs_a[i], _NT) for i in cs]
    aaks = [jnp.where(strict[items[i][0]], amats[i][0:S, 0:S], 0.0) for i in cs]
    ns = [jnp.where(strict[items[i][0]], amats[i][0:S, S:2 * S], 0.0) for i in cs]
    arkb = [jnp.where(incl2[items[i][0]], amats[i][S:2 * S, :], 0.0) for i in cs]
    ts = [eye + n for n in ns]
    ps = [_bdot(n, n) for n in ns]
    avs = [_bdot(aaks[i], vss[i]) for i in cs]
    for _ in range(4):
        pps = [_bdot(ps[i], jnp.concatenate([ps[i], ts[i]], axis=1)) for i in cs]
        ts = [ts[i] + pps[i][:, S:2 * S] for i in cs]
        ps = [pps[i][:, 0:S] for i in cs]
    ts = [ts[i] + _bdot(ps[i], ts[i]) for i in cs]
    w12s = [_bdot(ts[i], jnp.concatenate([ags[i], avs[i]], axis=1)) for i in cs]
    rhss = [jnp.concatenate([jnp.concatenate([jnp.zeros_like(vss[i]), vss[i]], axis=1), w12s[i]], axis=0) for i in cs]
    sts = [_bdot(kbl[i], rhss[i], _TN) for i in cs]
    ots = [_bdot(arkb[i], rhss[i]) for i in cs]
    for i in cs:
        phi = sts[i][:, 0:RW_W] + eye * jnp.exp(gls[i])
        om_ref[i] = jnp.concatenate([ots[i][:, 0:RW_W] + rgs[i], phi], axis=0)
        th_ref[i] = jnp.concatenate([ots[i][:, RW_W:2 * RW_W], sts[i][:, RW_W:2 * RW_W]], axis=0)

    m = {False: mf_ref[...], True: mb_ref[...]}
    outs = {False: of_ref, True: ob_ref}
    for j in range(nc):
        for d in (False, True):
            c = nc - 1 - j if d else j
            i = items.index((d, c))
            res = _bdot(om_ref[i], m[d]) + th_ref[i]
            outs[d][0, pl.ds(c * L, L), :] = res[0:L] + res[L:S]
            m[d] = res[S:2 * S]
    mf_ref[...] = m[False]
    mb_ref[...] = m[True]


def rwkv7_scan(r, v, kk, lw, kd, bb):
    B, T, D = r.shape
    nt = T // RW_TT
    f3 = pl.BlockSpec((1, RW_TT, RW_W), lambda b, h, i: (b, i, h))
    r3 = pl.BlockSpec((1, RW_TT, RW_W), lambda b, h, i: (b, nt - 1 - i, h))
    f4 = pl.BlockSpec((None, 1, RW_TT, RW_W), lambda b, h, i: (0, b, i, h))
    r4 = pl.BlockSpec((None, 1, RW_TT, RW_W), lambda b, h, i: (1, b, nt - 1 - i, h))
    nc = RW_TT // RW_L
    out = jax.ShapeDtypeStruct((B, T, D), F32)
    return pl.pallas_call(
        _rwkv_kernel,
        grid=(B, D // RW_W, nt),
        in_specs=[f3, f3, f3, f4, f4, f4, r3, r3, r3, r4, r4, r4],
        out_specs=[f3, r3],
        out_shape=[out, out],
        scratch_shapes=[pltpu.VMEM((RW_W, RW_W), F32), pltpu.VMEM((RW_W, RW_W), F32),
                        pltpu.VMEM((2 * nc, 4 * RW_L, RW_W), F32),
                        pltpu.VMEM((2 * nc, 4 * RW_L, RW_W), F32)],
        compiler_params=_cparams("parallel", "parallel", "arbitrary"),
        name="rwkv7",
    )(r, v, kk, lw, kd, bb, r, v, kk, lw, kd, bb)


HG_SUB = 16
HG_GRP = 128
HG_TT = 512


def _hgrn2_group(q_ref, v_ref, z_ref, lb_ref, o_ref, st_ref, g0, reverse):
    C, G = HG_SUB, HG_GRP
    log_lb, log_1mlb, one_mlb = lb_ref[0, 0:1, :], lb_ref[0, 1:2, :], lb_ref[0, 2:3, :]
    row = lax.broadcasted_iota(jnp.int32, (G, G), 0)
    col = lax.broadcasted_iota(jnp.int32, (G, G), 1)
    same = (row // C) == (col // C)
    cum = (same & ((col >= row) if reverse else (col <= row))).astype(BF16)
    sub_t = lax.broadcasted_iota(jnp.int32, (C, HEAD_A), 0)
    rows = pl.ds(pl.multiple_of(g0 * G, G), G)
    z = z_ref[0, rows, :]
    a_q = q_ref[0, rows, :]
    vv = v_ref[0, rows, :]
    ls = jnp.minimum(z, 0.0) - jnp.log1p(jnp.exp(-jnp.abs(z)))
    t2 = log_1mlb + ls
    mx = jnp.maximum(log_lb, t2)
    lf = mx + jnp.log1p(jnp.exp(-jnp.abs(log_lb - t2)))
    kk = one_mlb * (1.0 / (1.0 + jnp.exp(z)))
    qq = a_q * (1.0 / (1.0 + jnp.exp(-a_q)))
    b = _cumdot(cum, lf)
    st = st_ref[...]
    order = range(G // C - 1, -1, -1) if reverse else range(G // C)
    outs = [None] * (G // C)
    for j in order:
        sl = slice(j * C, (j + 1) * C)
        bj, qj, kj, vj = b[sl], qq[sl], kk[sl], vv[sl]
        e_in = 0 if reverse else C - 1
        b_end = bj[e_in:e_in + 1, :]
        o = _bdot(qj * jnp.exp(bj), st, _NT)
        for s in range(C):
            keep = (sub_t <= s) if reverse else (sub_t >= s)
            dec = jnp.exp(jnp.where(keep, bj - bj[s:s + 1, :], -1e30))
            sc = jnp.sum(dec * (qj * kj[s:s + 1, :]), axis=-1, keepdims=True)
            o = o + sc * vj[s:s + 1, :]
        outs[j] = o
        st = st * jnp.exp(b_end) + _bdot(vj, kj * jnp.exp(b_end - bj), _TN)
    st_ref[...] = st
    o_ref[0, rows, :] = jnp.concatenate(outs, axis=0)


def _hgrn2_kernel(qf_ref, vf_ref, zf_ref, qb_ref, vb_ref, zb_ref, lbf_ref, lbb_ref, of_ref, ob_ref, stf_ref, stb_ref):
    @pl.when(pl.program_id(2) == 0)
    def _():
        stf_ref[...] = jnp.zeros_like(stf_ref)
        stb_ref[...] = jnp.zeros_like(stb_ref)

    ng = HG_TT // HG_GRP

    def body(gi, carry):
        _hgrn2_group(qf_ref, vf_ref, zf_ref, lbf_ref, of_ref, stf_ref, gi, False)
        _hgrn2_group(qb_ref, vb_ref, zb_ref, lbb_ref, ob_ref, stb_ref, ng - 1 - gi, True)
        return carry

    lax.fori_loop(0, ng, body, 0)


def hgrn2_scan(proj, lbp):
    B, T, _ = proj.shape
    nt = T // HG_TT
    blk = (1, HG_TT, HEAD_A)
    fmap = lambda off: (lambda b, h, i: (b, i, off + h))
    rmap = lambda off: (lambda b, h, i: (b, nt - 1 - i, off + h))
    out = jax.ShapeDtypeStruct((B, T, D_A), F32)
    return pl.pallas_call(
        _hgrn2_kernel,
        grid=(B, H_A, nt),
        in_specs=[pl.BlockSpec(blk, fmap(0)), pl.BlockSpec(blk, fmap(H_A)), pl.BlockSpec(blk, fmap(2 * H_A)),
                  pl.BlockSpec(blk, rmap(0)), pl.BlockSpec(blk, rmap(H_A)), pl.BlockSpec(blk, rmap(3 * H_A)),
                  pl.BlockSpec((1, 3, HEAD_A), lambda b, h, i: (0, 0, h)),
                  pl.BlockSpec((1, 3, HEAD_A), lambda b, h, i: (1, 0, h))],
        out_specs=[pl.BlockSpec(blk, fmap(0)), pl.BlockSpec(blk, rmap(0))],
        out_shape=[out, out],
        scratch_shapes=[pltpu.VMEM((HEAD_A, HEAD_A), F32), pltpu.VMEM((HEAD_A, HEAD_A), F32)],
        compiler_params=_cparams("parallel", "parallel", "arbitrary"),
        name="hgrn2",
    )(proj, proj, proj, proj, proj, proj, lbp, lbp)


RT_TT = 256


def rope_tables(T):
    theta = 1.0 / jnp.power(ROPE_BASE, jnp.linspace(0.0, 1.0, HEAD_B // 2, dtype=F32))
    ang = jnp.arange(T, dtype=F32)[:, None] * theta[None, :]
    cos = jnp.repeat(jnp.cos(ang), 2, axis=1)
    sin = jnp.repeat(jnp.sin(ang), 2, axis=1)
    sign = jnp.where(jnp.arange(HEAD_B) % 2 == 0, -1.0, 1.0).astype(F32)
    return cos, sin * sign[None, :]


def _rope(t, cos, sin):
    lane = lax.broadcasted_iota(jnp.int32, t.shape, 1)
    nxt = pltpu.roll(t, t.shape[1] - 1, axis=1)
    prv = pltpu.roll(t, 1, axis=1)
    return t * cos + jnp.where(lane % 2 == 0, nxt, prv) * sin


def _ret_fwd_kernel(lg_ref, cos_ref, sin_ref, q_ref, k_ref, v_ref, o_ref, st_ref):
    Tt = RT_TT

    @pl.when(pl.program_id(2) == 0)
    def _():
        st_ref[...] = jnp.zeros_like(st_ref)

    lgw = lg_ref[0]
    lg = lgw[:, 0:HEAD_B]
    cos, sin = cos_ref[...], sin_ref[...]
    q = _rope(q_ref[0], cos, sin)
    k = _rope(k_ref[0], cos, sin) * (HEAD_B ** -0.5)
    v = v_ref[0]
    row = lax.broadcasted_iota(jnp.int32, (Tt, Tt), 0)
    col = lax.broadcasted_iota(jnp.int32, (Tt, Tt), 1)
    dmat = jnp.exp(lgw * jnp.abs(row - col).astype(F32))
    pos = lax.broadcasted_iota(jnp.int32, (Tt, 1), 0).astype(F32)
    st = st_ref[...]
    o = _bdot(_bdot(q, k, _NT) * dmat, v)
    o_ref[0] = o + _bdot(q * jnp.exp(lg * (pos + 1.0)), st)
    st_ref[...] = st * jnp.exp(lg * Tt) + _bdot(k * jnp.exp(lg * (Tt - 1.0 - pos)), v, _TN)


def _ret_bwd_kernel(lg_ref, cos_ref, sin_ref, q_ref, k_ref, v_ref, oin_ref, o_ref, st_ref):
    Tt = RT_TT

    @pl.when(pl.program_id(2) == 0)
    def _():
        st_ref[...] = jnp.zeros_like(st_ref)

    lg = lg_ref[0][:, 0:HEAD_B]
    cos, sin = cos_ref[...], sin_ref[...]
    q = _rope(q_ref[0], cos, sin)
    k = _rope(k_ref[0], cos, sin) * (HEAD_B ** -0.5)
    pos = lax.broadcasted_iota(jnp.int32, (Tt, 1), 0).astype(F32)
    st = st_ref[...]
    o_ref[0] = oin_ref[0] + _bdot(q * jnp.exp(lg * (Tt - pos)), st)
    st_ref[...] = st * jnp.exp(lg * Tt) + _bdot(k * jnp.exp(lg * pos), v_ref[0], _TN)


def retention(proj):
    B, T, _ = proj.shape
    nt = T // RT_TT
    blk = (1, RT_TT, HEAD_B)
    qoff = 5 * D_A // HEAD_B
    koff, voff = qoff + H_B, qoff + 2 * H_B
    cos, sin = rope_tables(T)
    lgam = jnp.log1p(-jnp.power(2.0, -5.0 - jnp.arange(H_B, dtype=F32)))
    lgam = jnp.broadcast_to(lgam[:, None, None], (H_B, 1, RT_TT))
    lspec = pl.BlockSpec((1, 1, RT_TT), lambda b, h, i: (h, 0, 0))

    def specs(tix):
        tab = pl.BlockSpec((RT_TT, HEAD_B), lambda b, h, i: (tix(i), 0))
        return [lspec, tab, tab] + [pl.BlockSpec(blk, (lambda off: lambda b, h, i: (b, tix(i), off + h))(off))
                                    for off in (qoff, koff, voff)]
    fwd = lambda i: i
    rev = lambda i: nt - 1 - i
    o1 = pl.pallas_call(
        _ret_fwd_kernel, grid=(B, H_B, nt),
        in_specs=specs(fwd),
        out_specs=pl.BlockSpec(blk, lambda b, h, i: (b, i, h)),
        out_shape=jax.ShapeDtypeStruct((B, T, D_B), F32),
        scratch_shapes=[pltpu.VMEM((HEAD_B, HEAD_B), F32)],
        compiler_params=_cparams("parallel", "parallel", "arbitrary"), name="retention_fwd",
    )(lgam, cos, sin, proj, proj, proj)
    return pl.pallas_call(
        _ret_bwd_kernel, grid=(B, H_B, nt),
        in_specs=specs(rev) + [pl.BlockSpec(blk, lambda b, h, i: (b, nt - 1 - i, h))],
        out_specs=pl.BlockSpec(blk, lambda b, h, i: (b, nt - 1 - i, h)),
        out_shape=jax.ShapeDtypeStruct((B, T, D_B), F32),
        scratch_shapes=[pltpu.VMEM((HEAD_B, HEAD_B), F32)],
        input_output_aliases={6: 0},
        compiler_params=_cparams("parallel", "parallel", "arbitrary"), name="retention_bwd",
    )(lgam, cos, sin, proj, proj, proj, o1)


LANES = 128


def _head_ones():
    r = lax.broadcasted_iota(jnp.int32, (LANES, LANES), 0) // HEAD_C
    c = lax.broadcasted_iota(jnp.int32, (LANES, LANES), 1) // HEAD_C
    return (r == c).astype(BF16)


def _head_sum(x, bd):
    hi = x.astype(BF16)
    lo = (x - hi.astype(F32)).astype(BF16)
    return (lax.dot_general(hi, bd, _NN, preferred_element_type=F32)
            + lax.dot_general(lo, bd, _NN, preferred_element_type=F32))


def _layer_norm_rows(y, w, b):
    mu = jnp.mean(y, axis=-1, keepdims=True)
    yc = y - mu
    var = jnp.mean(yc * yc, axis=-1, keepdims=True)
    return yc * lax.rsqrt(var + LN_EPS) * w + b


def _full_spec(a):
    return pl.BlockSpec(a.shape, lambda i: (0,) * a.ndim)


def _odd_pre_kernel(x_ref, xx_ref, mu_ref, wrkv_ref, w1_ref, w2_ref, a1_ref, a2_ref, g1_ref, g2_ref,
                    vec_ref, r_ref, v_ref, kk_ref, g_ref, kd_ref, bb_ref, lw_ref):
    D = D_MODEL
    x = x_ref[...]
    xx = xx_ref[...]
    mix = lambda p: (x + xx * mu_ref[p:p + 1, :]).astype(BF16)
    r = jnp.dot(mix(0), wrkv_ref[0], preferred_element_type=F32)
    k = jnp.dot(mix(1), wrkv_ref[1], preferred_element_type=F32)
    v = jnp.dot(mix(2), wrkv_ref[2], preferred_element_type=F32)
    hw = jnp.tanh(jnp.dot(mix(3), w1_ref[...], preferred_element_type=F32))
    lora_w = _bdot(hw, w2_ref[...])
    lora_a = _bdot(jnp.dot(mix(4), a1_ref[...], preferred_element_type=F32), a2_ref[...])
    hg = jnp.dot(mix(5), g1_ref[...], preferred_element_type=F32)
    g = _bdot(1.0 / (1.0 + jnp.exp(-hg)), g2_ref[...])
    k_k, k_a = vec_ref[4:5, :], vec_ref[5:6, :]
    bd = _head_ones()
    kq = k * k_k
    kks = []
    for s in range(D // LANES):
        ks = kq[:, s * LANES:(s + 1) * LANES]
        nrm = jnp.sqrt(_head_sum(ks * ks, bd))
        kks.append(ks / jnp.maximum(nrm, 1e-12))
    kk = jnp.concatenate(kks, axis=1)
    r_ref[...] = r.astype(r_ref.dtype)
    v_ref[...] = v.astype(v_ref.dtype)
    kk_ref[...] = kk.astype(kk_ref.dtype)
    g_ref[...] = g.astype(g_ref.dtype)
    for n in range(2):
        u = vec_ref[n:n + 1, :] + lora_w[:, n * D:(n + 1) * D]
        lw_ref[n] = -math.exp(-0.5) / (1.0 + jnp.exp(-u))
        a = 1.0 / (1.0 + jnp.exp(-(vec_ref[2 + n:3 + n, :] + lora_a[:, n * D:(n + 1) * D])))
        kd_ref[n] = (k * (1.0 + (a - 1.0) * k_a)).astype(kd_ref.dtype)
        bb_ref[n] = (-(kk * a)).astype(bb_ref.dtype)


def odd_pre(x, xx, mu, wrkv, w1c, w2bd, a1c, a2bd, g1, g2, vec, tm=256):
    n, D = x.shape
    row = pl.BlockSpec((tm, D), lambda i: (i, 0))
    row2 = pl.BlockSpec((2, tm, D), lambda i: (0, i, 0))
    o1 = jax.ShapeDtypeStruct((n, D), BF16)
    o2 = jax.ShapeDtypeStruct((2, n, D), BF16)
    return pl.pallas_call(
        _odd_pre_kernel, grid=(n // tm,),
        in_specs=[row, row] + [_full_spec(a) for a in (mu, wrkv, w1c, w2bd, a1c, a2bd, g1, g2, vec)],
        out_specs=[row, row, row, row, row2, row2, row2],
        out_shape=[o1, o1, o1, o1, o2, o2, jax.ShapeDtypeStruct((2, n, D), F32)],
        compiler_params=_cparams("parallel"), name="rwkv7_pre",
    )(x, xx, mu, wrkv, w1c, w2bd, a1c, a2bd, g1, g2, vec)


def _odd_post_kernel(of_ref, ob_ref, r_ref, v_ref, g_ref, kd_ref, x_ref, vec_ref, wout_ref, o_ref):
    D = D_MODEL
    bd = _head_ones()
    lnx_w, lnx_b, r_k = vec_ref[0:1, :], vec_ref[1:2, :], vec_ref[2:3, :]
    o = of_ref[...] + ob_ref[...]
    rk = r_ref[...].astype(F32) * r_k * (kd_ref[0].astype(F32) + kd_ref[1].astype(F32))
    v = v_ref[...].astype(F32)
    ys = []
    for s in range(D // LANES):
        sl = slice(s * LANES, (s + 1) * LANES)
        os_ = o[:, sl]
        oc = os_ - _head_sum(os_, bd) * (1.0 / HEAD_C)
        var = _head_sum(oc * oc, bd) * (1.0 / HEAD_C)
        on = oc * lax.rsqrt(var + LNX_EPS) * lnx_w[:, sl] + lnx_b[:, sl]
        ys.append(on + _head_sum(rk[:, sl], bd) * v[:, sl])
    y = jnp.concatenate(ys, axis=1) * g_ref[...].astype(F32)
    h = jnp.dot(y.astype(BF16), wout_ref[...], preferred_element_type=F32)
    o_ref[...] = _layer_norm_rows(DN_ALPHA * x_ref[...] + h, vec_ref[3:4, :], vec_ref[4:5, :])


def odd_post(o_f, o_b, r, v, g, kd, x, vec, w_out, tm=256):
    n, D = x.shape
    row = pl.BlockSpec((tm, D), lambda i: (i, 0))
    row2 = pl.BlockSpec((2, tm, D), lambda i: (0, i, 0))
    return pl.pallas_call(
        _odd_post_kernel, grid=(n // tm,),
        in_specs=[row, row, row, row, row, row2, row, _full_spec(vec), _full_spec(w_out)],
        out_specs=row,
        out_shape=jax.ShapeDtypeStruct((n, D), F32),
        compiler_params=_cparams("parallel"), name="rwkv7_post",
    )(o_f, o_b, r, v, g, kd, x, vec, w_out)


def _proj_ln_kernel(a_ref, w_ref, x_ref, ln_ref, o_ref):
    h = jnp.dot(a_ref[...].astype(BF16), w_ref[...], preferred_element_type=F32)
    o_ref[...] = _layer_norm_rows(DN_ALPHA * x_ref[...] + h, ln_ref[0:1, :], ln_ref[1:2, :])


def proj_ln(a, w, x, ln, tm=512):
    n, K = a.shape
    D = w.shape[1]
    return pl.pallas_call(
        _proj_ln_kernel, grid=(n // tm,),
        in_specs=[pl.BlockSpec((tm, K), lambda i: (i, 0)), pl.BlockSpec((K, D), lambda i: (0, 0)),
                  pl.BlockSpec((tm, D), lambda i: (i, 0)), pl.BlockSpec((2, D), lambda i: (0, 0))],
        out_specs=pl.BlockSpec((tm, D), lambda i: (i, 0)),
        out_shape=jax.ShapeDtypeStruct((n, D), F32),
        compiler_params=_cparams("parallel"), name="proj_res_ln",
    )(a, w, x, ln)


def _add_ln_kernel(h_ref, x_ref, ln_ref, o_ref):
    o_ref[...] = _layer_norm_rows(DN_ALPHA * x_ref[...] + h_ref[...], ln_ref[0:1, :], ln_ref[1:2, :])


def add_ln(h, x, ln, tm=512):
    n, D = x.shape
    row = pl.BlockSpec((tm, D), lambda i: (i, 0))
    return pl.pallas_call(
        _add_ln_kernel, grid=(n // tm,),
        in_specs=[row, row, pl.BlockSpec((2, D), lambda i: (0, 0))],
        out_specs=row, out_shape=jax.ShapeDtypeStruct((n, D), F32),
        compiler_params=_cparams("parallel"), name="res_ln",
    )(h, x, ln)


def _even_post_kernel(oaf_ref, oab_ref, ob_ref, ga_ref, gb_ref, x_ref, vec_ref, wout_ref, o_ref):
    silu = lambda t: t * (1.0 / (1.0 + jnp.exp(-t)))
    oa = oaf_ref[...] + oab_ref[...]
    ob = ob_ref[...]
    ga, gb = ga_ref[...], gb_ref[...]
    na, nb = vec_ref[0:1, :], vec_ref[1:2, :]
    ys = []
    for h in range(H_A):
        sl = slice(h * HEAD_A, (h + 1) * HEAD_A)
        t = oa[:, sl]
        t = t * lax.rsqrt(jnp.mean(t * t, axis=-1, keepdims=True) + 1e-6)
        ys.append(t * na[:, sl] * silu(ga[:, sl]))
    for h in range(H_B):
        sl = slice(h * HEAD_B, (h + 1) * HEAD_B)
        t = ob[:, sl]
        t = t - jnp.mean(t, axis=-1, keepdims=True)
        t = t * lax.rsqrt(jnp.mean(t * t, axis=-1, keepdims=True) + 1e-6)
        ys.append(t * nb[:, sl] * silu(gb[:, sl]))
    y = jnp.concatenate(ys, axis=1)
    h_ = jnp.dot(y.astype(BF16), wout_ref[...], preferred_element_type=F32)
    o_ref[...] = _layer_norm_rows(DN_ALPHA * x_ref[...] + h_, vec_ref[2:3, :], vec_ref[3:4, :])


def even_post(oa_f, oa_b, o_b, proj, x, vec, w_out, tm=256):
    n, D = x.shape
    half = pl.BlockSpec((tm, D_A), lambda i: (i, 0))
    row = pl.BlockSpec((tm, D), lambda i: (i, 0))
    return pl.pallas_call(
        _even_post_kernel, grid=(n // tm,),
        in_specs=[half, half, half, pl.BlockSpec((tm, D_A), lambda i: (i, 4)),
                  pl.BlockSpec((tm, D_B), lambda i: (i, 8)), row,
                  pl.BlockSpec(vec.shape, lambda i: (0, 0)), pl.BlockSpec(w_out.shape, lambda i: (0, 0))],
        out_specs=row,
        out_shape=jax.ShapeDtypeStruct((n, D), F32),
        compiler_params=_cparams("parallel"), name="even_post",
    )(oa_f, oa_b, o_b, proj, proj, x, vec, w_out)


def _even_layer(xf, Bn, T, w_in, lb, norm_a, norm_b, w_out, ln):
    n, D = xf.shape
    proj = pmm(xf, w_in).reshape(Bn, T, -1)
    lbp = jnp.stack([jnp.log(lb), jnp.log1p(-lb), 1.0 - lb], axis=1)
    oa_f, oa_b = hgrn2_scan(proj, lbp)
    o_b = retention(proj)
    z = jnp.zeros((D - D_A,), F32)
    vec = jnp.concatenate([jnp.stack([jnp.concatenate([norm_a, z]), jnp.concatenate([norm_b, z])]), ln,
                           jnp.zeros((4, D), F32)], 0)
    fl = lambda t: t.reshape(n, -1)
    return even_post(fl(oa_f), fl(oa_b), fl(o_b), fl(proj), xf, vec, w_out)


def _odd_weights(p, j):
    w1, w2, a1, a2 = p['od_w1'][j], p['od_w2'][j], p['od_a1'][j], p['od_a2'][j]
    z = jnp.zeros_like(w2[0])
    w2bd = jnp.concatenate([jnp.concatenate([w2[0], z], 1), jnp.concatenate([z, w2[1]], 1)], 0)
    a2bd = jnp.concatenate([jnp.concatenate([a2[0], z], 1), jnp.concatenate([z, a2[1]], 1)], 0)
    w0, a0 = p['od_w0'][j], p['od_a0'][j]
    zero = jnp.zeros_like(w0[0])
    vec_pre = jnp.stack([w0[0], w0[1], a0[0], a0[1], p['od_k_k'][j], p['od_k_a'][j], zero, zero])
    vec_post = jnp.stack([p['od_lnx_w'][j], p['od_lnx_b'][j], p['od_r_k'][j]])
    pre = (p['od_mu'][j], p['od_w_rkv'][j], jnp.concatenate([w1[0], w1[1]], 1), w2bd,
           jnp.concatenate([a1[0], a1[1]], 1), a2bd, p['od_g1'][j], p['od_g2'][j], vec_pre)
    return pre, vec_post


def _odd_layer(xf, Bn, T, pre, vec_post, w_out, ln):
    D = xf.shape[-1]
    x = xf.reshape(Bn, T, D)
    zero = jnp.zeros_like(x[:, :1])
    xx = 0.5 * (jnp.concatenate([zero, x[:, :-1]], 1) + jnp.concatenate([x[:, 1:], zero], 1)) - x
    r, v, kk, g, kd, bb, lw = odd_pre(xf, xx.reshape(-1, D), *pre)
    sh = lambda t: t.reshape(Bn, T, D)
    sh2 = lambda t: t.reshape(2, Bn, T, D)
    o_f, o_b = rwkv7_scan(sh(r), sh(v), sh(kk), sh2(lw), sh2(kd), sh2(bb))
    vec = jnp.concatenate([vec_post, ln, jnp.zeros((3, D), F32)], 0)
    return odd_post(o_f.reshape(-1, D), o_b.reshape(-1, D), r, v, g, kd, xf, vec, w_out)


def _cross_attn_layer(xf, Bn, T, mem, w_q, w_kv, w_out, ln):
    return cross_attn_layer(xf, T, _mm(mem, w_kv), w_q, w_out, ln)


def _moe_layer(xt, w_router, w_in, w_out, ln):
    n, D = xt.shape
    cap = (CAP_FACTOR * n) // N_EXPERTS
    aff = jax.nn.softmax(pmm(xt, w_router), axis=-1)
    gate, idx = lax.top_k(aff.T, cap)
    xe = xt[idx]
    ye = moe_ffn(xe, w_in, w_out, gate[..., None])
    y = jnp.zeros_like(xt).at[idx.reshape(-1)].add(ye.reshape(-1, D))
    return add_ln(y, xt, ln)


def _trunk(x, mem, p):
    Bn, T, D = x.shape
    xf = x.reshape(Bn * T, D)
    p_lb = jax.nn.softmax(p['ev_lb_logits'], axis=1)
    lb_all = jnp.clip(jnp.cumsum(p_lb, axis=1) - p_lb[:, :1], 0.0, 1.0)
    for layer in range(DEPTH):
        j = layer // 2
        ln = lambda s: jnp.stack([p['ln_w'][layer, s], p['ln_b'][layer, s]])
        if layer % 2 == 0:
            xf = _even_layer(xf, Bn, T, p['ev_w_in'][j], lb_all[:, j], p['ev_norm_a'][j], p['ev_norm_b'][j],
                             p['ev_w_out'][j], ln(0))
        else:
            pre, vec_post = _odd_weights(p, j)
            xf = _odd_layer(xf, Bn, T, pre, vec_post, p['od_w_out'][j], ln(0))
        xf = _cross_attn_layer(xf, Bn, T, mem, p['ca_w_q'][layer], p['ca_w_kv'][layer], p['ca_w_out'][layer], ln(1))
        xf = _moe_layer(xf, p['moe_router'][layer], p['moe_w_in'][layer], p['moe_w_out'][layer], ln(2))
    return xf.reshape(Bn, T, D)


_MATMUL_WEIGHTS = ('ev_w_in', 'ev_w_out', 'od_w_rkv', 'od_w1', 'od_w2', 'od_a1', 'od_a2', 'od_g1', 'od_g2',
                   'od_w_out', 'ca_w_q', 'ca_w_kv', 'ca_w_out', 'moe_router', 'moe_w_in', 'moe_w_out')


def kernel(x_prompt, x_sample, mem_prompt, mem_sample, ev_w_in, ev_lb_logits, ev_norm_a, ev_norm_b, ev_w_out, od_mu, od_w_rkv, od_w0, od_w1, od_w2, od_a0, od_a1, od_a2, od_g1, od_g2, od_k_k, od_k_a, od_r_k, od_lnx_w, od_lnx_b, od_w_out, ca_w_q, ca_w_kv, ca_w_out, moe_router, moe_w_in, moe_w_out, ln_w, ln_b):
    params = dict(ev_w_in=ev_w_in, ev_lb_logits=ev_lb_logits, ev_norm_a=ev_norm_a, ev_norm_b=ev_norm_b,
                  ev_w_out=ev_w_out, od_mu=od_mu, od_w_rkv=od_w_rkv, od_w0=od_w0, od_w1=od_w1, od_w2=od_w2,
                  od_a0=od_a0, od_a1=od_a1, od_a2=od_a2, od_g1=od_g1, od_g2=od_g2, od_k_k=od_k_k,
                  od_k_a=od_k_a, od_r_k=od_r_k, od_lnx_w=od_lnx_w, od_lnx_b=od_lnx_b, od_w_out=od_w_out,
                  ca_w_q=ca_w_q, ca_w_kv=ca_w_kv, ca_w_out=ca_w_out, moe_router=moe_router,
                  moe_w_in=moe_w_in, moe_w_out=moe_w_out, ln_w=ln_w, ln_b=ln_b)
    for name in _MATMUL_WEIGHTS:
        params[name] = params[name].astype(BF16)
    y_prompt = _trunk(x_prompt, mem_prompt, params)
    y_sample = _trunk(x_sample, mem_sample, params)
    return (y_prompt, y_sample)
```

```python
import functools
import math

import jax
import jax.numpy as jnp
from jax import lax
from jax.experimental import pallas as pl
from jax.experimental.pallas import tpu as pltpu

F32 = jnp.float32
BF16 = jnp.bfloat16

D_MODEL = 1024
DEPTH = 4
D_A = D_MODEL // 2
HEAD_A = 128
H_A = D_A // HEAD_A
CHUNK_A = 32
D_B = D_MODEL // 2
HEAD_B = 128
H_B = D_B // HEAD_B
CHUNK_B = 128
ROPE_BASE = 10000.0
HEAD_C = 64
H_C = D_MODEL // HEAD_C
LNX_EPS = 64e-5
H_CA = 4
HEAD_CA = D_MODEL // H_CA
N_EXPERTS = 16
D_EXPERT = 2048
CAP_FACTOR = 2
DN_ALPHA = (2.0 * DEPTH) ** 0.25
LN_EPS = 1e-5

VMEM_LIMIT = 48 * 1024 * 1024


def _cparams(*sem):
    return pltpu.CompilerParams(dimension_semantics=sem, vmem_limit_bytes=VMEM_LIMIT)


def _mm_kernel(x_ref, w_ref, o_ref):
    o_ref[...] = jnp.dot(x_ref[...].astype(BF16), w_ref[...],
                         preferred_element_type=F32).astype(o_ref.dtype)


def _pick(n, pref):
    t = min(n, pref)
    while n % t:
        t //= 2
    return t


def pmm(x, w, tm=1024, tn=512):
    M, K = x.shape
    N = w.shape[1]
    tm = _pick(M, tm)
    tn = _pick(N, tn) if N % 128 == 0 else N
    return pl.pallas_call(
        _mm_kernel,
        grid=(M // tm, N // tn),
        in_specs=[pl.BlockSpec((tm, K), lambda i, j: (i, 0)),
                  pl.BlockSpec((K, tn), lambda i, j: (0, j))],
        out_specs=pl.BlockSpec((tm, tn), lambda i, j: (i, j)),
        out_shape=jax.ShapeDtypeStruct((M, N), F32),
        compiler_params=_cparams("parallel", "parallel"),
        name="dense_proj",
    )(x, w)


def _mm(x, w):
    lead = x.shape[:-1]
    return pmm(x.reshape(-1, x.shape[-1]), w).reshape(lead + (w.shape[1],))


def _ca_kernel(x_ref, wq_ref, k_ref, v_ref, wo_ref, ln_ref, o_ref):
    scale = HEAD_CA ** -0.5
    x = x_ref[...]
    q_all = jnp.dot(x.astype(BF16), wq_ref[...], preferred_element_type=F32).astype(BF16)
    outs = []
    for h in range(H_CA):
        sl = slice(h * HEAD_CA, (h + 1) * HEAD_CA)
        k = k_ref[0, :, sl].astype(BF16)
        v = v_ref[0, :, sl].astype(BF16)
        s = lax.dot_general(q_all[:, sl], k, (((1,), (1,)), ((), ())), preferred_element_type=F32) * scale
        m = jnp.max(s, axis=-1, keepdims=True)
        e = jnp.exp(s - m)
        p = e / jnp.sum(e, axis=-1, keepdims=True)
        outs.append(jnp.dot(p.astype(BF16), v, preferred_element_type=F32).astype(BF16))
    h_ = jnp.dot(jnp.concatenate(outs, axis=1), wo_ref[...], preferred_element_type=F32)
    y = DN_ALPHA * x + h_
    mu = jnp.mean(y, axis=-1, keepdims=True)
    yc = y - mu
    var = jnp.mean(yc * yc, axis=-1, keepdims=True)
    o_ref[...] = yc * lax.rsqrt(var + LN_EPS) * ln_ref[0:1, :] + ln_ref[1:2, :]


def cross_attn_layer(xf, T, kv, w_q, w_out, ln, tq=512):
    n, D = xf.shape
    M = kv.shape[1]
    tq = _pick(T, tq)
    per = T // tq
    return pl.pallas_call(
        _ca_kernel,
        grid=(n // tq,),
        in_specs=[pl.BlockSpec((tq, D), lambda i: (i, 0)),
                  pl.BlockSpec((D, D), lambda i: (0, 0)),
                  pl.BlockSpec((1, M, D), lambda i: (i // per, 0, 0)),
                  pl.BlockSpec((1, M, D), lambda i: (i // per, 0, 1)),
                  pl.BlockSpec((D, D), lambda i: (0, 0)),
                  pl.BlockSpec((2, D), lambda i: (0, 0))],
        out_specs=pl.BlockSpec((tq, D), lambda i: (i, 0)),
        out_shape=jax.ShapeDtypeStruct((n, D), F32),
        compiler_params=_cparams("parallel"),
        name="cross_attn",
    )(xf, w_q, kv, kv, w_out, ln)


def _moe_kernel(x_ref, wg_ref, wu_ref, wo_ref, gate_ref, o_ref, acc_ref, *, nf):
    f = pl.program_id(2)

    @pl.when(f == 0)
    def _():
        acc_ref[...] = jnp.zeros_like(acc_ref)

    x = x_ref[0].astype(BF16)
    hg = jnp.dot(x, wg_ref[0], preferred_element_type=F32)
    hu = jnp.dot(x, wu_ref[0], preferred_element_type=F32)
    a = (hg * jax.nn.sigmoid(hg) * hu).astype(BF16)
    acc_ref[...] += jnp.dot(a, wo_ref[0], preferred_element_type=F32)

    @pl.when(f == nf - 1)
    def _():
        o_ref[0] = acc_ref[...] * gate_ref[0]


def moe_ffn(xe, w_in, w_out, gate, layer, tm=1024, tf=512):
    E, cap, D = xe.shape
    Fh = w_out.shape[2]
    tm = _pick(cap, tm)
    nf = Fh // tf
    return pl.pallas_call(
        functools.partial(_moe_kernel, nf=nf),
        grid=(E, cap // tm, nf),
        in_specs=[pl.BlockSpec((1, tm, D), lambda e, i, f: (e, i, 0)),
                  pl.BlockSpec((None, 1, D, tf), lambda e, i, f: (layer, e, 0, f)),
                  pl.BlockSpec((None, 1, D, tf), lambda e, i, f: (layer, e, 0, nf + f)),
                  pl.BlockSpec((None, 1, tf, D), lambda e, i, f: (layer, e, f, 0)),
                  pl.BlockSpec((1, tm, 1), lambda e, i, f: (e, i, 0))],
        out_specs=pl.BlockSpec((1, tm, D), lambda e, i, f: (e, i, 0)),
        out_shape=jax.ShapeDtypeStruct((E, cap, D), F32),
        scratch_shapes=[pltpu.VMEM((tm, D), F32)],
        compiler_params=_cparams("parallel", "parallel", "arbitrary"),
        name="moe_ffn",
    )(xe, w_in, w_in, w_out, gate)


_NN = (((1,), (0,)), ((), ()))
_NT = (((1,), (1,)), ((), ()))
_TN = (((0,), (0,)), ((), ()))


def _bdot(a, b, dims=_NN):
    return lax.dot_general(a.astype(BF16), b.astype(BF16), dims, preferred_element_type=F32)


def _split3(x):
    hi = x.astype(BF16)
    r1 = x - hi.astype(F32)
    mid = r1.astype(BF16)
    lo = (r1 - mid.astype(F32)).astype(BF16)
    return hi, mid, lo


def _cumdot(cum, x):
    hi, mid, lo = _split3(x)
    return (lax.dot_general(cum, hi, _NN, preferred_element_type=F32)
            + lax.dot_general(cum, mid, _NN, preferred_element_type=F32)
            + lax.dot_general(cum, lo, _NN, preferred_element_type=F32))


RW_L = 64
RW_TT = 512
RW_W = 128


def _rwkv_kernel(rf, vf, kkf, lwf, kdf, bbf, rb, vb, kkb, lwb, kdb, bbb, of_ref, ob_ref,
                 mf_ref, mb_ref, om_ref, th_ref):
    L = RW_L
    nc = RW_TT // L
    S = 2 * L

    @pl.when(pl.program_id(2) == 0)
    def _():
        mf_ref[...] = jnp.zeros_like(mf_ref)
        mb_ref[...] = jnp.zeros_like(mb_ref)

    row = lax.broadcasted_iota(jnp.int32, (S, S), 0)
    col = lax.broadcasted_iota(jnp.int32, (S, S), 1)
    strict = {False: col < row, True: col > row}
    incl = {False: col <= row, True: col >= row}
    incl2 = {d: jnp.concatenate([incl[d], incl[d]], axis=1) for d in (False, True)}
    eye = (row == col).astype(F32)
    cr = lax.broadcasted_iota(jnp.int32, (L, L), 0)
    cc = lax.broadcasted_iota(jnp.int32, (L, L), 1)
    cum = {False: (cc <= cr).astype(BF16), True: (cc >= cr).astype(BF16)}
    lane = lax.broadcasted_iota(jnp.int32, (L, RW_W), 1)
    m0 = lane < HEAD_C

    def stack(x):
        return jnp.concatenate([jnp.where(m0, x, 0.0), jnp.where(m0, 0.0, x)], axis=0)

    refs = {False: (rf, vf, kkf, lwf, kdf, bbf), True: (rb, vb, kkb, lwb, kdb, bbb)}
    items = [(d, c) for c in range(nc) for d in (False, True)]
    ni = len(items)
    cs = range(ni)
    ld = lambda i, which: refs[items[i][0]][which][0, pl.ds(items[i][1] * L, L), :]
    lws = [ld(i, 3) for i in cs]
    gs = [_cumdot(cum[items[i][0]], lws[i]) for i in cs]
    gls = [(gs[i][0:1, :] if items[i][0] else gs[i][L - 1:L, :]) for i in cs]
    lhs, rhs_a, kbl, vss, rgs, ags = [], [], [], [], [], []
    for i in cs:
        g, gl = gs[i], gls[i]
        eg, eng, egl = jnp.exp(g), jnp.exp(-g), jnp.exp(gl - g)
        kd, bb = ld(i, 4), ld(i, 5)
        rg_s, ag_s = stack(ld(i, 0) * eg), stack(ld(i, 2) * jnp.exp(g - lws[i]))
        rgs.append(rg_s)
        ags.append(ag_s)
        lhs.append(jnp.concatenate([ag_s, rg_s], axis=0))
        rhs_a.append(jnp.concatenate([stack(kd * eng), stack(bb * eng)], axis=0))
        kbl.append(jnp.concatenate([stack(kd * egl), stack(bb * egl)], axis=0))
        vss.append(stack(ld(i, 1)))
    amats = [_bdot(lhs[i], rhs_a[i], _NT) for i in cs]
    aaks = [jnp.where(strict[items[i][0]], amats[i][0:S, 0:S], 0.0) for i in cs]
    ns = [jnp.where(strict[items[i][0]], amats[i][0:S, S:2 * S], 0.0) for i in cs]
    arkb = [jnp.where(incl2[items[i][0]], amats[i][S:2 * S, :], 0.0) for i in cs]
    ts = [eye + n for n in ns]
    ps = [_bdot(n, n) for n in ns]
    avs = [_bdot(aaks[i], vss[i]) for i in cs]
    for _ in range(4):
        pps = [_bdot(ps[i], jnp.concatenate([ps[i], ts[i]], axis=1)) for i in cs]
        ts = [ts[i] + pps[i][:, S:2 * S] for i in cs]
        ps = [pps[i][:, 0:S] for i in cs]
    ts = [ts[i] + _bdot(ps[i], ts[i]) for i in cs]
    w12s = [_bdot(ts[i], jnp.concatenate([ags[i], avs[i]], axis=1)) for i in cs]
    rhss = [jnp.concatenate([jnp.concatenate([jnp.zeros_like(vss[i]), vss[i]], axis=1), w12s[i]], axis=0) for i in cs]
    sts = [_bdot(kbl[i], rhss[i], _TN) for i in cs]
    ots = [_bdot(arkb[i], rhss[i]) for i in cs]
    for i in cs:
        phi = sts[i][:, 0:RW_W] + eye * jnp.exp(gls[i])
        om_ref[i] = jnp.concatenate([ots[i][:, 0:RW_W] + rgs[i], phi], axis=0)
        th_ref[i] = jnp.concatenate([ots[i][:, RW_W:2 * RW_W], sts[i][:, RW_W:2 * RW_W]], axis=0)

    m = {False: mf_ref[...], True: mb_ref[...]}
    outs = {False: of_ref, True: ob_ref}
    for j in range(nc):
        for d in (False, True):
            c = nc - 1 - j if d else j
            i = items.index((d, c))
            res = _bdot(om_ref[i], m[d]) + th_ref[i]
            outs[d][0, pl.ds(c * L, L), :] = res[0:L] + res[L:S]
            m[d] = res[S:2 * S]
    mf_ref[...] = m[False]
    mb_ref[...] = m[True]


def rwkv7_scan(r, v, kk, lw, kd, bb):
    B, T, D = r.shape
    nt = T // RW_TT
    f3 = pl.BlockSpec((1, RW_TT, RW_W), lambda b, h, i: (b, i, h))
    r3 = pl.BlockSpec((1, RW_TT, RW_W), lambda b, h, i: (b, nt - 1 - i, h))
    f4 = pl.BlockSpec((None, 1, RW_TT, RW_W), lambda b, h, i: (0, b, i, h))
    r4 = pl.BlockSpec((None, 1, RW_TT, RW_W), lambda b, h, i: (1, b, nt - 1 - i, h))
    nc = RW_TT // RW_L
    out = jax.ShapeDtypeStruct((B, T, D), F32)
    return pl.pallas_call(
        _rwkv_kernel,
        grid=(B, D // RW_W, nt),
        in_specs=[f3, f3, f3, f4, f4, f4, r3, r3, r3, r4, r4, r4],
        out_specs=[f3, r3],
        out_shape=[out, out],
        scratch_shapes=[pltpu.VMEM((RW_W, RW_W), F32), pltpu.VMEM((RW_W, RW_W), F32),
                        pltpu.VMEM((2 * nc, 4 * RW_L, RW_W), F32),
                        pltpu.VMEM((2 * nc, 4 * RW_L, RW_W), F32)],
        compiler_params=_cparams("parallel", "parallel", "arbitrary"),
        name="rwkv7",
    )(r, v, kk, lw, kd, bb, r, v, kk, lw, kd, bb)


HG_SUB = 16
HG_GRP = 128
HG_TT = 512


def _hgrn2_group(q_ref, v_ref, z_ref, lb_ref, o_ref, st_ref, g0, reverse):
    C, G = HG_SUB, HG_GRP
    log_lb, log_1mlb, one_mlb = lb_ref[0, 0:1, :], lb_ref[0, 1:2, :], lb_ref[0, 2:3, :]
    row = lax.broadcasted_iota(jnp.int32, (G, G), 0)
    col = lax.broadcasted_iota(jnp.int32, (G, G), 1)
    same = (row // C) == (col // C)
    cum = (same & ((col >= row) if reverse else (col <= row))).astype(BF16)
    sub_t = lax.broadcasted_iota(jnp.int32, (C, HEAD_A), 0)
    rows = pl.ds(pl.multiple_of(g0 * G, G), G)
    z = z_ref[0, rows, :]
    a_q = q_ref[0, rows, :]
    vv = v_ref[0, rows, :]
    ls = jnp.minimum(z, 0.0) - jnp.log1p(jnp.exp(-jnp.abs(z)))
    t2 = log_1mlb + ls
    mx = jnp.maximum(log_lb, t2)
    lf = mx + jnp.log1p(jnp.exp(-jnp.abs(log_lb - t2)))
    kk = one_mlb * (1.0 / (1.0 + jnp.exp(z)))
    qq = a_q * (1.0 / (1.0 + jnp.exp(-a_q)))
    b = _cumdot(cum, lf)
    st = st_ref[...]
    order = range(G // C - 1, -1, -1) if reverse else range(G // C)
    outs = [None] * (G // C)
    for j in order:
        sl = slice(j * C, (j + 1) * C)
        bj, qj, kj, vj = b[sl], qq[sl], kk[sl], vv[sl]
        e_in = 0 if reverse else C - 1
        b_end = bj[e_in:e_in + 1, :]
        o = _bdot(qj * jnp.exp(bj), st, _NT)
        for s in range(C):
            keep = (sub_t <= s) if reverse else (sub_t >= s)
            dec = jnp.exp(jnp.where(keep, bj - bj[s:s + 1, :], -1e30))
            sc = jnp.sum(dec * (qj * kj[s:s + 1, :]), axis=-1, keepdims=True)
            o = o + sc * vj[s:s + 1, :]
        outs[j] = o
        st = st * jnp.exp(b_end) + _bdot(vj, kj * jnp.exp(b_end - bj), _TN)
    st_ref[...] = st
    o_ref[0, rows, :] = jnp.concatenate(outs, axis=0)


def _hgrn2_kernel(qf_ref, vf_ref, zf_ref, qb_ref, vb_ref, zb_ref, lbf_ref, lbb_ref, of_ref, ob_ref, stf_ref, stb_ref):
    @pl.when(pl.program_id(2) == 0)
    def _():
        stf_ref[...] = jnp.zeros_like(stf_ref)
        stb_ref[...] = jnp.zeros_like(stb_ref)

    ng = HG_TT // HG_GRP

    def body(gi, carry):
        _hgrn2_group(qf_ref, vf_ref, zf_ref, lbf_ref, of_ref, stf_ref, gi, False)
        _hgrn2_group(qb_ref, vb_ref, zb_ref, lbb_ref, ob_ref, stb_ref, ng - 1 - gi, True)
        return carry

    lax.fori_loop(0, ng, body, 0)


def hgrn2_scan(proj, lbp):
    B, T, _ = proj.shape
    nt = T // HG_TT
    blk = (1, HG_TT, HEAD_A)
    fmap = lambda off: (lambda b, h, i: (b, i, off + h))
    rmap = lambda off: (lambda b, h, i: (b, nt - 1 - i, off + h))
    out = jax.ShapeDtypeStruct((B, T, D_A), F32)
    return pl.pallas_call(
        _hgrn2_kernel,
        grid=(B, H_A, nt),
        in_specs=[pl.BlockSpec(blk, fmap(0)), pl.BlockSpec(blk, fmap(H_A)), pl.BlockSpec(blk, fmap(2 * H_A)),
                  pl.BlockSpec(blk, rmap(0)), pl.BlockSpec(blk, rmap(H_A)), pl.BlockSpec(blk, rmap(3 * H_A)),
                  pl.BlockSpec((1, 3, HEAD_A), lambda b, h, i: (0, 0, h)),
                  pl.BlockSpec((1, 3, HEAD_A), lambda b, h, i: (1, 0, h))],
        out_specs=[pl.BlockSpec(blk, fmap(0)), pl.BlockSpec(blk, rmap(0))],
        out_shape=[out, out],
        scratch_shapes=[pltpu.VMEM((HEAD_A, HEAD_A), F32), pltpu.VMEM((HEAD_A, HEAD_A), F32)],
        compiler_params=_cparams("parallel", "parallel", "arbitrary"),
        name="hgrn2",
    )(proj, proj, proj, proj, proj, proj, lbp, lbp)


RT_TT = 256


def rope_tables(T):
    theta = 1.0 / jnp.power(ROPE_BASE, jnp.linspace(0.0, 1.0, HEAD_B // 2, dtype=F32))
    ang = jnp.arange(T, dtype=F32)[:, None] * theta[None, :]
    cos = jnp.repeat(jnp.cos(ang), 2, axis=1)
    sin = jnp.repeat(jnp.sin(ang), 2, axis=1)
    sign = jnp.where(jnp.arange(HEAD_B) % 2 == 0, -1.0, 1.0).astype(F32)
    return cos, sin * sign[None, :]


def _rope(t, cos, sin):
    lane = lax.broadcasted_iota(jnp.int32, t.shape, 1)
    nxt = pltpu.roll(t, t.shape[1] - 1, axis=1)
    prv = pltpu.roll(t, 1, axis=1)
    return t * cos + jnp.where(lane % 2 == 0, nxt, prv) * sin


def _ret_fwd_kernel(lg_ref, cos_ref, sin_ref, q_ref, k_ref, v_ref, o_ref, st_ref):
    Tt = RT_TT

    @pl.when(pl.program_id(2) == 0)
    def _():
        st_ref[...] = jnp.zeros_like(st_ref)

    lgw = lg_ref[0]
    lg = lgw[:, 0:HEAD_B]
    cos, sin = cos_ref[...], sin_ref[...]
    q = _rope(q_ref[0], cos, sin)
    k = _rope(k_ref[0], cos, sin) * (HEAD_B ** -0.5)
    v = v_ref[0]
    row = lax.broadcasted_iota(jnp.int32, (Tt, Tt), 0)
    col = lax.broadcasted_iota(jnp.int32, (Tt, Tt), 1)
    dmat = jnp.exp(lgw * jnp.abs(row - col).astype(F32))
    pos = lax.broadcasted_iota(jnp.int32, (Tt, 1), 0).astype(F32)
    st = st_ref[...]
    o = _bdot(_bdot(q, k, _NT) * dmat, v)
    o_ref[0] = o + _bdot(q * jnp.exp(lg * (pos + 1.0)), st)
    st_ref[...] = st * jnp.exp(lg * Tt) + _bdot(k * jnp.exp(lg * (Tt - 1.0 - pos)), v, _TN)


def _ret_bwd_kernel(lg_ref, cos_ref, sin_ref, q_ref, k_ref, v_ref, oin_ref, o_ref, st_ref):
    Tt = RT_TT

    @pl.when(pl.program_id(2) == 0)
    def _():
        st_ref[...] = jnp.zeros_like(st_ref)

    lg = lg_ref[0][:, 0:HEAD_B]
    cos, sin = cos_ref[...], sin_ref[...]
    q = _rope(q_ref[0], cos, sin)
    k = _rope(k_ref[0], cos, sin) * (HEAD_B ** -0.5)
    pos = lax.broadcasted_iota(jnp.int32, (Tt, 1), 0).astype(F32)
    st = st_ref[...]
    o_ref[0] = oin_ref[0] + _bdot(q * jnp.exp(lg * (Tt - pos)), st)
    st_ref[...] = st * jnp.exp(lg * Tt) + _bdot(k * jnp.exp(lg * pos), v_ref[0], _TN)


def retention(proj):
    B, T, _ = proj.shape
    nt = T // RT_TT
    blk = (1, RT_TT, HEAD_B)
    qoff = 5 * D_A // HEAD_B
    koff, voff = qoff + H_B, qoff + 2 * H_B
    cos, sin = rope_tables(T)
    lgam = jnp.log1p(-jnp.power(2.0, -5.0 - jnp.arange(H_B, dtype=F32)))
    lgam = jnp.broadcast_to(lgam[:, None, None], (H_B, 1, RT_TT))
    lspec = pl.BlockSpec((1, 1, RT_TT), lambda b, h, i: (h, 0, 0))

    def specs(tix):
        tab = pl.BlockSpec((RT_TT, HEAD_B), lambda b, h, i: (tix(i), 0))
        return [lspec, tab, tab] + [pl.BlockSpec(blk, (lambda off: lambda b, h, i: (b, tix(i), off + h))(off))
                                    for off in (qoff, koff, voff)]
    fwd = lambda i: i
    rev = lambda i: nt - 1 - i
    o1 = pl.pallas_call(
        _ret_fwd_kernel, grid=(B, H_B, nt),
        in_specs=specs(fwd),
        out_specs=pl.BlockSpec(blk, lambda b, h, i: (b, i, h)),
        out_shape=jax.ShapeDtypeStruct((B, T, D_B), F32),
        scratch_shapes=[pltpu.VMEM((HEAD_B, HEAD_B), F32)],
        compiler_params=_cparams("parallel", "parallel", "arbitrary"), name="retention_fwd",
    )(lgam, cos, sin, proj, proj, proj)
    return pl.pallas_call(
        _ret_bwd_kernel, grid=(B, H_B, nt),
        in_specs=specs(rev) + [pl.BlockSpec(blk, lambda b, h, i: (b, nt - 1 - i, h))],
        out_specs=pl.BlockSpec(blk, lambda b, h, i: (b, nt - 1 - i, h)),
        out_shape=jax.ShapeDtypeStruct((B, T, D_B), F32),
        scratch_shapes=[pltpu.VMEM((HEAD_B, HEAD_B), F32)],
        input_output_aliases={6: 0},
        compiler_params=_cparams("parallel", "parallel", "arbitrary"), name="retention_bwd",
    )(lgam, cos, sin, proj, proj, proj, o1)


LANES = 128


def _head_ones():
    r = lax.broadcasted_iota(jnp.int32, (LANES, LANES), 0) // HEAD_C
    c = lax.broadcasted_iota(jnp.int32, (LANES, LANES), 1) // HEAD_C
    return (r == c).astype(BF16)


def _head_sum(x, bd):
    hi = x.astype(BF16)
    lo = (x - hi.astype(F32)).astype(BF16)
    return (lax.dot_general(hi, bd, _NN, preferred_element_type=F32)
            + lax.dot_general(lo, bd, _NN, preferred_element_type=F32))


def _layer_norm_rows(y, w, b):
    mu = jnp.mean(y, axis=-1, keepdims=True)
    yc = y - mu
    var = jnp.mean(yc * yc, axis=-1, keepdims=True)
    return yc * lax.rsqrt(var + LN_EPS) * w + b


def _full_spec(a):
    return pl.BlockSpec(a.shape, lambda i: (0,) * a.ndim)


def _odd_pre_kernel(x_ref, xp_ref, xn_ref, mu_ref, wrkv_ref, w1_ref, w2_ref, a1_ref, a2_ref, g1_ref, g2_ref,
                    vec_ref, r_ref, v_ref, kk_ref, g_ref, kd_ref, bb_ref, lw_ref, *, tiles_per_seq):
    D = D_MODEL
    x = x_ref[...]
    tm = x.shape[0]
    t_in_seq = pl.program_id(0) % tiles_per_seq
    prev_row = jnp.where(t_in_seq == 0, 0.0, xp_ref[0, 7:8, :])
    next_row = jnp.where(t_in_seq == tiles_per_seq - 1, 0.0, xn_ref[0, 0:1, :])
    rid = lax.broadcasted_iota(jnp.int32, x.shape, 0)
    x_prev = jnp.where(rid == 0, prev_row, pltpu.roll(x, 1, axis=0))
    x_next = jnp.where(rid == tm - 1, next_row, pltpu.roll(x, tm - 1, axis=0))
    xx = 0.5 * (x_prev + x_next) - x
    mix = lambda p: (x + xx * mu_ref[p:p + 1, :]).astype(BF16)
    r = jnp.dot(mix(0), wrkv_ref[0], preferred_element_type=F32)
    k = jnp.dot(mix(1), wrkv_ref[1], preferred_element_type=F32)
    v = jnp.dot(mix(2), wrkv_ref[2], preferred_element_type=F32)
    hw = jnp.tanh(jnp.dot(mix(3), w1_ref[...], preferred_element_type=F32))
    lora_w = _bdot(hw, w2_ref[...])
    lora_a = _bdot(jnp.dot(mix(4), a1_ref[...], preferred_element_type=F32), a2_ref[...])
    hg = jnp.dot(mix(5), g1_ref[...], preferred_element_type=F32)
    g = _bdot(1.0 / (1.0 + jnp.exp(-hg)), g2_ref[...])
    k_k, k_a = vec_ref[4:5, :], vec_ref[5:6, :]
    bd = _head_ones()
    kq = k * k_k
    kks = []
    for s in range(D // LANES):
        ks = kq[:, s * LANES:(s + 1) * LANES]
        nrm = jnp.sqrt(_head_sum(ks * ks, bd))
        kks.append(ks / jnp.maximum(nrm, 1e-12))
    kk = jnp.concatenate(kks, axis=1)
    r_ref[...] = r.astype(r_ref.dtype)
    v_ref[...] = v.astype(v_ref.dtype)
    kk_ref[...] = kk.astype(kk_ref.dtype)
    g_ref[...] = g.astype(g_ref.dtype)
    for n in range(2):
        u = vec_ref[n:n + 1, :] + lora_w[:, n * D:(n + 1) * D]
        lw_ref[n] = -math.exp(-0.5) / (1.0 + jnp.exp(-u))
        a = 1.0 / (1.0 + jnp.exp(-(vec_ref[2 + n:3 + n, :] + lora_a[:, n * D:(n + 1) * D])))
        kd_ref[n] = (k * (1.0 + (a - 1.0) * k_a)).astype(kd_ref.dtype)
        bb_ref[n] = (-(kk * a)).astype(bb_ref.dtype)


def odd_pre(x, T, mu, wrkv, w1c, w2bd, a1c, a2bd, g1, g2, vec, tm=256):
    n, D = x.shape
    row = pl.BlockSpec((tm, D), lambda i: (i, 0))
    x8 = x.reshape(n // 8, 8, D)
    prev8 = pl.BlockSpec((1, 8, D), lambda i: (jnp.maximum(i * (tm // 8) - 1, 0), 0, 0))
    next8 = pl.BlockSpec((1, 8, D), lambda i: (jnp.minimum((i + 1) * (tm // 8), n // 8 - 1), 0, 0))
    row2 = pl.BlockSpec((2, tm, D), lambda i: (0, i, 0))
    o1 = jax.ShapeDtypeStruct((n, D), BF16)
    o2 = jax.ShapeDtypeStruct((2, n, D), BF16)
    return pl.pallas_call(
        functools.partial(_odd_pre_kernel, tiles_per_seq=T // tm), grid=(n // tm,),
        in_specs=[row, prev8, next8] + [_full_spec(a) for a in (mu, wrkv, w1c, w2bd, a1c, a2bd, g1, g2, vec)],
        out_specs=[row, row, row, row, row2, row2, row2],
        out_shape=[o1, o1, o1, o1, o2, o2, jax.ShapeDtypeStruct((2, n, D), F32)],
        compiler_params=_cparams("parallel"), name="rwkv7_pre",
    )(x, x8, x8, mu, wrkv, w1c, w2bd, a1c, a2bd, g1, g2, vec)


def _odd_post_kernel(of_ref, ob_ref, r_ref, v_ref, g_ref, kd_ref, x_ref, vec_ref, wout_ref, o_ref):
    D = D_MODEL
    bd = _head_ones()
    lnx_w, lnx_b, r_k = vec_ref[0:1, :], vec_ref[1:2, :], vec_ref[2:3, :]
    o = of_ref[...] + ob_ref[...]
    rk = r_ref[...].astype(F32) * r_k * (kd_ref[0].astype(F32) + kd_ref[1].astype(F32))
    v = v_ref[...].astype(F32)
    ys = []
    for s in range(D // LANES):
        sl = slice(s * LANES, (s + 1) * LANES)
        os_ = o[:, sl]
        oc = os_ - _head_sum(os_, bd) * (1.0 / HEAD_C)
        var = _head_sum(oc * oc, bd) * (1.0 / HEAD_C)
        on = oc * lax.rsqrt(var + LNX_EPS) * lnx_w[:, sl] + lnx_b[:, sl]
        ys.append(on + _head_sum(rk[:, sl], bd) * v[:, sl])
    y = jnp.concatenate(ys, axis=1) * g_ref[...].astype(F32)
    h = jnp.dot(y.astype(BF16), wout_ref[...], preferred_element_type=F32)
    o_ref[...] = _layer_norm_rows(DN_ALPHA * x_ref[...] + h, vec_ref[3:4, :], vec_ref[4:5, :])


def odd_post(o_f, o_b, r, v, g, kd, x, vec, w_out, tm=256):
    n, D = x.shape
    row = pl.BlockSpec((tm, D), lambda i: (i, 0))
    row2 = pl.BlockSpec((2, tm, D), lambda i: (0, i, 0))
    return pl.pallas_call(
        _odd_post_kernel, grid=(n // tm,),
        in_specs=[row, row, row, row, row, row2, row, _full_spec(vec), _full_spec(w_out)],
        out_specs=row,
        out_shape=jax.ShapeDtypeStruct((n, D), F32),
        compiler_params=_cparams("parallel"), name="rwkv7_post",
    )(o_f, o_b, r, v, g, kd, x, vec, w_out)


def _proj_ln_kernel(a_ref, w_ref, x_ref, ln_ref, o_ref):
    h = jnp.dot(a_ref[...].astype(BF16), w_ref[...], preferred_element_type=F32)
    o_ref[...] = _layer_norm_rows(DN_ALPHA * x_ref[...] + h, ln_ref[0:1, :], ln_ref[1:2, :])


def proj_ln(a, w, x, ln, tm=512):
    n, K = a.shape
    D = w.shape[1]
    return pl.pallas_call(
        _proj_ln_kernel, grid=(n // tm,),
        in_specs=[pl.BlockSpec((tm, K), lambda i: (i, 0)), pl.BlockSpec((K, D), lambda i: (0, 0)),
                  pl.BlockSpec((tm, D), lambda i: (i, 0)), pl.BlockSpec((2, D), lambda i: (0, 0))],
        out_specs=pl.BlockSpec((tm, D), lambda i: (i, 0)),
        out_shape=jax.ShapeDtypeStruct((n, D), F32),
        compiler_params=_cparams("parallel"), name="proj_res_ln",
    )(a, w, x, ln)


def _add_ln_kernel(h_ref, x_ref, ln_ref, o_ref):
    o_ref[...] = _layer_norm_rows(DN_ALPHA * x_ref[...] + h_ref[...], ln_ref[0:1, :], ln_ref[1:2, :])


def add_ln(h, x, ln, tm=512):
    n, D = x.shape
    row = pl.BlockSpec((tm, D), lambda i: (i, 0))
    return pl.pallas_call(
        _add_ln_kernel, grid=(n // tm,),
        in_specs=[row, row, pl.BlockSpec((2, D), lambda i: (0, 0))],
        out_specs=row, out_shape=jax.ShapeDtypeStruct((n, D), F32),
        compiler_params=_cparams("parallel"), name="res_ln",
    )(h, x, ln)


def _even_post_kernel(oaf_ref, oab_ref, ob_ref, ga_ref, gb_ref, x_ref, vec_ref, wout_ref, o_ref):
    silu = lambda t: t * (1.0 / (1.0 + jnp.exp(-t)))
    oa = oaf_ref[...] + oab_ref[...]
    ob = ob_ref[...]
    ga, gb = ga_ref[...], gb_ref[...]
    na, nb = vec_ref[0:1, :], vec_ref[1:2, :]
    ys = []
    for h in range(H_A):
        sl = slice(h * HEAD_A, (h + 1) * HEAD_A)
        t = oa[:, sl]
        t = t * lax.rsqrt(jnp.mean(t * t, axis=-1, keepdims=True) + 1e-6)
        ys.append(t * na[:, sl] * silu(ga[:, sl]))
    for h in range(H_B):
        sl = slice(h * HEAD_B, (h + 1) * HEAD_B)
        t = ob[:, sl]
        t = t - jnp.mean(t, axis=-1, keepdims=True)
        t = t * lax.rsqrt(jnp.mean(t * t, axis=-1, keepdims=True) + 1e-6)
        ys.append(t * nb[:, sl] * silu(gb[:, sl]))
    y = jnp.concatenate(ys, axis=1)
    h_ = jnp.dot(y.astype(BF16), wout_ref[...], preferred_element_type=F32)
    o_ref[...] = _layer_norm_rows(DN_ALPHA * x_ref[...] + h_, vec_ref[2:3, :], vec_ref[3:4, :])


def even_post(oa_f, oa_b, o_b, proj, x, vec, w_out, tm=256):
    n, D = x.shape
    half = pl.BlockSpec((tm, D_A), lambda i: (i, 0))
    row = pl.BlockSpec((tm, D), lambda i: (i, 0))
    return pl.pallas_call(
        _even_post_kernel, grid=(n // tm,),
        in_specs=[half, half, half, pl.BlockSpec((tm, D_A), lambda i: (i, 4)),
                  pl.BlockSpec((tm, D_B), lambda i: (i, 8)), row,
                  pl.BlockSpec(vec.shape, lambda i: (0, 0)), pl.BlockSpec(w_out.shape, lambda i: (0, 0))],
        out_specs=row,
        out_shape=jax.ShapeDtypeStruct((n, D), F32),
        compiler_params=_cparams("parallel"), name="even_post",
    )(oa_f, oa_b, o_b, proj, proj, x, vec, w_out)


def _even_layer(xf, Bn, T, w_in, lb, norm_a, norm_b, w_out, ln):
    n, D = xf.shape
    proj = pmm(xf, w_in).reshape(Bn, T, -1)
    lbp = jnp.stack([jnp.log(lb), jnp.log1p(-lb), 1.0 - lb], axis=1)
    oa_f, oa_b = hgrn2_scan(proj, lbp)
    o_b = retention(proj)
    z = jnp.zeros((D - D_A,), F32)
    vec = jnp.concatenate([jnp.stack([jnp.concatenate([norm_a, z]), jnp.concatenate([norm_b, z])]), ln,
                           jnp.zeros((4, D), F32)], 0)
    fl = lambda t: t.reshape(n, -1)
    return even_post(fl(oa_f), fl(oa_b), fl(o_b), fl(proj), xf, vec, w_out)


def _odd_weights(p, j):
    w1, w2, a1, a2 = p['od_w1'][j], p['od_w2'][j], p['od_a1'][j], p['od_a2'][j]
    z = jnp.zeros_like(w2[0])
    w2bd = jnp.concatenate([jnp.concatenate([w2[0], z], 1), jnp.concatenate([z, w2[1]], 1)], 0)
    a2bd = jnp.concatenate([jnp.concatenate([a2[0], z], 1), jnp.concatenate([z, a2[1]], 1)], 0)
    w0, a0 = p['od_w0'][j], p['od_a0'][j]
    zero = jnp.zeros_like(w0[0])
    vec_pre = jnp.stack([w0[0], w0[1], a0[0], a0[1], p['od_k_k'][j], p['od_k_a'][j], zero, zero])
    vec_post = jnp.stack([p['od_lnx_w'][j], p['od_lnx_b'][j], p['od_r_k'][j]])
    pre = (p['od_mu'][j], p['od_w_rkv'][j], jnp.concatenate([w1[0], w1[1]], 1), w2bd,
           jnp.concatenate([a1[0], a1[1]], 1), a2bd, p['od_g1'][j], p['od_g2'][j], vec_pre)
    return pre, vec_post


def _odd_layer(xf, Bn, T, pre, vec_post, w_out, ln):
    D = xf.shape[-1]
    r, v, kk, g, kd, bb, lw = odd_pre(xf, T, *pre)
    sh = lambda t: t.reshape(Bn, T, D)
    sh2 = lambda t: t.reshape(2, Bn, T, D)
    o_f, o_b = rwkv7_scan(sh(r), sh(v), sh(kk), sh2(lw), sh2(kd), sh2(bb))
    vec = jnp.concatenate([vec_post, ln, jnp.zeros((3, D), F32)], 0)
    return odd_post(o_f.reshape(-1, D), o_b.reshape(-1, D), r, v, g, kd, xf, vec, w_out)


def _cross_attn_layer(xf, Bn, T, mem, w_q, w_kv, w_out, ln):
    return cross_attn_layer(xf, T, _mm(mem, w_kv), w_q, w_out, ln)


def _moe_layer(xt, w_router, w_in, w_out, layer, ln):
    n, D = xt.shape
    cap = (CAP_FACTOR * n) // N_EXPERTS
    aff = jax.nn.softmax(pmm(xt, w_router), axis=-1)
    gate, idx = lax.top_k(aff.T, cap)
    xe = xt[idx]
    ye = moe_ffn(xe, w_in, w_out, gate[..., None], layer)
    y = jnp.zeros_like(xt).at[idx.reshape(-1)].add(ye.reshape(-1, D))
    return add_ln(y, xt, ln)


def _trunk(x, mem, p):
    Bn, T, D = x.shape
    xf = x.reshape(Bn * T, D)
    p_lb = jax.nn.softmax(p['ev_lb_logits'], axis=1)
    lb_all = jnp.clip(jnp.cumsum(p_lb, axis=1) - p_lb[:, :1], 0.0, 1.0)
    for layer in range(DEPTH):
        j = layer // 2
        ln = lambda s: jnp.stack([p['ln_w'][layer, s], p['ln_b'][layer, s]])
        if layer % 2 == 0:
            xf = _even_layer(xf, Bn, T, p['ev_w_in'][j], lb_all[:, j], p['ev_norm_a'][j], p['ev_norm_b'][j],
                             p['ev_w_out'][j], ln(0))
        else:
            pre, vec_post = _odd_weights(p, j)
            xf = _odd_layer(xf, Bn, T, pre, vec_post, p['od_w_out'][j], ln(0))
        xf = _cross_attn_layer(xf, Bn, T, mem, p['ca_w_q'][layer], p['ca_w_kv'][layer], p['ca_w_out'][layer], ln(1))
        xf = _moe_layer(xf, p['moe_router'][layer], p['moe_w_in'], p['moe_w_out'], layer, ln(2))
    return xf.reshape(Bn, T, D)


_MATMUL_WEIGHTS = ('ev_w_in', 'ev_w_out', 'od_w_rkv', 'od_w1', 'od_w2', 'od_a1', 'od_a2', 'od_g1', 'od_g2',
                   'od_w_out', 'ca_w_q', 'ca_w_kv', 'ca_w_out', 'moe_router', 'moe_w_in', 'moe_w_out')


def kernel(x_prompt, x_sample, mem_prompt, mem_sample, ev_w_in, ev_lb_logits, ev_norm_a, ev_norm_b, ev_w_out, od_mu, od_w_rkv, od_w0, od_w1, od_w2, od_a0, od_a1, od_a2, od_g1, od_g2, od_k_k, od_k_a, od_r_k, od_lnx_w, od_lnx_b, od_w_out, ca_w_q, ca_w_kv, ca_w_out, moe_router, moe_w_in, moe_w_out, ln_w, ln_b):
    params = dict(ev_w_in=ev_w_in, ev_lb_logits=ev_lb_logits, ev_norm_a=ev_norm_a, ev_norm_b=ev_norm_b,
                  ev_w_out=ev_w_out, od_mu=od_mu, od_w_rkv=od_w_rkv, od_w0=od_w0, od_w1=od_w1, od_w2=od_w2,
                  od_a0=od_a0, od_a1=od_a1, od_a2=od_a2, od_g1=od_g1, od_g2=od_g2, od_k_k=od_k_k,
                  od_k_a=od_k_a, od_r_k=od_r_k, od_lnx_w=od_lnx_w, od_lnx_b=od_lnx_b, od_w_out=od_w_out,
                  ca_w_q=ca_w_q, ca_w_kv=ca_w_kv, ca_w_out=ca_w_out, moe_router=moe_router,
                  moe_w_in=moe_w_in, moe_w_out=moe_w_out, ln_w=ln_w, ln_b=ln_b)
    for name in _MATMUL_WEIGHTS:
        params[name] = params[name].astype(BF16)
    y_prompt = _trunk(x_prompt, mem_prompt, params)
    y_sample = _trunk(x_sample, mem_sample, params)
    return (y_prompt, y_sample)
```

```python
import functools
import math

import jax
import jax.numpy as jnp
from jax import lax
from jax.experimental import pallas as pl
from jax.experimental.pallas import tpu as pltpu

F32 = jnp.float32
BF16 = jnp.bfloat16

D_MODEL = 1024
DEPTH = 4
D_A = D_MODEL // 2
HEAD_A = 128
H_A = D_A // HEAD_A
CHUNK_A = 32
D_B = D_MODEL // 2
HEAD_B = 128
H_B = D_B // HEAD_B
CHUNK_B = 128
ROPE_BASE = 10000.0
HEAD_C = 64
H_C = D_MODEL // HEAD_C
LNX_EPS = 64e-5
H_CA = 4
HEAD_CA = D_MODEL // H_CA
N_EXPERTS = 16
D_EXPERT = 2048
CAP_FACTOR = 2
DN_ALPHA = (2.0 * DEPTH) ** 0.25
LN_EPS = 1e-5

VMEM_LIMIT = 48 * 1024 * 1024


def _cparams(*sem):
    return pltpu.CompilerParams(dimension_semantics=sem, vmem_limit_bytes=VMEM_LIMIT)


def _mm_kernel(x_ref, w_ref, o_ref):
    o_ref[...] = jnp.dot(x_ref[...].astype(BF16), w_ref[...],
                         preferred_element_type=F32).astype(o_ref.dtype)


def _pick(n, pref):
    t = min(n, pref)
    while n % t:
        t //= 2
    return t


def pmm(x, w, tm=1024, tn=512):
    M, K = x.shape
    N = w.shape[1]
    tm = _pick(M, tm)
    tn = _pick(N, tn) if N % 128 == 0 else N
    return pl.pallas_call(
        _mm_kernel,
        grid=(M // tm, N // tn),
        in_specs=[pl.BlockSpec((tm, K), lambda i, j: (i, 0)),
                  pl.BlockSpec((K, tn), lambda i, j: (0, j))],
        out_specs=pl.BlockSpec((tm, tn), lambda i, j: (i, j)),
        out_shape=jax.ShapeDtypeStruct((M, N), F32),
        compiler_params=_cparams("parallel", "parallel"),
        name="dense_proj",
    )(x, w)


def _mm(x, w):
    lead = x.shape[:-1]
    return pmm(x.reshape(-1, x.shape[-1]), w).reshape(lead + (w.shape[1],))


def _ca_kernel(x_ref, wq_ref, k_ref, v_ref, wo_ref, ln_ref, o_ref):
    scale = HEAD_CA ** -0.5
    x = x_ref[...]
    q_all = jnp.dot(x.astype(BF16), wq_ref[...], preferred_element_type=F32).astype(BF16)
    outs = []
    for h in range(H_CA):
        sl = slice(h * HEAD_CA, (h + 1) * HEAD_CA)
        k = k_ref[0, :, sl].astype(BF16)
        v = v_ref[0, :, sl].astype(BF16)
        s = lax.dot_general(q_all[:, sl], k, (((1,), (1,)), ((), ())), preferred_element_type=F32) * scale
        m = jnp.max(s, axis=-1, keepdims=True)
        e = jnp.exp(s - m)
        p = e / jnp.sum(e, axis=-1, keepdims=True)
        outs.append(jnp.dot(p.astype(BF16), v, preferred_element_type=F32).astype(BF16))
    h_ = jnp.dot(jnp.concatenate(outs, axis=1), wo_ref[...], preferred_element_type=F32)
    y = DN_ALPHA * x + h_
    mu = jnp.mean(y, axis=-1, keepdims=True)
    yc = y - mu
    var = jnp.mean(yc * yc, axis=-1, keepdims=True)
    o_ref[...] = yc * lax.rsqrt(var + LN_EPS) * ln_ref[0:1, :] + ln_ref[1:2, :]


def cross_attn_layer(xf, T, kv, w_q, w_out, ln, tq=512):
    n, D = xf.shape
    M = kv.shape[1]
    tq = _pick(T, tq)
    per = T // tq
    return pl.pallas_call(
        _ca_kernel,
        grid=(n // tq,),
        in_specs=[pl.BlockSpec((tq, D), lambda i: (i, 0)),
                  pl.BlockSpec((D, D), lambda i: (0, 0)),
                  pl.BlockSpec((1, M, D), lambda i: (i // per, 0, 0)),
                  pl.BlockSpec((1, M, D), lambda i: (i // per, 0, 1)),
                  pl.BlockSpec((D, D), lambda i: (0, 0)),
                  pl.BlockSpec((2, D), lambda i: (0, 0))],
        out_specs=pl.BlockSpec((tq, D), lambda i: (i, 0)),
        out_shape=jax.ShapeDtypeStruct((n, D), F32),
        compiler_params=_cparams("parallel"),
        name="cross_attn",
    )(xf, w_q, kv, kv, w_out, ln)


MOE_CHUNK = 2048


def _moe_kernel(x_ref, wg_ref, wu_ref, wo_ref, gate_ref, o_ref, *, nf, tm):
    f = pl.program_id(2)
    rows = pl.ds(pl.multiple_of(pl.program_id(3) * tm, tm), tm)
    x = x_ref[0]
    hg = jnp.dot(x, wg_ref[0].astype(BF16), preferred_element_type=F32)
    hu = jnp.dot(x, wu_ref[0].astype(BF16), preferred_element_type=F32)
    a = (hg * jax.nn.sigmoid(hg) * hu).astype(BF16)
    part = jnp.dot(a, wo_ref[0].astype(BF16), preferred_element_type=F32)

    @pl.when(f == 0)
    def _():
        o_ref[0, rows, :] = part

    @pl.when(jnp.logical_and(f > 0, f < nf - 1))
    def _():
        o_ref[0, rows, :] += part

    @pl.when(f == nf - 1)
    def _():
        o_ref[0, rows, :] = (o_ref[0, rows, :] + part) * gate_ref[0]


def moe_ffn(xe, w_in, w_out, gate, layer, tm=1024, tf=512):
    E, cap, D = xe.shape
    Fh = w_out.shape[2]
    ch = min(cap, MOE_CHUNK)
    tm = _pick(ch, tm)
    nf = Fh // tf
    nm = ch // tm
    return pl.pallas_call(
        functools.partial(_moe_kernel, nf=nf, tm=tm),
        grid=(E, cap // ch, nf, nm),
        in_specs=[pl.BlockSpec((1, tm, D), lambda e, h, f, m: (e, h * nm + m, 0)),
                  pl.BlockSpec((None, 1, D, tf), lambda e, h, f, m: (layer, e, 0, f)),
                  pl.BlockSpec((None, 1, D, tf), lambda e, h, f, m: (layer, e, 0, nf + f)),
                  pl.BlockSpec((None, 1, tf, D), lambda e, h, f, m: (layer, e, f, 0)),
                  pl.BlockSpec((1, tm, 1), lambda e, h, f, m: (e, h * nm + m, 0))],
        out_specs=pl.BlockSpec((1, ch, D), lambda e, h, f, m: (e, h, 0)),
        out_shape=jax.ShapeDtypeStruct((E, cap, D), F32),
        compiler_params=_cparams("parallel", "parallel", "arbitrary", "arbitrary"),
        name="moe_ffn",
    )(xe, w_in, w_in, w_out, gate)


_NN = (((1,), (0,)), ((), ()))
_NT = (((1,), (1,)), ((), ()))
_TN = (((0,), (0,)), ((), ()))


def _bdot(a, b, dims=_NN):
    return lax.dot_general(a.astype(BF16), b.astype(BF16), dims, preferred_element_type=F32)


def _split3(x):
    hi = x.astype(BF16)
    r1 = x - hi.astype(F32)
    mid = r1.astype(BF16)
    lo = (r1 - mid.astype(F32)).astype(BF16)
    return hi, mid, lo


def _cumdot(cum, x):
    hi, mid, lo = _split3(x)
    return (lax.dot_general(cum, hi, _NN, preferred_element_type=F32)
            + lax.dot_general(cum, mid, _NN, preferred_element_type=F32)
            + lax.dot_general(cum, lo, _NN, preferred_element_type=F32))


RW_L = 64
RW_TT = 512
RW_W = 128


def _rwkv_kernel(rf, vf, kkf, lwf, kdf, bbf, rb, vb, kkb, lwb, kdb, bbb, of_ref, ob_ref,
                 mf_ref, mb_ref, om_ref, th_ref):
    L = RW_L
    nc = RW_TT // L
    S = 2 * L

    @pl.when(pl.program_id(2) == 0)
    def _():
        mf_ref[...] = jnp.zeros_like(mf_ref)
        mb_ref[...] = jnp.zeros_like(mb_ref)

    row = lax.broadcasted_iota(jnp.int32, (S, S), 0)
    col = lax.broadcasted_iota(jnp.int32, (S, S), 1)
    strict = {False: col < row, True: col > row}
    incl = {False: col <= row, True: col >= row}
    incl2 = {d: jnp.concatenate([incl[d], incl[d]], axis=1) for d in (False, True)}
    eye = (row == col).astype(F32)
    cr = lax.broadcasted_iota(jnp.int32, (L, L), 0)
    cc = lax.broadcasted_iota(jnp.int32, (L, L), 1)
    cum = {False: (cc <= cr).astype(BF16), True: (cc >= cr).astype(BF16)}
    lane = lax.broadcasted_iota(jnp.int32, (L, RW_W), 1)
    m0 = lane < HEAD_C

    def stack(x):
        return jnp.concatenate([jnp.where(m0, x, 0.0), jnp.where(m0, 0.0, x)], axis=0)

    refs = {False: (rf, vf, kkf, lwf, kdf, bbf), True: (rb, vb, kkb, lwb, kdb, bbb)}
    items = [(d, c) for c in range(nc) for d in (False, True)]
    ni = len(items)
    cs = range(ni)
    ld = lambda i, which: refs[items[i][0]][which][0, pl.ds(items[i][1] * L, L), :]
    lws = [ld(i, 3) for i in cs]
    gs = [_cumdot(cum[items[i][0]], lws[i]) for i in cs]
    gls = [(gs[i][0:1, :] if items[i][0] else gs[i][L - 1:L, :]) for i in cs]
    lhs, rhs_a, kbl, vss, rgs, ags = [], [], [], [], [], []
    for i in cs:
        g, gl = gs[i], gls[i]
        eg, eng, egl = jnp.exp(g), jnp.exp(-g), jnp.exp(gl - g)
        kd, bb = ld(i, 4), ld(i, 5)
        rg_s, ag_s = stack(ld(i, 0) * eg), stack(ld(i, 2) * jnp.exp(g - lws[i]))
        rgs.append(rg_s)
        ags.append(ag_s)
        lhs.append(jnp.concatenate([ag_s, rg_s], axis=0))
        rhs_a.append(jnp.concatenate([stack(kd * eng), stack(bb * eng)], axis=0))
        kbl.append(jnp.concatenate([stack(kd * egl), stack(bb * egl)], axis=0))
        vss.append(stack(ld(i, 1)))
    amats = [_bdot(lhs[i], rhs_a[i], _NT) for i in cs]
    aaks = [jnp.where(strict[items[i][0]], amats[i][0:S, 0:S], 0.0) for i in cs]
    ns = [jnp.where(strict[items[i][0]], amats[i][0:S, S:2 * S], 0.0) for i in cs]
    arkb = [jnp.where(incl2[items[i][0]], amats[i][S:2 * S, :], 0.0) for i in cs]
    ts = [eye + n for n in ns]
    ps = [_bdot(n, n) for n in ns]
    avs = [_bdot(aaks[i], vss[i]) for i in cs]
    for _ in range(4):
        pps = [_bdot(ps[i], jnp.concatenate([ps[i], ts[i]], axis=1)) for i in cs]
        ts = [ts[i] + pps[i][:, S:2 * S] for i in cs]
        ps = [pps[i][:, 0:S] for i in cs]
    ts = [ts[i] + _bdot(ps[i], ts[i]) for i in cs]
    w12s = [_bdot(ts[i], jnp.concatenate([ags[i], avs[i]], axis=1)) for i in cs]
    rhss = [jnp.concatenate([jnp.concatenate([jnp.zeros_like(vss[i]), vss[i]], axis=1), w12s[i]], axis=0) for i in cs]
    sts = [_bdot(kbl[i], rhss[i], _TN) for i in cs]
    ots = [_bdot(arkb[i], rhss[i]) for i in cs]
    for i in cs:
        phi = sts[i][:, 0:RW_W] + eye * jnp.exp(gls[i])
        om_ref[i] = jnp.concatenate([ots[i][:, 0:RW_W] + rgs[i], phi], axis=0)
        th_ref[i] = jnp.concatenate([ots[i][:, RW_W:2 * RW_W], sts[i][:, RW_W:2 * RW_W]], axis=0)

    m = {False: mf_ref[...], True: mb_ref[...]}
    outs = {False: of_ref, True: ob_ref}
    for j in range(nc):
        for d in (False, True):
            c = nc - 1 - j if d else j
            i = items.index((d, c))
            res = _bdot(om_ref[i], m[d]) + th_ref[i]
            outs[d][0, pl.ds(c * L, L), :] = res[0:L] + res[L:S]
            m[d] = res[S:2 * S]
    mf_ref[...] = m[False]
    mb_ref[...] = m[True]


def rwkv7_scan(r, v, kk, lw, kd, bb):
    B, T, D = r.shape
    nt = T // RW_TT
    f3 = pl.BlockSpec((1, RW_TT, RW_W), lambda b, h, i: (b, i, h))
    r3 = pl.BlockSpec((1, RW_TT, RW_W), lambda b, h, i: (b, nt - 1 - i, h))
    f4 = pl.BlockSpec((None, 1, RW_TT, RW_W), lambda b, h, i: (0, b, i, h))
    r4 = pl.BlockSpec((None, 1, RW_TT, RW_W), lambda b, h, i: (1, b, nt - 1 - i, h))
    nc = RW_TT // RW_L
    out = jax.ShapeDtypeStruct((B, T, D), F32)
    return pl.pallas_call(
        _rwkv_kernel,
        grid=(B, D // RW_W, nt),
        in_specs=[f3, f3, f3, f4, f4, f4, r3, r3, r3, r4, r4, r4],
        out_specs=[f3, r3],
        out_shape=[out, out],
        scratch_shapes=[pltpu.VMEM((RW_W, RW_W), F32), pltpu.VMEM((RW_W, RW_W), F32),
                        pltpu.VMEM((2 * nc, 4 * RW_L, RW_W), F32),
                        pltpu.VMEM((2 * nc, 4 * RW_L, RW_W), F32)],
        compiler_params=_cparams("parallel", "parallel", "arbitrary"),
        name="rwkv7",
    )(r, v, kk, lw, kd, bb, r, v, kk, lw, kd, bb)


HG_SUB = 16
HG_GRP = 128
HG_TT = 512


def _hgrn2_group(q_ref, v_ref, z_ref, lb_ref, o_ref, st_ref, g0, reverse):
    C, G = HG_SUB, HG_GRP
    log_lb, log_1mlb, one_mlb = lb_ref[0, 0:1, :], lb_ref[0, 1:2, :], lb_ref[0, 2:3, :]
    row = lax.broadcasted_iota(jnp.int32, (G, G), 0)
    col = lax.broadcasted_iota(jnp.int32, (G, G), 1)
    same = (row // C) == (col // C)
    cum = (same & ((col >= row) if reverse else (col <= row))).astype(BF16)
    sub_h = lax.broadcasted_iota(jnp.int32, (C // 2, HEAD_A), 0)
    rows = pl.ds(pl.multiple_of(g0 * G, G), G)
    z = z_ref[0, rows, :]
    a_q = q_ref[0, rows, :]
    vv = v_ref[0, rows, :]
    ls = jnp.minimum(z, 0.0) - jnp.log1p(jnp.exp(-jnp.abs(z)))
    t2 = log_1mlb + ls
    mx = jnp.maximum(log_lb, t2)
    lf = mx + jnp.log1p(jnp.exp(-jnp.abs(log_lb - t2)))
    kk = one_mlb * (1.0 / (1.0 + jnp.exp(z)))
    qq = a_q * (1.0 / (1.0 + jnp.exp(-a_q)))
    b = _cumdot(cum, lf)
    st = st_ref[...]
    order = range(G // C - 1, -1, -1) if reverse else range(G // C)
    outs = [None] * (G // C)
    for j in order:
        sl = slice(j * C, (j + 1) * C)
        bj, qj, kj, vj = b[sl], qq[sl], kk[sl], vv[sl]
        e_in = 0 if reverse else C - 1
        b_end = bj[e_in:e_in + 1, :]
        H = C // 2
        acc = [None, None]
        for s in range(C):
            bs, ks, vs = bj[s:s + 1, :], kj[s:s + 1, :], vj[s:s + 1, :]
            hs = s // H
            full = (hs - 1) if reverse else (hs + 1)
            for h in (0, 1):
                if h != hs and h != full:
                    continue
                rows_h = slice(h * H, (h + 1) * H)
                diff = bj[rows_h] - bs
                if h == hs:
                    keep = (sub_h <= s - hs * H) if reverse else (sub_h >= s - hs * H)
                    diff = jnp.where(keep, diff, -1e30)
                sc = jnp.sum(jnp.exp(diff) * (qj[rows_h] * ks), axis=-1, keepdims=True)
                acc[h] = sc * vs if acc[h] is None else acc[h] + sc * vs
        outs[j] = jnp.concatenate(acc, axis=0) + _bdot(qj * jnp.exp(bj), st, _NT)
        st = st * jnp.exp(b_end) + _bdot(vj, kj * jnp.exp(b_end - bj), _TN)
    st_ref[...] = st
    o_ref[0, rows, :] = jnp.concatenate(outs, axis=0)


def _hgrn2_kernel(qf_ref, vf_ref, zf_ref, qb_ref, vb_ref, zb_ref, lbf_ref, lbb_ref, of_ref, ob_ref, stf_ref, stb_ref):
    @pl.when(pl.program_id(2) == 0)
    def _():
        stf_ref[...] = jnp.zeros_like(stf_ref)
        stb_ref[...] = jnp.zeros_like(stb_ref)

    ng = HG_TT // HG_GRP

    def body(gi, carry):
        _hgrn2_group(qf_ref, vf_ref, zf_ref, lbf_ref, of_ref, stf_ref, gi, False)
        _hgrn2_group(qb_ref, vb_ref, zb_ref, lbb_ref, ob_ref, stb_ref, ng - 1 - gi, True)
        return carry

    lax.fori_loop(0, ng, body, 0)


def hgrn2_scan(proj, lbp):
    B, T, _ = proj.shape
    nt = T // HG_TT
    blk = (1, HG_TT, HEAD_A)
    fmap = lambda off: (lambda b, h, i: (b, i, off + h))
    rmap = lambda off: (lambda b, h, i: (b, nt - 1 - i, off + h))
    out = jax.ShapeDtypeStruct((B, T, D_A), F32)
    return pl.pallas_call(
        _hgrn2_kernel,
        grid=(B, H_A, nt),
        in_specs=[pl.BlockSpec(blk, fmap(0)), pl.BlockSpec(blk, fmap(H_A)), pl.BlockSpec(blk, fmap(2 * H_A)),
                  pl.BlockSpec(blk, rmap(0)), pl.BlockSpec(blk, rmap(H_A)), pl.BlockSpec(blk, rmap(3 * H_A)),
                  pl.BlockSpec((1, 3, HEAD_A), lambda b, h, i: (0, 0, h)),
                  pl.BlockSpec((1, 3, HEAD_A), lambda b, h, i: (1, 0, h))],
        out_specs=[pl.BlockSpec(blk, fmap(0)), pl.BlockSpec(blk, rmap(0))],
        out_shape=[out, out],
        scratch_shapes=[pltpu.VMEM((HEAD_A, HEAD_A), F32), pltpu.VMEM((HEAD_A, HEAD_A), F32)],
        compiler_params=_cparams("parallel", "parallel", "arbitrary"),
        name="hgrn2",
    )(proj, proj, proj, proj, proj, proj, lbp, lbp)


RT_TT = 256


def rope_tables(T):
    theta = 1.0 / jnp.power(ROPE_BASE, jnp.linspace(0.0, 1.0, HEAD_B // 2, dtype=F32))
    ang = jnp.arange(T, dtype=F32)[:, None] * theta[None, :]
    cos = jnp.repeat(jnp.cos(ang), 2, axis=1)
    sin = jnp.repeat(jnp.sin(ang), 2, axis=1)
    sign = jnp.where(jnp.arange(HEAD_B) % 2 == 0, -1.0, 1.0).astype(F32)
    return cos, sin * sign[None, :]


def _rope(t, cos, sin):
    lane = lax.broadcasted_iota(jnp.int32, t.shape, 1)
    nxt = pltpu.roll(t, t.shape[1] - 1, axis=1)
    prv = pltpu.roll(t, 1, axis=1)
    return t * cos + jnp.where(lane % 2 == 0, nxt, prv) * sin


def _ret_fwd_kernel(lg_ref, cos_ref, sin_ref, q_ref, k_ref, v_ref, o_ref, st_ref):
    Tt = RT_TT

    @pl.when(pl.program_id(2) == 0)
    def _():
        st_ref[...] = jnp.zeros_like(st_ref)

    lgw = lg_ref[0]
    lg = lgw[:, 0:HEAD_B]
    cos, sin = cos_ref[...], sin_ref[...]
    q = _rope(q_ref[0], cos, sin)
    k = _rope(k_ref[0], cos, sin) * (HEAD_B ** -0.5)
    v = v_ref[0]
    row = lax.broadcasted_iota(jnp.int32, (Tt, Tt), 0)
    col = lax.broadcasted_iota(jnp.int32, (Tt, Tt), 1)
    dmat = jnp.exp(lgw * jnp.abs(row - col).astype(F32))
    pos = lax.broadcasted_iota(jnp.int32, (Tt, 1), 0).astype(F32)
    st = st_ref[...]
    o = _bdot(_bdot(q, k, _NT) * dmat, v)
    o_ref[0] = o + _bdot(q * jnp.exp(lg * (pos + 1.0)), st)
    st_ref[...] = st * jnp.exp(lg * Tt) + _bdot(k * jnp.exp(lg * (Tt - 1.0 - pos)), v, _TN)


def _ret_bwd_kernel(lg_ref, cos_ref, sin_ref, q_ref, k_ref, v_ref, oin_ref, o_ref, st_ref):
    Tt = RT_TT

    @pl.when(pl.program_id(2) == 0)
    def _():
        st_ref[...] = jnp.zeros_like(st_ref)

    lg = lg_ref[0][:, 0:HEAD_B]
    cos, sin = cos_ref[...], sin_ref[...]
    q = _rope(q_ref[0], cos, sin)
    k = _rope(k_ref[0], cos, sin) * (HEAD_B ** -0.5)
    pos = lax.broadcasted_iota(jnp.int32, (Tt, 1), 0).astype(F32)
    st = st_ref[...]
    o_ref[0] = oin_ref[0] + _bdot(q * jnp.exp(lg * (Tt - pos)), st)
    st_ref[...] = st * jnp.exp(lg * Tt) + _bdot(k * jnp.exp(lg * pos), v_ref[0], _TN)


def retention(proj):
    B, T, _ = proj.shape
    nt = T // RT_TT
    blk = (1, RT_TT, HEAD_B)
    qoff = 5 * D_A // HEAD_B
    koff, voff = qoff + H_B, qoff + 2 * H_B
    cos, sin = rope_tables(T)
    lgam = jnp.log1p(-jnp.power(2.0, -5.0 - jnp.arange(H_B, dtype=F32)))
    lgam = jnp.broadcast_to(lgam[:, None, None], (H_B, 1, RT_TT))
    lspec = pl.BlockSpec((1, 1, RT_TT), lambda b, h, i: (h, 0, 0))

    def specs(tix):
        tab = pl.BlockSpec((RT_TT, HEAD_B), lambda b, h, i: (tix(i), 0))
        return [lspec, tab, tab] + [pl.BlockSpec(blk, (lambda off: lambda b, h, i: (b, tix(i), off + h))(off))
                                    for off in (qoff, koff, voff)]
    fwd = lambda i: i
    rev = lambda i: nt - 1 - i
    o1 = pl.pallas_call(
        _ret_fwd_kernel, grid=(B, H_B, nt),
        in_specs=specs(fwd),
        out_specs=pl.BlockSpec(blk, lambda b, h, i: (b, i, h)),
        out_shape=jax.ShapeDtypeStruct((B, T, D_B), F32),
        scratch_shapes=[pltpu.VMEM((HEAD_B, HEAD_B), F32)],
        compiler_params=_cparams("parallel", "parallel", "arbitrary"), name="retention_fwd",
    )(lgam, cos, sin, proj, proj, proj)
    return pl.pallas_call(
        _ret_bwd_kernel, grid=(B, H_B, nt),
        in_specs=specs(rev) + [pl.BlockSpec(blk, lambda b, h, i: (b, nt - 1 - i, h))],
        out_specs=pl.BlockSpec(blk, lambda b, h, i: (b, nt - 1 - i, h)),
        out_shape=jax.ShapeDtypeStruct((B, T, D_B), F32),
        scratch_shapes=[pltpu.VMEM((HEAD_B, HEAD_B), F32)],
        input_output_aliases={6: 0},
        compiler_params=_cparams("parallel", "parallel", "arbitrary"), name="retention_bwd",
    )(lgam, cos, sin, proj, proj, proj, o1)


LANES = 128


def _head_ones():
    r = lax.broadcasted_iota(jnp.int32, (LANES, LANES), 0) // HEAD_C
    c = lax.broadcasted_iota(jnp.int32, (LANES, LANES), 1) // HEAD_C
    return (r == c).astype(BF16)


def _head_sum(x, bd):
    hi = x.astype(BF16)
    lo = (x - hi.astype(F32)).astype(BF16)
    return (lax.dot_general(hi, bd, _NN, preferred_element_type=F32)
            + lax.dot_general(lo, bd, _NN, preferred_element_type=F32))


def _layer_norm_rows(y, w, b):
    mu = jnp.mean(y, axis=-1, keepdims=True)
    yc = y - mu
    var = jnp.mean(yc * yc, axis=-1, keepdims=True)
    return yc * lax.rsqrt(var + LN_EPS) * w + b


def _full_spec(a):
    return pl.BlockSpec(a.shape, lambda i: (0,) * a.ndim)


def _odd_pre_kernel(x_ref, xp_ref, xn_ref, mu_ref, wrkv_ref, w1_ref, w2_ref, a1_ref, a2_ref, g1_ref, g2_ref,
                    vec_ref, r_ref, v_ref, kk_ref, g_ref, kd_ref, bb_ref, lw_ref, *, tiles_per_seq):
    D = D_MODEL
    x = x_ref[...]
    tm = x.shape[0]
    t_in_seq = pl.program_id(0) % tiles_per_seq
    prev_row = jnp.where(t_in_seq == 0, 0.0, xp_ref[0, 7:8, :])
    next_row = jnp.where(t_in_seq == tiles_per_seq - 1, 0.0, xn_ref[0, 0:1, :])
    rid = lax.broadcasted_iota(jnp.int32, x.shape, 0)
    x_prev = jnp.where(rid == 0, prev_row, pltpu.roll(x, 1, axis=0))
    x_next = jnp.where(rid == tm - 1, next_row, pltpu.roll(x, tm - 1, axis=0))
    xx = 0.5 * (x_prev + x_next) - x
    mix = lambda p: (x + xx * mu_ref[p:p + 1, :]).astype(BF16)
    r = jnp.dot(mix(0), wrkv_ref[0], preferred_element_type=F32)
    k = jnp.dot(mix(1), wrkv_ref[1], preferred_element_type=F32)
    v = jnp.dot(mix(2), wrkv_ref[2], preferred_element_type=F32)
    hw = jnp.tanh(jnp.dot(mix(3), w1_ref[...], preferred_element_type=F32))
    lora_w = _bdot(hw, w2_ref[...])
    lora_a = _bdot(jnp.dot(mix(4), a1_ref[...], preferred_element_type=F32), a2_ref[...])
    hg = jnp.dot(mix(5), g1_ref[...], preferred_element_type=F32)
    g = _bdot(1.0 / (1.0 + jnp.exp(-hg)), g2_ref[...])
    k_k, k_a = vec_ref[4:5, :], vec_ref[5:6, :]
    bd = _head_ones()
    kq = k * k_k
    kks = []
    for s in range(D // LANES):
        ks = kq[:, s * LANES:(s + 1) * LANES]
        nrm = jnp.sqrt(_head_sum(ks * ks, bd))
        kks.append(ks / jnp.maximum(nrm, 1e-12))
    kk = jnp.concatenate(kks, axis=1)
    r_ref[...] = r.astype(r_ref.dtype)
    v_ref[...] = v.astype(v_ref.dtype)
    kk_ref[...] = kk.astype(kk_ref.dtype)
    g_ref[...] = g.astype(g_ref.dtype)
    for n in range(2):
        u = vec_ref[n:n + 1, :] + lora_w[:, n * D:(n + 1) * D]
        lw_ref[n] = -math.exp(-0.5) / (1.0 + jnp.exp(-u))
        a = 1.0 / (1.0 + jnp.exp(-(vec_ref[2 + n:3 + n, :] + lora_a[:, n * D:(n + 1) * D])))
        kd_ref[n] = (k * (1.0 + (a - 1.0) * k_a)).astype(kd_ref.dtype)
        bb_ref[n] = (-(kk * a)).astype(bb_ref.dtype)


def odd_pre(x, T, mu, wrkv, w1c, w2bd, a1c, a2bd, g1, g2, vec, tm=256):
    n, D = x.shape
    row = pl.BlockSpec((tm, D), lambda i: (i, 0))
    x8 = x.reshape(n // 8, 8, D)
    prev8 = pl.BlockSpec((1, 8, D), lambda i: (jnp.maximum(i * (tm // 8) - 1, 0), 0, 0))
    next8 = pl.BlockSpec((1, 8, D), lambda i: (jnp.minimum((i + 1) * (tm // 8), n // 8 - 1), 0, 0))
    row2 = pl.BlockSpec((2, tm, D), lambda i: (0, i, 0))
    o1 = jax.ShapeDtypeStruct((n, D), BF16)
    o2 = jax.ShapeDtypeStruct((2, n, D), BF16)
    return pl.pallas_call(
        functools.partial(_odd_pre_kernel, tiles_per_seq=T // tm), grid=(n // tm,),
        in_specs=[row, prev8, next8] + [_full_spec(a) for a in (mu, wrkv, w1c, w2bd, a1c, a2bd, g1, g2, vec)],
        out_specs=[row, row, row, row, row2, row2, row2],
        out_shape=[o1, o1, o1, o1, o2, o2, jax.ShapeDtypeStruct((2, n, D), F32)],
        compiler_params=_cparams("parallel"), name="rwkv7_pre",
    )(x, x8, x8, mu, wrkv, w1c, w2bd, a1c, a2bd, g1, g2, vec)


def _odd_post_kernel(of_ref, ob_ref, r_ref, v_ref, g_ref, kd_ref, x_ref, vec_ref, wout_ref, o_ref):
    D = D_MODEL
    bd = _head_ones()
    lnx_w, lnx_b, r_k = vec_ref[0:1, :], vec_ref[1:2, :], vec_ref[2:3, :]
    o = of_ref[...] + ob_ref[...]
    rk = r_ref[...].astype(F32) * r_k * (kd_ref[0].astype(F32) + kd_ref[1].astype(F32))
    v = v_ref[...].astype(F32)
    ys = []
    for s in range(D // LANES):
        sl = slice(s * LANES, (s + 1) * LANES)
        os_ = o[:, sl]
        oc = os_ - _head_sum(os_, bd) * (1.0 / HEAD_C)
        var = _head_sum(oc * oc, bd) * (1.0 / HEAD_C)
        on = oc * lax.rsqrt(var + LNX_EPS) * lnx_w[:, sl] + lnx_b[:, sl]
        ys.append(on + _head_sum(rk[:, sl], bd) * v[:, sl])
    y = jnp.concatenate(ys, axis=1) * g_ref[...].astype(F32)
    h = jnp.dot(y.astype(BF16), wout_ref[...], preferred_element_type=F32)
    o_ref[...] = _layer_norm_rows(DN_ALPHA * x_ref[...] + h, vec_ref[3:4, :], vec_ref[4:5, :])


def odd_post(o_f, o_b, r, v, g, kd, x, vec, w_out, tm=256):
    n, D = x.shape
    row = pl.BlockSpec((tm, D), lambda i: (i, 0))
    row2 = pl.BlockSpec((2, tm, D), lambda i: (0, i, 0))
    return pl.pallas_call(
        _odd_post_kernel, grid=(n // tm,),
        in_specs=[row, row, row, row, row, row2, row, _full_spec(vec), _full_spec(w_out)],
        out_specs=row,
        out_shape=jax.ShapeDtypeStruct((n, D), F32),
        compiler_params=_cparams("parallel"), name="rwkv7_post",
    )(o_f, o_b, r, v, g, kd, x, vec, w_out)


def _proj_ln_kernel(a_ref, w_ref, x_ref, ln_ref, o_ref):
    h = jnp.dot(a_ref[...].astype(BF16), w_ref[...], preferred_element_type=F32)
    o_ref[...] = _layer_norm_rows(DN_ALPHA * x_ref[...] + h, ln_ref[0:1, :], ln_ref[1:2, :])


def proj_ln(a, w, x, ln, tm=512):
    n, K = a.shape
    D = w.shape[1]
    return pl.pallas_call(
        _proj_ln_kernel, grid=(n // tm,),
        in_specs=[pl.BlockSpec((tm, K), lambda i: (i, 0)), pl.BlockSpec((K, D), lambda i: (0, 0)),
                  pl.BlockSpec((tm, D), lambda i: (i, 0)), pl.BlockSpec((2, D), lambda i: (0, 0))],
        out_specs=pl.BlockSpec((tm, D), lambda i: (i, 0)),
        out_shape=jax.ShapeDtypeStruct((n, D), F32),
        compiler_params=_cparams("parallel"), name="proj_res_ln",
    )(a, w, x, ln)


def _add_ln_kernel(h_ref, x_ref, ln_ref, o_ref):
    o_ref[...] = _layer_norm_rows(DN_ALPHA * x_ref[...] + h_ref[...], ln_ref[0:1, :], ln_ref[1:2, :])


def add_ln(h, x, ln, tm=512):
    n, D = x.shape
    row = pl.BlockSpec((tm, D), lambda i: (i, 0))
    return pl.pallas_call(
        _add_ln_kernel, grid=(n // tm,),
        in_specs=[row, row, pl.BlockSpec((2, D), lambda i: (0, 0))],
        out_specs=row, out_shape=jax.ShapeDtypeStruct((n, D), F32),
        compiler_params=_cparams("parallel"), name="res_ln",
    )(h, x, ln)


def _even_post_kernel(oaf_ref, oab_ref, ob_ref, ga_ref, gb_ref, x_ref, vec_ref, wout_ref, o_ref):
    silu = lambda t: t * (1.0 / (1.0 + jnp.exp(-t)))
    oa = oaf_ref[...] + oab_ref[...]
    ob = ob_ref[...]
    ga, gb = ga_ref[...], gb_ref[...]
    na, nb = vec_ref[0:1, :], vec_ref[1:2, :]
    ys = []
    for h in range(H_A):
        sl = slice(h * HEAD_A, (h + 1) * HEAD_A)
        t = oa[:, sl]
        t = t * lax.rsqrt(jnp.mean(t * t, axis=-1, keepdims=True) + 1e-6)
        ys.append(t * na[:, sl] * silu(ga[:, sl]))
    for h in range(H_B):
        sl = slice(h * HEAD_B, (h + 1) * HEAD_B)
        t = ob[:, sl]
        t = t - jnp.mean(t, axis=-1, keepdims=True)
        t = t * lax.rsqrt(jnp.mean(t * t, axis=-1, keepdims=True) + 1e-6)
        ys.append(t * nb[:, sl] * silu(gb[:, sl]))
    y = jnp.concatenate(ys, axis=1)
    h_ = jnp.dot(y.astype(BF16), wout_ref[...], preferred_element_type=F32)
    o_ref[...] = _layer_norm_rows(DN_ALPHA * x_ref[...] + h_, vec_ref[2:3, :], vec_ref[3:4, :])


def even_post(oa_f, oa_b, o_b, proj, x, vec, w_out, tm=256):
    n, D = x.shape
    half = pl.BlockSpec((tm, D_A), lambda i: (i, 0))
    row = pl.BlockSpec((tm, D), lambda i: (i, 0))
    return pl.pallas_call(
        _even_post_kernel, grid=(n // tm,),
        in_specs=[half, half, half, pl.BlockSpec((tm, D_A), lambda i: (i, 4)),
                  pl.BlockSpec((tm, D_B), lambda i: (i, 8)), row,
                  pl.BlockSpec(vec.shape, lambda i: (0, 0)), pl.BlockSpec(w_out.shape, lambda i: (0, 0))],
        out_specs=row,
        out_shape=jax.ShapeDtypeStruct((n, D), F32),
        compiler_params=_cparams("parallel"), name="even_post",
    )(oa_f, oa_b, o_b, proj, proj, x, vec, w_out)


def _even_layer(xf, Bn, T, w_in, lb, norm_a, norm_b, w_out, ln):
    n, D = xf.shape
    proj = pmm(xf, w_in).reshape(Bn, T, -1)
    lbp = jnp.stack([jnp.log(lb), jnp.log1p(-lb), 1.0 - lb], axis=1)
    oa_f, oa_b = hgrn2_scan(proj, lbp)
    o_b = retention(proj)
    z = jnp.zeros((D - D_A,), F32)
    vec = jnp.concatenate([jnp.stack([jnp.concatenate([norm_a, z]), jnp.concatenate([norm_b, z])]), ln,
                           jnp.zeros((4, D), F32)], 0)
    fl = lambda t: t.reshape(n, -1)
    return even_post(fl(oa_f), fl(oa_b), fl(o_b), fl(proj), xf, vec, w_out)


def _odd_weights(p, j):
    w1, w2, a1, a2 = p['od_w1'][j], p['od_w2'][j], p['od_a1'][j], p['od_a2'][j]
    z = jnp.zeros_like(w2[0])
    w2bd = jnp.concatenate([jnp.concatenate([w2[0], z], 1), jnp.concatenate([z, w2[1]], 1)], 0)
    a2bd = jnp.concatenate([jnp.concatenate([a2[0], z], 1), jnp.concatenate([z, a2[1]], 1)], 0)
    w0, a0 = p['od_w0'][j], p['od_a0'][j]
    zero = jnp.zeros_like(w0[0])
    vec_pre = jnp.stack([w0[0], w0[1], a0[0], a0[1], p['od_k_k'][j], p['od_k_a'][j], zero, zero])
    vec_post = jnp.stack([p['od_lnx_w'][j], p['od_lnx_b'][j], p['od_r_k'][j]])
    pre = (p['od_mu'][j], p['od_w_rkv'][j], jnp.concatenate([w1[0], w1[1]], 1), w2bd,
           jnp.concatenate([a1[0], a1[1]], 1), a2bd, p['od_g1'][j], p['od_g2'][j], vec_pre)
    return pre, vec_post


def _odd_layer(xf, Bn, T, pre, vec_post, w_out, ln):
    D = xf.shape[-1]
    r, v, kk, g, kd, bb, lw = odd_pre(xf, T, *pre)
    sh = lambda t: t.reshape(Bn, T, D)
    sh2 = lambda t: t.reshape(2, Bn, T, D)
    o_f, o_b = rwkv7_scan(sh(r), sh(v), sh(kk), sh2(lw), sh2(kd), sh2(bb))
    vec = jnp.concatenate([vec_post, ln, jnp.zeros((3, D), F32)], 0)
    return odd_post(o_f.reshape(-1, D), o_b.reshape(-1, D), r, v, g, kd, xf, vec, w_out)


def _cross_attn_layer(xf, Bn, T, mem, w_q, w_kv, w_out, ln):
    return cross_attn_layer(xf, T, _mm(mem, w_kv), w_q, w_out, ln)


def _moe_layer(xt, w_router, w_in, w_out, layer, ln):
    n, D = xt.shape
    cap = (CAP_FACTOR * n) // N_EXPERTS
    aff = jax.nn.softmax(pmm(xt, w_router), axis=-1)
    gate, idx = lax.top_k(aff.T, cap)
    xe = xt.astype(BF16)[idx]
    ye = moe_ffn(xe, w_in, w_out, gate[..., None], layer)
    y = jnp.zeros_like(xt).at[idx.reshape(-1)].add(ye.reshape(-1, D))
    return add_ln(y, xt, ln)


def _trunk(x, mem, p):
    Bn, T, D = x.shape
    xf = x.reshape(Bn * T, D)
    p_lb = jax.nn.softmax(p['ev_lb_logits'], axis=1)
    lb_all = jnp.clip(jnp.cumsum(p_lb, axis=1) - p_lb[:, :1], 0.0, 1.0)
    for layer in range(DEPTH):
        j = layer // 2
        ln = lambda s: jnp.stack([p['ln_w'][layer, s], p['ln_b'][layer, s]])
        if layer % 2 == 0:
            xf = _even_layer(xf, Bn, T, p['ev_w_in'][j], lb_all[:, j], p['ev_norm_a'][j], p['ev_norm_b'][j],
                             p['ev_w_out'][j], ln(0))
        else:
            pre, vec_post = _odd_weights(p, j)
            xf = _odd_layer(xf, Bn, T, pre, vec_post, p['od_w_out'][j], ln(0))
        xf = _cross_attn_layer(xf, Bn, T, mem, p['ca_w_q'][layer], p['ca_w_kv'][layer], p['ca_w_out'][layer], ln(1))
        xf = _moe_layer(xf, p['moe_router'][layer], p['moe_w_in'], p['moe_w_out'], layer, ln(2))
    return xf.reshape(Bn, T, D)


_MATMUL_WEIGHTS = ('ev_w_in', 'ev_w_out', 'od_w_rkv', 'od_w1', 'od_w2', 'od_a1', 'od_a2', 'od_g1', 'od_g2',
                   'od_w_out', 'ca_w_q', 'ca_w_kv', 'ca_w_out', 'moe_router')


def kernel(x_prompt, x_sample, mem_prompt, mem_sample, ev_w_in, ev_lb_logits, ev_norm_a, ev_norm_b, ev_w_out, od_mu, od_w_rkv, od_w0, od_w1, od_w2, od_a0, od_a1, od_a2, od_g1, od_g2, od_k_k, od_k_a, od_r_k, od_lnx_w, od_lnx_b, od_w_out, ca_w_q, ca_w_kv, ca_w_out, moe_router, moe_w_in, moe_w_out, ln_w, ln_b):
    params = dict(ev_w_in=ev_w_in, ev_lb_logits=ev_lb_logits, ev_norm_a=ev_norm_a, ev_norm_b=ev_norm_b,
                  ev_w_out=ev_w_out, od_mu=od_mu, od_w_rkv=od_w_rkv, od_w0=od_w0, od_w1=od_w1, od_w2=od_w2,
                  od_a0=od_a0, od_a1=od_a1, od_a2=od_a2, od_g1=od_g1, od_g2=od_g2, od_k_k=od_k_k,
                  od_k_a=od_k_a, od_r_k=od_r_k, od_lnx_w=od_lnx_w, od_lnx_b=od_lnx_b, od_w_out=od_w_out,
                  ca_w_q=ca_w_q, ca_w_kv=ca_w_kv, ca_w_out=ca_w_out, moe_router=moe_router,
                  moe_w_in=moe_w_in, moe_w_out=moe_w_out, ln_w=ln_w, ln_b=ln_b)
    for name in _MATMUL_WEIGHTS:
        params[name] = params[name].astype(BF16)
    y_prompt = _trunk(x_prompt, mem_prompt, params)
    y_sample = _trunk(x_sample, mem_sample, params)
    return (y_prompt, y_sample)
```

```python
import functools
import math

import jax
import jax.numpy as jnp
from jax import lax
from jax.experimental import pallas as pl
from jax.experimental.pallas import tpu as pltpu

F32 = jnp.float32
BF16 = jnp.bfloat16

D_MODEL = 1024
DEPTH = 4
D_A = D_MODEL // 2
HEAD_A = 128
H_A = D_A // HEAD_A
CHUNK_A = 32
D_B = D_MODEL // 2
HEAD_B = 128
H_B = D_B // HEAD_B
CHUNK_B = 128
ROPE_BASE = 10000.0
HEAD_C = 64
H_C = D_MODEL // HEAD_C
LNX_EPS = 64e-5
H_CA = 4
HEAD_CA = D_MODEL // H_CA
N_EXPERTS = 16
D_EXPERT = 2048
CAP_FACTOR = 2
DN_ALPHA = (2.0 * DEPTH) ** 0.25
LN_EPS = 1e-5

VMEM_LIMIT = 48 * 1024 * 1024


def _cparams(*sem):
    return pltpu.CompilerParams(dimension_semantics=sem, vmem_limit_bytes=VMEM_LIMIT)


def _mm_kernel(x_ref, w_ref, o_ref):
    o_ref[...] = jnp.dot(x_ref[...].astype(BF16), w_ref[...],
                         preferred_element_type=F32).astype(o_ref.dtype)


def _pick(n, pref):
    t = min(n, pref)
    while n % t:
        t //= 2
    return t


def pmm(x, w, tm=1024, tn=512):
    M, K = x.shape
    N = w.shape[1]
    tm = _pick(M, tm)
    tn = _pick(N, tn) if N % 128 == 0 else N
    return pl.pallas_call(
        _mm_kernel,
        grid=(M // tm, N // tn),
        in_specs=[pl.BlockSpec((tm, K), lambda i, j: (i, 0)),
                  pl.BlockSpec((K, tn), lambda i, j: (0, j))],
        out_specs=pl.BlockSpec((tm, tn), lambda i, j: (i, j)),
        out_shape=jax.ShapeDtypeStruct((M, N), F32),
        compiler_params=_cparams("parallel", "parallel"),
        name="dense_proj",
    )(x, w)


def _mm(x, w):
    lead = x.shape[:-1]
    return pmm(x.reshape(-1, x.shape[-1]), w).reshape(lead + (w.shape[1],))


def _ca_kernel(x_ref, wq_ref, k_ref, v_ref, wo_ref, ln_ref, o_ref):
    scale = HEAD_CA ** -0.5
    x = x_ref[...]
    q_all = jnp.dot(x.astype(BF16), wq_ref[...], preferred_element_type=F32).astype(BF16)
    outs = []
    for h in range(H_CA):
        sl = slice(h * HEAD_CA, (h + 1) * HEAD_CA)
        k = k_ref[0, :, sl].astype(BF16)
        v = v_ref[0, :, sl].astype(BF16)
        s = lax.dot_general(q_all[:, sl], k, (((1,), (1,)), ((), ())), preferred_element_type=F32) * scale
        m = jnp.max(s, axis=-1, keepdims=True)
        e = jnp.exp(s - m)
        p = e / jnp.sum(e, axis=-1, keepdims=True)
        outs.append(jnp.dot(p.astype(BF16), v, preferred_element_type=F32).astype(BF16))
    h_ = jnp.dot(jnp.concatenate(outs, axis=1), wo_ref[...], preferred_element_type=F32)
    y = DN_ALPHA * x + h_
    mu = jnp.mean(y, axis=-1, keepdims=True)
    yc = y - mu
    var = jnp.mean(yc * yc, axis=-1, keepdims=True)
    o_ref[...] = yc * lax.rsqrt(var + LN_EPS) * ln_ref[0:1, :] + ln_ref[1:2, :]


def cross_attn_layer(xf, T, kv, w_q, w_out, ln, tq=512):
    n, D = xf.shape
    M = kv.shape[1]
    tq = _pick(T, tq)
    per = T // tq
    return pl.pallas_call(
        _ca_kernel,
        grid=(n // tq,),
        in_specs=[pl.BlockSpec((tq, D), lambda i: (i, 0)),
                  pl.BlockSpec((D, D), lambda i: (0, 0)),
                  pl.BlockSpec((1, M, D), lambda i: (i // per, 0, 0)),
                  pl.BlockSpec((1, M, D), lambda i: (i // per, 0, 1)),
                  pl.BlockSpec((D, D), lambda i: (0, 0)),
                  pl.BlockSpec((2, D), lambda i: (0, 0))],
        out_specs=pl.BlockSpec((tq, D), lambda i: (i, 0)),
        out_shape=jax.ShapeDtypeStruct((n, D), F32),
        compiler_params=_cparams("parallel"),
        name="cross_attn",
    )(xf, w_q, kv, kv, w_out, ln)


MOE_CHUNK = 2048


def _moe_kernel(x_ref, wg_ref, wu_ref, wo_ref, gate_ref, o_ref, *, nf, tm):
    f = pl.program_id(2)
    rows = pl.ds(pl.multiple_of(pl.program_id(3) * tm, tm), tm)
    x = x_ref[0].astype(BF16)
    hg = jnp.dot(x, wg_ref[0].astype(BF16), preferred_element_type=F32)
    hu = jnp.dot(x, wu_ref[0].astype(BF16), preferred_element_type=F32)
    a = (hg * jax.nn.sigmoid(hg) * hu).astype(BF16)
    part = jnp.dot(a, wo_ref[0].astype(BF16), preferred_element_type=F32)

    @pl.when(f == 0)
    def _():
        o_ref[0, rows, :] = part

    @pl.when(jnp.logical_and(f > 0, f < nf - 1))
    def _():
        o_ref[0, rows, :] += part

    @pl.when(f == nf - 1)
    def _():
        o_ref[0, rows, :] = (o_ref[0, rows, :] + part) * gate_ref[0]


def moe_ffn(xe, w_in, w_out, gate, layer, tm=1024, tf=512):
    E, cap, D = xe.shape
    Fh = w_out.shape[2]
    ch = min(cap, MOE_CHUNK)
    tm = _pick(ch, tm)
    nf = Fh // tf
    nm = ch // tm
    return pl.pallas_call(
        functools.partial(_moe_kernel, nf=nf, tm=tm),
        grid=(E, cap // ch, nf, nm),
        in_specs=[pl.BlockSpec((1, tm, D), lambda e, h, f, m: (e, h * nm + m, 0)),
                  pl.BlockSpec((None, 1, D, tf), lambda e, h, f, m: (layer, e, 0, f)),
                  pl.BlockSpec((None, 1, D, tf), lambda e, h, f, m: (layer, e, 0, nf + f)),
                  pl.BlockSpec((None, 1, tf, D), lambda e, h, f, m: (layer, e, f, 0)),
                  pl.BlockSpec((1, tm, 1), lambda e, h, f, m: (e, h * nm + m, 0))],
        out_specs=pl.BlockSpec((1, ch, D), lambda e, h, f, m: (e, h, 0)),
        out_shape=jax.ShapeDtypeStruct((E, cap, D), F32),
        compiler_params=_cparams("parallel", "parallel", "arbitrary", "arbitrary"),
        name="moe_ffn",
    )(xe, w_in, w_in, w_out, gate)


_NN = (((1,), (0,)), ((), ()))
_NT = (((1,), (1,)), ((), ()))
_TN = (((0,), (0,)), ((), ()))


def _bdot(a, b, dims=_NN):
    return lax.dot_general(a.astype(BF16), b.astype(BF16), dims, preferred_element_type=F32)


def _split3(x):
    hi = x.astype(BF16)
    r1 = x - hi.astype(F32)
    mid = r1.astype(BF16)
    lo = (r1 - mid.astype(F32)).astype(BF16)
    return hi, mid, lo


def _cumdot(cum, x):
    hi, mid, lo = _split3(x)
    return (lax.dot_general(cum, hi, _NN, preferred_element_type=F32)
            + lax.dot_general(cum, mid, _NN, preferred_element_type=F32)
            + lax.dot_general(cum, lo, _NN, preferred_element_type=F32))


RW_L = 64
RW_TT = 512
RW_W = 128


def _rwkv_kernel(rf, vf, kkf, lwf, kdf, bbf, rb, vb, kkb, lwb, kdb, bbb, of_ref, ob_ref,
                 mf_ref, mb_ref, om_ref, th_ref):
    L = RW_L
    nc = RW_TT // L
    S = 2 * L

    @pl.when(pl.program_id(2) == 0)
    def _():
        mf_ref[...] = jnp.zeros_like(mf_ref)
        mb_ref[...] = jnp.zeros_like(mb_ref)

    row = lax.broadcasted_iota(jnp.int32, (S, S), 0)
    col = lax.broadcasted_iota(jnp.int32, (S, S), 1)
    strict = {False: col < row, True: col > row}
    incl = {False: col <= row, True: col >= row}
    incl2 = {d: jnp.concatenate([incl[d], incl[d]], axis=1) for d in (False, True)}
    eye = (row == col).astype(F32)
    cr = lax.broadcasted_iota(jnp.int32, (L, L), 0)
    cc = lax.broadcasted_iota(jnp.int32, (L, L), 1)
    cum = {False: (cc <= cr).astype(BF16), True: (cc >= cr).astype(BF16)}
    lane = lax.broadcasted_iota(jnp.int32, (L, RW_W), 1)
    m0 = lane < HEAD_C

    def stack(x):
        return jnp.concatenate([jnp.where(m0, x, 0.0), jnp.where(m0, 0.0, x)], axis=0)

    refs = {False: (rf, vf, kkf, lwf, kdf, bbf), True: (rb, vb, kkb, lwb, kdb, bbb)}
    items = [(d, c) for c in range(nc) for d in (False, True)]
    ni = len(items)
    cs = range(ni)
    ld = lambda i, which: refs[items[i][0]][which][0, pl.ds(items[i][1] * L, L), :]
    lws = [ld(i, 3) for i in cs]
    gs = [_cumdot(cum[items[i][0]], lws[i]) for i in cs]
    gls = [(gs[i][0:1, :] if items[i][0] else gs[i][L - 1:L, :]) for i in cs]
    lhs, rhs_a, kbl, vss, rgs, ags = [], [], [], [], [], []
    for i in cs:
        g, gl = gs[i], gls[i]
        eg, eng, egl = jnp.exp(g), jnp.exp(-g), jnp.exp(gl - g)
        kd, bb = ld(i, 4), ld(i, 5)
        rg_s, ag_s = stack(ld(i, 0) * eg), stack(ld(i, 2) * jnp.exp(g - lws[i]))
        rgs.append(rg_s)
        ags.append(ag_s)
        lhs.append(jnp.concatenate([ag_s, rg_s], axis=0))
        rhs_a.append(jnp.concatenate([stack(kd * eng), stack(bb * eng)], axis=0))
        kbl.append(jnp.concatenate([stack(kd * egl), stack(bb * egl)], axis=0))
        vss.append(stack(ld(i, 1)))
    amats = [_bdot(lhs[i], rhs_a[i], _NT) for i in cs]
    aaks = [jnp.where(strict[items[i][0]], amats[i][0:S, 0:S], 0.0) for i in cs]
    ns = [jnp.where(strict[items[i][0]], amats[i][0:S, S:2 * S], 0.0) for i in cs]
    arkb = [jnp.where(incl2[items[i][0]], amats[i][S:2 * S, :], 0.0) for i in cs]
    ts = [eye + n for n in ns]
    ps = [_bdot(n, n) for n in ns]
    avs = [_bdot(aaks[i], vss[i]) for i in cs]
    for _ in range(4):
        pps = [_bdot(ps[i], jnp.concatenate([ps[i], ts[i]], axis=1)) for i in cs]
        ts = [ts[i] + pps[i][:, S:2 * S] for i in cs]
        ps = [pps[i][:, 0:S] for i in cs]
    ts = [ts[i] + _bdot(ps[i], ts[i]) for i in cs]
    w12s = [_bdot(ts[i], jnp.concatenate([ags[i], avs[i]], axis=1)) for i in cs]
    rhss = [jnp.concatenate([jnp.concatenate([jnp.zeros_like(vss[i]), vss[i]], axis=1), w12s[i]], axis=0) for i in cs]
    sts = [_bdot(kbl[i], rhss[i], _TN) for i in cs]
    ots = [_bdot(arkb[i], rhss[i]) for i in cs]
    for i in cs:
        phi = sts[i][:, 0:RW_W] + eye * jnp.exp(gls[i])
        om_ref[i] = jnp.concatenate([ots[i][:, 0:RW_W] + rgs[i], phi], axis=0)
        th_ref[i] = jnp.concatenate([ots[i][:, RW_W:2 * RW_W], sts[i][:, RW_W:2 * RW_W]], axis=0)

    m = {False: mf_ref[...], True: mb_ref[...]}
    outs = {False: of_ref, True: ob_ref}
    for j in range(nc):
        for d in (False, True):
            c = nc - 1 - j if d else j
            i = items.index((d, c))
            res = _bdot(om_ref[i], m[d]) + th_ref[i]
            outs[d][0, pl.ds(c * L, L), :] = res[0:L] + res[L:S]
            m[d] = res[S:2 * S]
    mf_ref[...] = m[False]
    mb_ref[...] = m[True]


def rwkv7_scan(r, v, kk, lw, kd, bb):
    B, T, D = r.shape
    nt = T // RW_TT
    f3 = pl.BlockSpec((1, RW_TT, RW_W), lambda b, h, i: (b, i, h))
    r3 = pl.BlockSpec((1, RW_TT, RW_W), lambda b, h, i: (b, nt - 1 - i, h))
    f4 = pl.BlockSpec((None, 1, RW_TT, RW_W), lambda b, h, i: (0, b, i, h))
    r4 = pl.BlockSpec((None, 1, RW_TT, RW_W), lambda b, h, i: (1, b, nt - 1 - i, h))
    nc = RW_TT // RW_L
    out = jax.ShapeDtypeStruct((B, T, D), F32)
    return pl.pallas_call(
        _rwkv_kernel,
        grid=(B, D // RW_W, nt),
        in_specs=[f3, f3, f3, f4, f4, f4, r3, r3, r3, r4, r4, r4],
        out_specs=[f3, r3],
        out_shape=[out, out],
        scratch_shapes=[pltpu.VMEM((RW_W, RW_W), F32), pltpu.VMEM((RW_W, RW_W), F32),
                        pltpu.VMEM((2 * nc, 4 * RW_L, RW_W), F32),
                        pltpu.VMEM((2 * nc, 4 * RW_L, RW_W), F32)],
        compiler_params=_cparams("parallel", "parallel", "arbitrary"),
        name="rwkv7",
    )(r, v, kk, lw, kd, bb, r, v, kk, lw, kd, bb)


HG_SUB = 16
HG_GRP = 128
HG_TT = 512


def _hgrn2_group(q_ref, v_ref, z_ref, lb_ref, o_ref, st_ref, g0, reverse):
    C, G = HG_SUB, HG_GRP
    log_lb, log_1mlb, one_mlb = lb_ref[0, 0:1, :], lb_ref[0, 1:2, :], lb_ref[0, 2:3, :]
    row = lax.broadcasted_iota(jnp.int32, (G, G), 0)
    col = lax.broadcasted_iota(jnp.int32, (G, G), 1)
    same = (row // C) == (col // C)
    cum = (same & ((col >= row) if reverse else (col <= row))).astype(BF16)
    sub_h = lax.broadcasted_iota(jnp.int32, (C // 2, HEAD_A), 0)
    rows = pl.ds(pl.multiple_of(g0 * G, G), G)
    z = z_ref[0, rows, :]
    a_q = q_ref[0, rows, :]
    vv = v_ref[0, rows, :]
    ls = jnp.minimum(z, 0.0) - jnp.log1p(jnp.exp(-jnp.abs(z)))
    t2 = log_1mlb + ls
    mx = jnp.maximum(log_lb, t2)
    lf = mx + jnp.log1p(jnp.exp(-jnp.abs(log_lb - t2)))
    kk = one_mlb * (1.0 / (1.0 + jnp.exp(z)))
    qq = a_q * (1.0 / (1.0 + jnp.exp(-a_q)))
    b = _cumdot(cum, lf)
    st = st_ref[...]
    order = range(G // C - 1, -1, -1) if reverse else range(G // C)
    outs = [None] * (G // C)
    for j in order:
        sl = slice(j * C, (j + 1) * C)
        bj, qj, kj, vj = b[sl], qq[sl], kk[sl], vv[sl]
        e_in = 0 if reverse else C - 1
        b_end = bj[e_in:e_in + 1, :]
        H = C // 2
        acc = [None, None]
        for s in range(C):
            bs, ks, vs = bj[s:s + 1, :], kj[s:s + 1, :], vj[s:s + 1, :]
            hs = s // H
            full = (hs - 1) if reverse else (hs + 1)
            for h in (0, 1):
                if h != hs and h != full:
                    continue
                rows_h = slice(h * H, (h + 1) * H)
                diff = bj[rows_h] - bs
                if h == hs:
                    keep = (sub_h <= s - hs * H) if reverse else (sub_h >= s - hs * H)
                    diff = jnp.where(keep, diff, -1e30)
                sc = jnp.sum(jnp.exp(diff) * (qj[rows_h] * ks), axis=-1, keepdims=True)
                acc[h] = sc * vs if acc[h] is None else acc[h] + sc * vs
        outs[j] = jnp.concatenate(acc, axis=0) + _bdot(qj * jnp.exp(bj), st, _NT)
        st = st * jnp.exp(b_end) + _bdot(vj, kj * jnp.exp(b_end - bj), _TN)
    st_ref[...] = st
    o_ref[0, rows, :] = jnp.concatenate(outs, axis=0)


def _hgrn2_kernel(qf_ref, vf_ref, zf_ref, qb_ref, vb_ref, zb_ref, lbf_ref, lbb_ref, of_ref, ob_ref, stf_ref, stb_ref):
    @pl.when(pl.program_id(2) == 0)
    def _():
        stf_ref[...] = jnp.zeros_like(stf_ref)
        stb_ref[...] = jnp.zeros_like(stb_ref)

    ng = HG_TT // HG_GRP

    def body(gi, carry):
        _hgrn2_group(qf_ref, vf_ref, zf_ref, lbf_ref, of_ref, stf_ref, gi, False)
        _hgrn2_group(qb_ref, vb_ref, zb_ref, lbb_ref, ob_ref, stb_ref, ng - 1 - gi, True)
        return carry

    lax.fori_loop(0, ng, body, 0)


def hgrn2_scan(proj, lbp):
    B, T, _ = proj.shape
    nt = T // HG_TT
    blk = (1, HG_TT, HEAD_A)
    fmap = lambda off: (lambda b, h, i: (b, i, off + h))
    rmap = lambda off: (lambda b, h, i: (b, nt - 1 - i, off + h))
    out = jax.ShapeDtypeStruct((B, T, D_A), F32)
    return pl.pallas_call(
        _hgrn2_kernel,
        grid=(B, H_A, nt),
        in_specs=[pl.BlockSpec(blk, fmap(0)), pl.BlockSpec(blk, fmap(H_A)), pl.BlockSpec(blk, fmap(2 * H_A)),
                  pl.BlockSpec(blk, rmap(0)), pl.BlockSpec(blk, rmap(H_A)), pl.BlockSpec(blk, rmap(3 * H_A)),
                  pl.BlockSpec((1, 3, HEAD_A), lambda b, h, i: (0, 0, h)),
                  pl.BlockSpec((1, 3, HEAD_A), lambda b, h, i: (1, 0, h))],
        out_specs=[pl.BlockSpec(blk, fmap(0)), pl.BlockSpec(blk, rmap(0))],
        out_shape=[out, out],
        scratch_shapes=[pltpu.VMEM((HEAD_A, HEAD_A), F32), pltpu.VMEM((HEAD_A, HEAD_A), F32)],
        compiler_params=_cparams("parallel", "parallel", "arbitrary"),
        name="hgrn2",
    )(proj, proj, proj, proj, proj, proj, lbp, lbp)


RT_TT = 256


def rope_tables(T):
    theta = 1.0 / jnp.power(ROPE_BASE, jnp.linspace(0.0, 1.0, HEAD_B // 2, dtype=F32))
    ang = jnp.arange(T, dtype=F32)[:, None] * theta[None, :]
    cos = jnp.repeat(jnp.cos(ang), 2, axis=1)
    sin = jnp.repeat(jnp.sin(ang), 2, axis=1)
    sign = jnp.where(jnp.arange(HEAD_B) % 2 == 0, -1.0, 1.0).astype(F32)
    return cos, sin * sign[None, :]


def _rope(t, cos, sin):
    lane = lax.broadcasted_iota(jnp.int32, t.shape, 1)
    nxt = pltpu.roll(t, t.shape[1] - 1, axis=1)
    prv = pltpu.roll(t, 1, axis=1)
    return t * cos + jnp.where(lane % 2 == 0, nxt, prv) * sin


def _ret_fwd_kernel(lg_ref, cos_ref, sin_ref, q_ref, k_ref, v_ref, o_ref, st_ref):
    Tt = RT_TT

    @pl.when(pl.program_id(2) == 0)
    def _():
        st_ref[...] = jnp.zeros_like(st_ref)

    lgw = lg_ref[0]
    lg = lgw[:, 0:HEAD_B]
    cos, sin = cos_ref[...], sin_ref[...]
    q = _rope(q_ref[0], cos, sin)
    k = _rope(k_ref[0], cos, sin) * (HEAD_B ** -0.5)
    v = v_ref[0]
    row = lax.broadcasted_iota(jnp.int32, (Tt, Tt), 0)
    col = lax.broadcasted_iota(jnp.int32, (Tt, Tt), 1)
    dmat = jnp.exp(lgw * jnp.abs(row - col).astype(F32))
    pos = lax.broadcasted_iota(jnp.int32, (Tt, 1), 0).astype(F32)
    st = st_ref[...]
    o = _bdot(_bdot(q, k, _NT) * dmat, v)
    o_ref[0] = o + _bdot(q * jnp.exp(lg * (pos + 1.0)), st)
    st_ref[...] = st * jnp.exp(lg * Tt) + _bdot(k * jnp.exp(lg * (Tt - 1.0 - pos)), v, _TN)


def _ret_bwd_kernel(lg_ref, cos_ref, sin_ref, q_ref, k_ref, v_ref, oin_ref, o_ref, st_ref):
    Tt = RT_TT

    @pl.when(pl.program_id(2) == 0)
    def _():
        st_ref[...] = jnp.zeros_like(st_ref)

    lg = lg_ref[0][:, 0:HEAD_B]
    cos, sin = cos_ref[...], sin_ref[...]
    q = _rope(q_ref[0], cos, sin)
    k = _rope(k_ref[0], cos, sin) * (HEAD_B ** -0.5)
    pos = lax.broadcasted_iota(jnp.int32, (Tt, 1), 0).astype(F32)
    st = st_ref[...]
    o_ref[0] = oin_ref[0] + _bdot(q * jnp.exp(lg * (Tt - pos)), st)
    st_ref[...] = st * jnp.exp(lg * Tt) + _bdot(k * jnp.exp(lg * pos), v_ref[0], _TN)


def retention(proj):
    B, T, _ = proj.shape
    nt = T // RT_TT
    blk = (1, RT_TT, HEAD_B)
    qoff = 5 * D_A // HEAD_B
    koff, voff = qoff + H_B, qoff + 2 * H_B
    cos, sin = rope_tables(T)
    lgam = jnp.log1p(-jnp.power(2.0, -5.0 - jnp.arange(H_B, dtype=F32)))
    lgam = jnp.broadcast_to(lgam[:, None, None], (H_B, 1, RT_TT))
    lspec = pl.BlockSpec((1, 1, RT_TT), lambda b, h, i: (h, 0, 0))

    def specs(tix):
        tab = pl.BlockSpec((RT_TT, HEAD_B), lambda b, h, i: (tix(i), 0))
        return [lspec, tab, tab] + [pl.BlockSpec(blk, (lambda off: lambda b, h, i: (b, tix(i), off + h))(off))
                                    for off in (qoff, koff, voff)]
    fwd = lambda i: i
    rev = lambda i: nt - 1 - i
    o1 = pl.pallas_call(
        _ret_fwd_kernel, grid=(B, H_B, nt),
        in_specs=specs(fwd),
        out_specs=pl.BlockSpec(blk, lambda b, h, i: (b, i, h)),
        out_shape=jax.ShapeDtypeStruct((B, T, D_B), F32),
        scratch_shapes=[pltpu.VMEM((HEAD_B, HEAD_B), F32)],
        compiler_params=_cparams("parallel", "parallel", "arbitrary"), name="retention_fwd",
    )(lgam, cos, sin, proj, proj, proj)
    return pl.pallas_call(
        _ret_bwd_kernel, grid=(B, H_B, nt),
        in_specs=specs(rev) + [pl.BlockSpec(blk, lambda b, h, i: (b, nt - 1 - i, h))],
        out_specs=pl.BlockSpec(blk, lambda b, h, i: (b, nt - 1 - i, h)),
        out_shape=jax.ShapeDtypeStruct((B, T, D_B), F32),
        scratch_shapes=[pltpu.VMEM((HEAD_B, HEAD_B), F32)],
        input_output_aliases={6: 0},
        compiler_params=_cparams("parallel", "parallel", "arbitrary"), name="retention_bwd",
    )(lgam, cos, sin, proj, proj, proj, o1)


LANES = 128


def _head_ones():
    r = lax.broadcasted_iota(jnp.int32, (LANES, LANES), 0) // HEAD_C
    c = lax.broadcasted_iota(jnp.int32, (LANES, LANES), 1) // HEAD_C
    return (r == c).astype(BF16)


def _head_sum(x, bd):
    hi = x.astype(BF16)
    lo = (x - hi.astype(F32)).astype(BF16)
    return (lax.dot_general(hi, bd, _NN, preferred_element_type=F32)
            + lax.dot_general(lo, bd, _NN, preferred_element_type=F32))


def _layer_norm_rows(y, w, b):
    mu = jnp.mean(y, axis=-1, keepdims=True)
    yc = y - mu
    var = jnp.mean(yc * yc, axis=-1, keepdims=True)
    return yc * lax.rsqrt(var + LN_EPS) * w + b


def _full_spec(a):
    return pl.BlockSpec(a.shape, lambda i: (0,) * a.ndim)


def _odd_pre_kernel(x_ref, xp_ref, xn_ref, mu_ref, wrkv_ref, w1_ref, w2_ref, a1_ref, a2_ref, g1_ref, g2_ref,
                    vec_ref, r_ref, v_ref, kk_ref, g_ref, kd_ref, bb_ref, lw_ref, *, tiles_per_seq):
    D = D_MODEL
    x = x_ref[...]
    tm = x.shape[0]
    t_in_seq = pl.program_id(0) % tiles_per_seq
    prev_row = jnp.where(t_in_seq == 0, 0.0, xp_ref[0, 7:8, :])
    next_row = jnp.where(t_in_seq == tiles_per_seq - 1, 0.0, xn_ref[0, 0:1, :])
    rid = lax.broadcasted_iota(jnp.int32, x.shape, 0)
    x_prev = jnp.where(rid == 0, prev_row, pltpu.roll(x, 1, axis=0))
    x_next = jnp.where(rid == tm - 1, next_row, pltpu.roll(x, tm - 1, axis=0))
    xx = 0.5 * (x_prev + x_next) - x
    mix = lambda p: (x + xx * mu_ref[p:p + 1, :]).astype(BF16)
    r = jnp.dot(mix(0), wrkv_ref[0], preferred_element_type=F32)
    k = jnp.dot(mix(1), wrkv_ref[1], preferred_element_type=F32)
    v = jnp.dot(mix(2), wrkv_ref[2], preferred_element_type=F32)
    hw = jnp.tanh(jnp.dot(mix(3), w1_ref[...], preferred_element_type=F32))
    lora_w = _bdot(hw, w2_ref[...])
    lora_a = _bdot(jnp.dot(mix(4), a1_ref[...], preferred_element_type=F32), a2_ref[...])
    hg = jnp.dot(mix(5), g1_ref[...], preferred_element_type=F32)
    g = _bdot(1.0 / (1.0 + jnp.exp(-hg)), g2_ref[...])
    k_k, k_a = vec_ref[4:5, :], vec_ref[5:6, :]
    bd = _head_ones()
    kq = k * k_k
    kks = []
    for s in range(D // LANES):
        ks = kq[:, s * LANES:(s + 1) * LANES]
        nrm = jnp.sqrt(_head_sum(ks * ks, bd))
        kks.append(ks / jnp.maximum(nrm, 1e-12))
    kk = jnp.concatenate(kks, axis=1)
    r_ref[...] = r.astype(r_ref.dtype)
    v_ref[...] = v.astype(v_ref.dtype)
    kk_ref[...] = kk.astype(kk_ref.dtype)
    g_ref[...] = g.astype(g_ref.dtype)
    for n in range(2):
        u = vec_ref[n:n + 1, :] + lora_w[:, n * D:(n + 1) * D]
        lw_ref[n] = -math.exp(-0.5) / (1.0 + jnp.exp(-u))
        a = 1.0 / (1.0 + jnp.exp(-(vec_ref[2 + n:3 + n, :] + lora_a[:, n * D:(n + 1) * D])))
        kd_ref[n] = (k * (1.0 + (a - 1.0) * k_a)).astype(kd_ref.dtype)
        bb_ref[n] = (-(kk * a)).astype(bb_ref.dtype)


def odd_pre(x, T, mu, wrkv, w1c, w2bd, a1c, a2bd, g1, g2, vec, tm=256):
    n, D = x.shape
    row = pl.BlockSpec((tm, D), lambda i: (i, 0))
    x8 = x.reshape(n // 8, 8, D)
    prev8 = pl.BlockSpec((1, 8, D), lambda i: (jnp.maximum(i * (tm // 8) - 1, 0), 0, 0))
    next8 = pl.BlockSpec((1, 8, D), lambda i: (jnp.minimum((i + 1) * (tm // 8), n // 8 - 1), 0, 0))
    row2 = pl.BlockSpec((2, tm, D), lambda i: (0, i, 0))
    o1 = jax.ShapeDtypeStruct((n, D), BF16)
    o2 = jax.ShapeDtypeStruct((2, n, D), BF16)
    return pl.pallas_call(
        functools.partial(_odd_pre_kernel, tiles_per_seq=T // tm), grid=(n // tm,),
        in_specs=[row, prev8, next8] + [_full_spec(a) for a in (mu, wrkv, w1c, w2bd, a1c, a2bd, g1, g2, vec)],
        out_specs=[row, row, row, row, row2, row2, row2],
        out_shape=[o1, o1, o1, o1, o2, o2, jax.ShapeDtypeStruct((2, n, D), F32)],
        compiler_params=_cparams("parallel"), name="rwkv7_pre",
    )(x, x8, x8, mu, wrkv, w1c, w2bd, a1c, a2bd, g1, g2, vec)


def _odd_post_kernel(of_ref, ob_ref, r_ref, v_ref, g_ref, kd_ref, x_ref, vec_ref, wout_ref, o_ref):
    D = D_MODEL
    bd = _head_ones()
    lnx_w, lnx_b, r_k = vec_ref[0:1, :], vec_ref[1:2, :], vec_ref[2:3, :]
    o = of_ref[...] + ob_ref[...]
    rk = r_ref[...].astype(F32) * r_k * (kd_ref[0].astype(F32) + kd_ref[1].astype(F32))
    v = v_ref[...].astype(F32)
    ys = []
    for s in range(D // LANES):
        sl = slice(s * LANES, (s + 1) * LANES)
        os_ = o[:, sl]
        oc = os_ - _head_sum(os_, bd) * (1.0 / HEAD_C)
        var = _head_sum(oc * oc, bd) * (1.0 / HEAD_C)
        on = oc * lax.rsqrt(var + LNX_EPS) * lnx_w[:, sl] + lnx_b[:, sl]
        ys.append(on + _head_sum(rk[:, sl], bd) * v[:, sl])
    y = jnp.concatenate(ys, axis=1) * g_ref[...].astype(F32)
    h = jnp.dot(y.astype(BF16), wout_ref[...], preferred_element_type=F32)
    o_ref[...] = _layer_norm_rows(DN_ALPHA * x_ref[...] + h, vec_ref[3:4, :], vec_ref[4:5, :])


def odd_post(o_f, o_b, r, v, g, kd, x, vec, w_out, tm=256):
    n, D = x.shape
    row = pl.BlockSpec((tm, D), lambda i: (i, 0))
    row2 = pl.BlockSpec((2, tm, D), lambda i: (0, i, 0))
    return pl.pallas_call(
        _odd_post_kernel, grid=(n // tm,),
        in_specs=[row, row, row, row, row, row2, row, _full_spec(vec), _full_spec(w_out)],
        out_specs=row,
        out_shape=jax.ShapeDtypeStruct((n, D), F32),
        compiler_params=_cparams("parallel"), name="rwkv7_post",
    )(o_f, o_b, r, v, g, kd, x, vec, w_out)


def _proj_ln_kernel(a_ref, w_ref, x_ref, ln_ref, o_ref):
    h = jnp.dot(a_ref[...].astype(BF16), w_ref[...], preferred_element_type=F32)
    o_ref[...] = _layer_norm_rows(DN_ALPHA * x_ref[...] + h, ln_ref[0:1, :], ln_ref[1:2, :])


def proj_ln(a, w, x, ln, tm=512):
    n, K = a.shape
    D = w.shape[1]
    return pl.pallas_call(
        _proj_ln_kernel, grid=(n // tm,),
        in_specs=[pl.BlockSpec((tm, K), lambda i: (i, 0)), pl.BlockSpec((K, D), lambda i: (0, 0)),
                  pl.BlockSpec((tm, D), lambda i: (i, 0)), pl.BlockSpec((2, D), lambda i: (0, 0))],
        out_specs=pl.BlockSpec((tm, D), lambda i: (i, 0)),
        out_shape=jax.ShapeDtypeStruct((n, D), F32),
        compiler_params=_cparams("parallel"), name="proj_res_ln",
    )(a, w, x, ln)


def _add_ln_kernel(h_ref, x_ref, ln_ref, o_ref):
    o_ref[...] = _layer_norm_rows(DN_ALPHA * x_ref[...] + h_ref[...], ln_ref[0:1, :], ln_ref[1:2, :])


def add_ln(h, x, ln, tm=512):
    n, D = x.shape
    row = pl.BlockSpec((tm, D), lambda i: (i, 0))
    return pl.pallas_call(
        _add_ln_kernel, grid=(n // tm,),
        in_specs=[row, row, pl.BlockSpec((2, D), lambda i: (0, 0))],
        out_specs=row, out_shape=jax.ShapeDtypeStruct((n, D), F32),
        compiler_params=_cparams("parallel"), name="res_ln",
    )(h, x, ln)


def _even_post_kernel(oaf_ref, oab_ref, ob_ref, ga_ref, gb_ref, x_ref, vec_ref, wout_ref, o_ref):
    silu = lambda t: t * (1.0 / (1.0 + jnp.exp(-t)))
    oa = oaf_ref[...] + oab_ref[...]
    ob = ob_ref[...]
    ga, gb = ga_ref[...], gb_ref[...]
    na, nb = vec_ref[0:1, :], vec_ref[1:2, :]
    ys = []
    for h in range(H_A):
        sl = slice(h * HEAD_A, (h + 1) * HEAD_A)
        t = oa[:, sl]
        t = t * lax.rsqrt(jnp.mean(t * t, axis=-1, keepdims=True) + 1e-6)
        ys.append(t * na[:, sl] * silu(ga[:, sl]))
    for h in range(H_B):
        sl = slice(h * HEAD_B, (h + 1) * HEAD_B)
        t = ob[:, sl]
        t = t - jnp.mean(t, axis=-1, keepdims=True)
        t = t * lax.rsqrt(jnp.mean(t * t, axis=-1, keepdims=True) + 1e-6)
        ys.append(t * nb[:, sl] * silu(gb[:, sl]))
    y = jnp.concatenate(ys, axis=1)
    h_ = jnp.dot(y.astype(BF16), wout_ref[...], preferred_element_type=F32)
    o_ref[...] = _layer_norm_rows(DN_ALPHA * x_ref[...] + h_, vec_ref[2:3, :], vec_ref[3:4, :])


def even_post(oa_f, oa_b, o_b, proj, x, vec, w_out, tm=256):
    n, D = x.shape
    half = pl.BlockSpec((tm, D_A), lambda i: (i, 0))
    row = pl.BlockSpec((tm, D), lambda i: (i, 0))
    return pl.pallas_call(
        _even_post_kernel, grid=(n // tm,),
        in_specs=[half, half, half, pl.BlockSpec((tm, D_A), lambda i: (i, 4)),
                  pl.BlockSpec((tm, D_B), lambda i: (i, 8)), row,
                  pl.BlockSpec(vec.shape, lambda i: (0, 0)), pl.BlockSpec(w_out.shape, lambda i: (0, 0))],
        out_specs=row,
        out_shape=jax.ShapeDtypeStruct((n, D), F32),
        compiler_params=_cparams("parallel"), name="even_post",
    )(oa_f, oa_b, o_b, proj, proj, x, vec, w_out)


def _even_layer(xf, Bn, T, w_in, lb, norm_a, norm_b, w_out, ln):
    n, D = xf.shape
    proj = pmm(xf, w_in).reshape(Bn, T, -1)
    lbp = jnp.stack([jnp.log(lb), jnp.log1p(-lb), 1.0 - lb], axis=1)
    oa_f, oa_b = hgrn2_scan(proj, lbp)
    o_b = retention(proj)
    z = jnp.zeros((D - D_A,), F32)
    vec = jnp.concatenate([jnp.stack([jnp.concatenate([norm_a, z]), jnp.concatenate([norm_b, z])]), ln,
                           jnp.zeros((4, D), F32)], 0)
    fl = lambda t: t.reshape(n, -1)
    return even_post(fl(oa_f), fl(oa_b), fl(o_b), fl(proj), xf, vec, w_out)


def _odd_weights(p, j):
    w1, w2, a1, a2 = p['od_w1'][j], p['od_w2'][j], p['od_a1'][j], p['od_a2'][j]
    z = jnp.zeros_like(w2[0])
    w2bd = jnp.concatenate([jnp.concatenate([w2[0], z], 1), jnp.concatenate([z, w2[1]], 1)], 0)
    a2bd = jnp.concatenate([jnp.concatenate([a2[0], z], 1), jnp.concatenate([z, a2[1]], 1)], 0)
    w0, a0 = p['od_w0'][j], p['od_a0'][j]
    zero = jnp.zeros_like(w0[0])
    vec_pre = jnp.stack([w0[0], w0[1], a0[0], a0[1], p['od_k_k'][j], p['od_k_a'][j], zero, zero])
    vec_post = jnp.stack([p['od_lnx_w'][j], p['od_lnx_b'][j], p['od_r_k'][j]])
    pre = (p['od_mu'][j], p['od_w_rkv'][j], jnp.concatenate([w1[0], w1[1]], 1), w2bd,
           jnp.concatenate([a1[0], a1[1]], 1), a2bd, p['od_g1'][j], p['od_g2'][j], vec_pre)
    return pre, vec_post


def _odd_layer(xf, Bn, T, pre, vec_post, w_out, ln):
    D = xf.shape[-1]
    r, v, kk, g, kd, bb, lw = odd_pre(xf, T, *pre)
    sh = lambda t: t.reshape(Bn, T, D)
    sh2 = lambda t: t.reshape(2, Bn, T, D)
    o_f, o_b = rwkv7_scan(sh(r), sh(v), sh(kk), sh2(lw), sh2(kd), sh2(bb))
    vec = jnp.concatenate([vec_post, ln, jnp.zeros((3, D), F32)], 0)
    return odd_post(o_f.reshape(-1, D), o_b.reshape(-1, D), r, v, g, kd, xf, vec, w_out)


def _cross_attn_layer(xf, Bn, T, mem, w_q, w_kv, w_out, ln):
    return cross_attn_layer(xf, T, _mm(mem, w_kv), w_q, w_out, ln)


def _moe_layer(xt, w_router, w_in, w_out, layer, ln):
    n, D = xt.shape
    cap = (CAP_FACTOR * n) // N_EXPERTS
    aff = jax.nn.softmax(pmm(xt, w_router), axis=-1)
    gate, idx = lax.top_k(aff.T, cap)
    xe = xt[idx]
    ye = moe_ffn(xe, w_in, w_out, gate[..., None], layer)
    y = jnp.zeros_like(xt).at[idx.reshape(-1)].add(ye.reshape(-1, D))
    return add_ln(y, xt, ln)


def _trunk(x, mem, p):
    Bn, T, D = x.shape
    xf = x.reshape(Bn * T, D)
    p_lb = jax.nn.softmax(p['ev_lb_logits'], axis=1)
    lb_all = jnp.clip(jnp.cumsum(p_lb, axis=1) - p_lb[:, :1], 0.0, 1.0)
    for layer in range(DEPTH):
        j = layer // 2
        ln = lambda s: jnp.stack([p['ln_w'][layer, s], p['ln_b'][layer, s]])
        if layer % 2 == 0:
            xf = _even_layer(xf, Bn, T, p['ev_w_in'][j], lb_all[:, j], p['ev_norm_a'][j], p['ev_norm_b'][j],
                             p['ev_w_out'][j], ln(0))
        else:
            pre, vec_post = _odd_weights(p, j)
            xf = _odd_layer(xf, Bn, T, pre, vec_post, p['od_w_out'][j], ln(0))
        xf = _cross_attn_layer(xf, Bn, T, mem, p['ca_w_q'][layer], p['ca_w_kv'][layer], p['ca_w_out'][layer], ln(1))
        xf = _moe_layer(xf, p['moe_router'][layer], p['moe_w_in'], p['moe_w_out'], layer, ln(2))
    return xf.reshape(Bn, T, D)


_MATMUL_WEIGHTS = ('ev_w_in', 'ev_w_out', 'od_w_rkv', 'od_w1', 'od_w2', 'od_a1', 'od_a2', 'od_g1', 'od_g2',
                   'od_w_out', 'ca_w_q', 'ca_w_kv', 'ca_w_out', 'moe_router')


def kernel(x_prompt, x_sample, mem_prompt, mem_sample, ev_w_in, ev_lb_logits, ev_norm_a, ev_norm_b, ev_w_out, od_mu, od_w_rkv, od_w0, od_w1, od_w2, od_a0, od_a1, od_a2, od_g1, od_g2, od_k_k, od_k_a, od_r_k, od_lnx_w, od_lnx_b, od_w_out, ca_w_q, ca_w_kv, ca_w_out, moe_router, moe_w_in, moe_w_out, ln_w, ln_b):
    params = dict(ev_w_in=ev_w_in, ev_lb_logits=ev_lb_logits, ev_norm_a=ev_norm_a, ev_norm_b=ev_norm_b,
                  ev_w_out=ev_w_out, od_mu=od_mu, od_w_rkv=od_w_rkv, od_w0=od_w0, od_w1=od_w1, od_w2=od_w2,
                  od_a0=od_a0, od_a1=od_a1, od_a2=od_a2, od_g1=od_g1, od_g2=od_g2, od_k_k=od_k_k,
                  od_k_a=od_k_a, od_r_k=od_r_k, od_lnx_w=od_lnx_w, od_lnx_b=od_lnx_b, od_w_out=od_w_out,
                  ca_w_q=ca_w_q, ca_w_kv=ca_w_kv, ca_w_out=ca_w_out, moe_router=moe_router,
                  moe_w_in=moe_w_in, moe_w_out=moe_w_out, ln_w=ln_w, ln_b=ln_b)
    for name in _MATMUL_WEIGHTS:
        params[name] = params[name].astype(BF16)
    y_prompt = _trunk(x_prompt, mem_prompt, params)
    y_sample = _trunk(x_sample, mem_sample, params)
    return (y_prompt, y_sample)
```

```python
import functools
import math

import jax
import jax.numpy as jnp
from jax import lax
from jax.experimental import pallas as pl
from jax.experimental.pallas import tpu as pltpu

F32 = jnp.float32
BF16 = jnp.bfloat16

D_MODEL = 1024
DEPTH = 4
D_A = D_MODEL // 2
HEAD_A = 128
H_A = D_A // HEAD_A
CHUNK_A = 32
D_B = D_MODEL // 2
HEAD_B = 128
H_B = D_B // HEAD_B
CHUNK_B = 128
ROPE_BASE = 10000.0
HEAD_C = 64
H_C = D_MODEL // HEAD_C
LNX_EPS = 64e-5
H_CA = 4
HEAD_CA = D_MODEL // H_CA
N_EXPERTS = 16
D_EXPERT = 2048
CAP_FACTOR = 2
DN_ALPHA = (2.0 * DEPTH) ** 0.25
LN_EPS = 1e-5

VMEM_LIMIT = 48 * 1024 * 1024


def _cparams(*sem):
    return pltpu.CompilerParams(dimension_semantics=sem, vmem_limit_bytes=VMEM_LIMIT)


def _mm_kernel(x_ref, w_ref, o_ref):
    o_ref[...] = jnp.dot(x_ref[...].astype(BF16), w_ref[...],
                         preferred_element_type=F32).astype(o_ref.dtype)


def _pick(n, pref):
    t = min(n, pref)
    while n % t:
        t //= 2
    return t


def pmm(x, w, tm=1024, tn=512):
    M, K = x.shape
    N = w.shape[1]
    tm = _pick(M, tm)
    tn = _pick(N, tn) if N % 128 == 0 else N
    return pl.pallas_call(
        _mm_kernel,
        grid=(M // tm, N // tn),
        in_specs=[pl.BlockSpec((tm, K), lambda i, j: (i, 0)),
                  pl.BlockSpec((K, tn), lambda i, j: (0, j))],
        out_specs=pl.BlockSpec((tm, tn), lambda i, j: (i, j)),
        out_shape=jax.ShapeDtypeStruct((M, N), F32),
        compiler_params=_cparams("parallel", "parallel"),
        name="dense_proj",
    )(x, w)


def _mm(x, w):
    lead = x.shape[:-1]
    return pmm(x.reshape(-1, x.shape[-1]), w).reshape(lead + (w.shape[1],))


def _ca_kernel(x_ref, wq_ref, k_ref, v_ref, wo_ref, ln_ref, o_ref):
    scale = HEAD_CA ** -0.5
    x = x_ref[...]
    q_all = jnp.dot(x.astype(BF16), wq_ref[...], preferred_element_type=F32).astype(BF16)
    outs = []
    for h in range(H_CA):
        sl = slice(h * HEAD_CA, (h + 1) * HEAD_CA)
        k = k_ref[0, :, sl].astype(BF16)
        v = v_ref[0, :, sl].astype(BF16)
        s = lax.dot_general(q_all[:, sl], k, (((1,), (1,)), ((), ())), preferred_element_type=F32) * scale
        m = jnp.max(s, axis=-1, keepdims=True)
        e = jnp.exp(s - m)
        p = e / jnp.sum(e, axis=-1, keepdims=True)
        outs.append(jnp.dot(p.astype(BF16), v, preferred_element_type=F32).astype(BF16))
    h_ = jnp.dot(jnp.concatenate(outs, axis=1), wo_ref[...], preferred_element_type=F32)
    y = DN_ALPHA * x + h_
    mu = jnp.mean(y, axis=-1, keepdims=True)
    yc = y - mu
    var = jnp.mean(yc * yc, axis=-1, keepdims=True)
    o_ref[...] = yc * lax.rsqrt(var + LN_EPS) * ln_ref[0:1, :] + ln_ref[1:2, :]


def cross_attn_layer(xf, T, kv, w_q, w_out, ln, tq=512):
    n, D = xf.shape
    M = kv.shape[1]
    tq = _pick(T, tq)
    per = T // tq
    return pl.pallas_call(
        _ca_kernel,
        grid=(n // tq,),
        in_specs=[pl.BlockSpec((tq, D), lambda i: (i, 0)),
                  pl.BlockSpec((D, D), lambda i: (0, 0)),
                  pl.BlockSpec((1, M, D), lambda i: (i // per, 0, 0)),
                  pl.BlockSpec((1, M, D), lambda i: (i // per, 0, 1)),
                  pl.BlockSpec((D, D), lambda i: (0, 0)),
                  pl.BlockSpec((2, D), lambda i: (0, 0))],
        out_specs=pl.BlockSpec((tq, D), lambda i: (i, 0)),
        out_shape=jax.ShapeDtypeStruct((n, D), F32),
        compiler_params=_cparams("parallel"),
        name="cross_attn",
    )(xf, w_q, kv, kv, w_out, ln)


MOE_CHUNK = 2048


def _moe_kernel(x_ref, wg_ref, wu_ref, wo_ref, gate_ref, o_ref, *, nf, tm):
    f = pl.program_id(2)
    rows = pl.ds(pl.multiple_of(pl.program_id(3) * tm, tm), tm)
    x = x_ref[0].astype(BF16)
    hg = jnp.dot(x, wg_ref[0].astype(BF16), preferred_element_type=F32)
    hu = jnp.dot(x, wu_ref[0].astype(BF16), preferred_element_type=F32)
    a = (hg * jax.nn.sigmoid(hg) * hu).astype(BF16)
    part = jnp.dot(a, wo_ref[0].astype(BF16), preferred_element_type=F32)

    @pl.when(f == 0)
    def _():
        o_ref[0, rows, :] = part

    @pl.when(jnp.logical_and(f > 0, f < nf - 1))
    def _():
        o_ref[0, rows, :] += part

    @pl.when(f == nf - 1)
    def _():
        o_ref[0, rows, :] = (o_ref[0, rows, :] + part) * gate_ref[0]


def moe_ffn(xe, w_in, w_out, gate, layer, tm=1024, tf=512):
    E, cap, D = xe.shape
    Fh = w_out.shape[2]
    ch = min(cap, MOE_CHUNK)
    tm = _pick(ch, tm)
    nf = Fh // tf
    nm = ch // tm
    return pl.pallas_call(
        functools.partial(_moe_kernel, nf=nf, tm=tm),
        grid=(E, cap // ch, nf, nm),
        in_specs=[pl.BlockSpec((1, tm, D), lambda e, h, f, m: (e, h * nm + m, 0)),
                  pl.BlockSpec((None, 1, D, tf), lambda e, h, f, m: (layer, e, 0, f)),
                  pl.BlockSpec((None, 1, D, tf), lambda e, h, f, m: (layer, e, 0, nf + f)),
                  pl.BlockSpec((None, 1, tf, D), lambda e, h, f, m: (layer, e, f, 0)),
                  pl.BlockSpec((1, tm, 1), lambda e, h, f, m: (e, h * nm + m, 0))],
        out_specs=pl.BlockSpec((1, ch, D), lambda e, h, f, m: (e, h, 0)),
        out_shape=jax.ShapeDtypeStruct((E, cap, D), F32),
        compiler_params=_cparams("parallel", "parallel", "arbitrary", "arbitrary"),
        name="moe_ffn",
    )(xe, w_in, w_in, w_out, gate)


_NN = (((1,), (0,)), ((), ()))
_NT = (((1,), (1,)), ((), ()))
_TN = (((0,), (0,)), ((), ()))


def _bdot(a, b, dims=_NN):
    return lax.dot_general(a.astype(BF16), b.astype(BF16), dims, preferred_element_type=F32)


def _split3(x):
    hi = x.astype(BF16)
    r1 = x - hi.astype(F32)
    mid = r1.astype(BF16)
    lo = (r1 - mid.astype(F32)).astype(BF16)
    return hi, mid, lo


def _cumdot(cum, x):
    hi, mid, lo = _split3(x)
    return (lax.dot_general(cum, hi, _NN, preferred_element_type=F32)
            + lax.dot_general(cum, mid, _NN, preferred_element_type=F32)
            + lax.dot_general(cum, lo, _NN, preferred_element_type=F32))


RW_L = 64
RW_TT = 512
RW_W = 128


def _rwkv_kernel(rf, vf, kkf, lwf, kdf, bbf, rb, vb, kkb, lwb, kdb, bbb, of_ref, ob_ref,
                 mf_ref, mb_ref, om_ref, th_ref):
    L = RW_L
    nc = RW_TT // L
    S = 2 * L

    @pl.when(pl.program_id(2) == 0)
    def _():
        mf_ref[...] = jnp.zeros_like(mf_ref)
        mb_ref[...] = jnp.zeros_like(mb_ref)

    row = lax.broadcasted_iota(jnp.int32, (S, S), 0)
    col = lax.broadcasted_iota(jnp.int32, (S, S), 1)
    strict = {False: col < row, True: col > row}
    incl = {False: col <= row, True: col >= row}
    incl2 = {d: jnp.concatenate([incl[d], incl[d]], axis=1) for d in (False, True)}
    eye = (row == col).astype(F32)
    cr = lax.broadcasted_iota(jnp.int32, (L, L), 0)
    cc = lax.broadcasted_iota(jnp.int32, (L, L), 1)
    cum = {False: (cc <= cr).astype(BF16), True: (cc >= cr).astype(BF16)}
    lane = lax.broadcasted_iota(jnp.int32, (L, RW_W), 1)
    m0 = lane < HEAD_C

    def stack(x):
        return jnp.concatenate([jnp.where(m0, x, 0.0), jnp.where(m0, 0.0, x)], axis=0)

    refs = {False: (rf, vf, kkf, lwf, kdf, bbf), True: (rb, vb, kkb, lwb, kdb, bbb)}
    items = [(d, c) for c in range(nc) for d in (False, True)]
    ni = len(items)
    cs = range(ni)
    ld = lambda i, which: refs[items[i][0]][which][0, pl.ds(items[i][1] * L, L), :]
    lws = [ld(i, 3) for i in cs]
    gs = [_cumdot(cum[items[i][0]], lws[i]) for i in cs]
    gls = [(gs[i][0:1, :] if items[i][0] else gs[i][L - 1:L, :]) for i in cs]
    lhs, rhs_a, kbl, vss, rgs, ags = [], [], [], [], [], []
    for i in cs:
        g, gl = gs[i], gls[i]
        eg, eng, egl = jnp.exp(g), jnp.exp(-g), jnp.exp(gl - g)
        kd, bb = ld(i, 4), ld(i, 5)
        rg_s, ag_s = stack(ld(i, 0) * eg), stack(ld(i, 2) * jnp.exp(g - lws[i]))
        rgs.append(rg_s)
        ags.append(ag_s)
        lhs.append(jnp.concatenate([ag_s, rg_s], axis=0))
        rhs_a.append(jnp.concatenate([stack(kd * eng), stack(bb * eng)], axis=0))
        kbl.append(jnp.concatenate([stack(kd * egl), stack(bb * egl)], axis=0))
        vss.append(stack(ld(i, 1)))
    amats = [_bdot(lhs[i], rhs_a[i], _NT) for i in cs]
    aaks = [jnp.where(strict[items[i][0]], amats[i][0:S, 0:S], 0.0) for i in cs]
    ns = [jnp.where(strict[items[i][0]], amats[i][0:S, S:2 * S], 0.0) for i in cs]
    arkb = [jnp.where(incl2[items[i][0]], amats[i][S:2 * S, :], 0.0) for i in cs]
    ts = [eye + n for n in ns]
    ps = [_bdot(n, n) for n in ns]
    avs = [_bdot(aaks[i], vss[i]) for i in cs]
    for _ in range(4):
        pps = [_bdot(ps[i], jnp.concatenate([ps[i], ts[i]], axis=1)) for i in cs]
        ts = [ts[i] + pps[i][:, S:2 * S] for i in cs]
        ps = [pps[i][:, 0:S] for i in cs]
    ts = [ts[i] + _bdot(ps[i], ts[i]) for i in cs]
    w12s = [_bdot(ts[i], jnp.concatenate([ags[i], avs[i]], axis=1)) for i in cs]
    rhss = [jnp.concatenate([jnp.concatenate([jnp.zeros_like(vss[i]), vss[i]], axis=1), w12s[i]], axis=0) for i in cs]
    sts = [_bdot(kbl[i], rhss[i], _TN) for i in cs]
    ots = [_bdot(arkb[i], rhss[i]) for i in cs]
    for i in cs:
        phi = sts[i][:, 0:RW_W] + eye * jnp.exp(gls[i])
        om_ref[i] = jnp.concatenate([ots[i][:, 0:RW_W] + rgs[i], phi], axis=0)
        th_ref[i] = jnp.concatenate([ots[i][:, RW_W:2 * RW_W], sts[i][:, RW_W:2 * RW_W]], axis=0)

    m = {False: mf_ref[...], True: mb_ref[...]}
    outs = {False: of_ref, True: ob_ref}
    for j in range(nc):
        for d in (False, True):
            c = nc - 1 - j if d else j
            i = items.index((d, c))
            res = _bdot(om_ref[i], m[d]) + th_ref[i]
            outs[d][0, pl.ds(c * L, L), :] = res[0:L] + res[L:S]
            m[d] = res[S:2 * S]
    mf_ref[...] = m[False]
    mb_ref[...] = m[True]


def rwkv7_scan(r, v, kk, lw, kd, bb):
    B, T, D = r.shape
    nt = T // RW_TT
    f3 = pl.BlockSpec((1, RW_TT, RW_W), lambda b, h, i: (b, i, h))
    r3 = pl.BlockSpec((1, RW_TT, RW_W), lambda b, h, i: (b, nt - 1 - i, h))
    f4 = pl.BlockSpec((None, 1, RW_TT, RW_W), lambda b, h, i: (0, b, i, h))
    r4 = pl.BlockSpec((None, 1, RW_TT, RW_W), lambda b, h, i: (1, b, nt - 1 - i, h))
    nc = RW_TT // RW_L
    out = jax.ShapeDtypeStruct((B, T, D), F32)
    return pl.pallas_call(
        _rwkv_kernel,
        grid=(B, D // RW_W, nt),
        in_specs=[f3, f3, f3, f4, f4, f4, r3, r3, r3, r4, r4, r4],
        out_specs=[f3, r3],
        out_shape=[out, out],
        scratch_shapes=[pltpu.VMEM((RW_W, RW_W), F32), pltpu.VMEM((RW_W, RW_W), F32),
                        pltpu.VMEM((2 * nc, 4 * RW_L, RW_W), F32),
                        pltpu.VMEM((2 * nc, 4 * RW_L, RW_W), F32)],
        compiler_params=_cparams("parallel", "parallel", "arbitrary"),
        name="rwkv7",
    )(r, v, kk, lw, kd, bb, r, v, kk, lw, kd, bb)


HG_SUB = 16
HG_GRP = 128
HG_TT = 512


def _hgrn2_group(q_ref, v_ref, z_ref, lb_ref, o_ref, st_ref, g0, reverse):
    C, G = HG_SUB, HG_GRP
    log_lb, log_1mlb, one_mlb = lb_ref[0, 0:1, :], lb_ref[0, 1:2, :], lb_ref[0, 2:3, :]
    row = lax.broadcasted_iota(jnp.int32, (G, G), 0)
    col = lax.broadcasted_iota(jnp.int32, (G, G), 1)
    same = (row // C) == (col // C)
    cum = (same & ((col >= row) if reverse else (col <= row))).astype(BF16)
    sub_h = lax.broadcasted_iota(jnp.int32, (C // 2, HEAD_A), 0)
    rows = pl.ds(pl.multiple_of(g0 * G, G), G)
    z = z_ref[0, rows, :]
    a_q = q_ref[0, rows, :]
    vv = v_ref[0, rows, :]
    ls = jnp.minimum(z, 0.0) - jnp.log1p(jnp.exp(-jnp.abs(z)))
    t2 = log_1mlb + ls
    mx = jnp.maximum(log_lb, t2)
    lf = mx + jnp.log1p(jnp.exp(-jnp.abs(log_lb - t2)))
    kk = one_mlb * (1.0 / (1.0 + jnp.exp(z)))
    qq = a_q * (1.0 / (1.0 + jnp.exp(-a_q)))
    b = _cumdot(cum, lf)
    st = st_ref[...]
    order = range(G // C - 1, -1, -1) if reverse else range(G // C)
    outs = [None] * (G // C)
    for j in order:
        sl = slice(j * C, (j + 1) * C)
        bj, qj, kj, vj = b[sl], qq[sl], kk[sl], vv[sl]
        e_in = 0 if reverse else C - 1
        b_end = bj[e_in:e_in + 1, :]
        H = C // 2
        acc = [None, None]
        for s in range(C):
            bs, ks, vs = bj[s:s + 1, :], kj[s:s + 1, :], vj[s:s + 1, :]
            hs = s // H
            full = (hs - 1) if reverse else (hs + 1)
            for h in (0, 1):
                if h != hs and h != full:
                    continue
                rows_h = slice(h * H, (h + 1) * H)
                diff = bj[rows_h] - bs
                if h == hs:
                    keep = (sub_h <= s - hs * H) if reverse else (sub_h >= s - hs * H)
                    diff = jnp.where(keep, diff, -1e30)
                sc = jnp.sum(jnp.exp(diff) * (qj[rows_h] * ks), axis=-1, keepdims=True)
                acc[h] = sc * vs if acc[h] is None else acc[h] + sc * vs
        outs[j] = jnp.concatenate(acc, axis=0) + _bdot(qj * jnp.exp(bj), st, _NT)
        st = st * jnp.exp(b_end) + _bdot(vj, kj * jnp.exp(b_end - bj), _TN)
    st_ref[...] = st
    o_ref[0, rows, :] = jnp.concatenate(outs, axis=0)


def _hgrn2_kernel(qf_ref, vf_ref, zf_ref, qb_ref, vb_ref, zb_ref, lbf_ref, lbb_ref, of_ref, ob_ref, stf_ref, stb_ref):
    @pl.when(pl.program_id(2) == 0)
    def _():
        stf_ref[...] = jnp.zeros_like(stf_ref)
        stb_ref[...] = jnp.zeros_like(stb_ref)

    ng = HG_TT // HG_GRP

    def body(gi, carry):
        _hgrn2_group(qf_ref, vf_ref, zf_ref, lbf_ref, of_ref, stf_ref, gi, False)
        _hgrn2_group(qb_ref, vb_ref, zb_ref, lbb_ref, ob_ref, stb_ref, ng - 1 - gi, True)
        return carry

    lax.fori_loop(0, ng, body, 0)


def hgrn2_scan(proj, lbp):
    B, T, _ = proj.shape
    nt = T // HG_TT
    blk = (1, HG_TT, HEAD_A)
    fmap = lambda off: (lambda b, h, i: (b, i, off + h))
    rmap = lambda off: (lambda b, h, i: (b, nt - 1 - i, off + h))
    out = jax.ShapeDtypeStruct((B, T, D_A), F32)
    return pl.pallas_call(
        _hgrn2_kernel,
        grid=(B, H_A, nt),
        in_specs=[pl.BlockSpec(blk, fmap(0)), pl.BlockSpec(blk, fmap(H_A)), pl.BlockSpec(blk, fmap(2 * H_A)),
                  pl.BlockSpec(blk, rmap(0)), pl.BlockSpec(blk, rmap(H_A)), pl.BlockSpec(blk, rmap(3 * H_A)),
                  pl.BlockSpec((1, 3, HEAD_A), lambda b, h, i: (0, 0, h)),
                  pl.BlockSpec((1, 3, HEAD_A), lambda b, h, i: (1, 0, h))],
        out_specs=[pl.BlockSpec(blk, fmap(0)), pl.BlockSpec(blk, rmap(0))],
        out_shape=[out, out],
        scratch_shapes=[pltpu.VMEM((HEAD_A, HEAD_A), F32), pltpu.VMEM((HEAD_A, HEAD_A), F32)],
        compiler_params=_cparams("parallel", "parallel", "arbitrary"),
        name="hgrn2",
    )(proj, proj, proj, proj, proj, proj, lbp, lbp)


RT_TT = 256


def rope_tables(T):
    theta = 1.0 / jnp.power(ROPE_BASE, jnp.linspace(0.0, 1.0, HEAD_B // 2, dtype=F32))
    ang = jnp.arange(T, dtype=F32)[:, None] * theta[None, :]
    cos = jnp.repeat(jnp.cos(ang), 2, axis=1)
    sin = jnp.repeat(jnp.sin(ang), 2, axis=1)
    sign = jnp.where(jnp.arange(HEAD_B) % 2 == 0, -1.0, 1.0).astype(F32)
    return cos, sin * sign[None, :]


def _rope(t, cos, sin):
    lane = lax.broadcasted_iota(jnp.int32, t.shape, 1)
    nxt = pltpu.roll(t, t.shape[1] - 1, axis=1)
    prv = pltpu.roll(t, 1, axis=1)
    return t * cos + jnp.where(lane % 2 == 0, nxt, prv) * sin


def _ret_kernel(lg_ref, cosf_ref, sinf_ref, cosb_ref, sinb_ref, qf_ref, kf_ref, vf_ref, qb_ref, kb_ref, vb_ref,
                of_ref, ob_ref, stf_ref, stb_ref):
    Tt = RT_TT

    @pl.when(pl.program_id(2) == 0)
    def _():
        stf_ref[...] = jnp.zeros_like(stf_ref)
        stb_ref[...] = jnp.zeros_like(stb_ref)

    lgw = lg_ref[0]
    lg = lgw[:, 0:HEAD_B]
    pos = lax.broadcasted_iota(jnp.int32, (Tt, 1), 0).astype(F32)
    scale = HEAD_B ** -0.5

    cos, sin = cosf_ref[...], sinf_ref[...]
    q = _rope(qf_ref[0], cos, sin)
    k = _rope(kf_ref[0], cos, sin) * scale
    v = vf_ref[0]
    row = lax.broadcasted_iota(jnp.int32, (Tt, Tt), 0)
    col = lax.broadcasted_iota(jnp.int32, (Tt, Tt), 1)
    dmat = jnp.exp(lgw * jnp.abs(row - col).astype(F32))
    st = stf_ref[...]
    o = _bdot(_bdot(q, k, _NT) * dmat, v)
    of_ref[0] = o + _bdot(q * jnp.exp(lg * (pos + 1.0)), st)
    stf_ref[...] = st * jnp.exp(lg * Tt) + _bdot(k * jnp.exp(lg * (Tt - 1.0 - pos)), v, _TN)

    cos, sin = cosb_ref[...], sinb_ref[...]
    q = _rope(qb_ref[0], cos, sin)
    k = _rope(kb_ref[0], cos, sin) * scale
    st = stb_ref[...]
    ob_ref[0] = _bdot(q * jnp.exp(lg * (Tt - pos)), st)
    stb_ref[...] = st * jnp.exp(lg * Tt) + _bdot(k * jnp.exp(lg * pos), vb_ref[0], _TN)


def retention(proj):
    B, T, _ = proj.shape
    nt = T // RT_TT
    blk = (1, RT_TT, HEAD_B)
    qoff = 5 * D_A // HEAD_B
    koff, voff = qoff + H_B, qoff + 2 * H_B
    cos, sin = rope_tables(T)
    lgam = jnp.log1p(-jnp.power(2.0, -5.0 - jnp.arange(H_B, dtype=F32)))
    lgam = jnp.broadcast_to(lgam[:, None, None], (H_B, 1, RT_TT))
    lspec = pl.BlockSpec((1, 1, RT_TT), lambda b, h, i: (h, 0, 0))
    fwd = lambda i: i
    rev = lambda i: nt - 1 - i
    tab = lambda tix: pl.BlockSpec((RT_TT, HEAD_B), lambda b, h, i: (tix(i), 0))
    col = lambda tix, off: pl.BlockSpec(blk, lambda b, h, i: (b, tix(i), off + h))
    out = jax.ShapeDtypeStruct((B, T, D_B), F32)
    return pl.pallas_call(
        _ret_kernel, grid=(B, H_B, nt),
        in_specs=[lspec, tab(fwd), tab(fwd), tab(rev), tab(rev),
                  col(fwd, qoff), col(fwd, koff), col(fwd, voff), col(rev, qoff), col(rev, koff), col(rev, voff)],
        out_specs=[col(fwd, 0), col(rev, 0)],
        out_shape=[out, out],
        scratch_shapes=[pltpu.VMEM((HEAD_B, HEAD_B), F32), pltpu.VMEM((HEAD_B, HEAD_B), F32)],
        compiler_params=_cparams("parallel", "parallel", "arbitrary"), name="retention",
    )(lgam, cos, sin, cos, sin, proj, proj, proj, proj, proj, proj)


LANES = 128


def _head_ones():
    r = lax.broadcasted_iota(jnp.int32, (LANES, LANES), 0) // HEAD_C
    c = lax.broadcasted_iota(jnp.int32, (LANES, LANES), 1) // HEAD_C
    return (r == c).astype(BF16)


def _head_sum(x, bd):
    hi = x.astype(BF16)
    lo = (x - hi.astype(F32)).astype(BF16)
    return (lax.dot_general(hi, bd, _NN, preferred_element_type=F32)
            + lax.dot_general(lo, bd, _NN, preferred_element_type=F32))


def _layer_norm_rows(y, w, b):
    mu = jnp.mean(y, axis=-1, keepdims=True)
    yc = y - mu
    var = jnp.mean(yc * yc, axis=-1, keepdims=True)
    return yc * lax.rsqrt(var + LN_EPS) * w + b


def _full_spec(a):
    return pl.BlockSpec(a.shape, lambda i: (0,) * a.ndim)


def _odd_pre_kernel(x_ref, xp_ref, xn_ref, mu_ref, wrkv_ref, w1_ref, w2_ref, a1_ref, a2_ref, g1_ref, g2_ref,
                    vec_ref, r_ref, v_ref, kk_ref, g_ref, kd_ref, bb_ref, lw_ref, *, tiles_per_seq):
    D = D_MODEL
    x = x_ref[...]
    tm = x.shape[0]
    t_in_seq = pl.program_id(0) % tiles_per_seq
    prev_row = jnp.where(t_in_seq == 0, 0.0, xp_ref[0, 7:8, :])
    next_row = jnp.where(t_in_seq == tiles_per_seq - 1, 0.0, xn_ref[0, 0:1, :])
    rid = lax.broadcasted_iota(jnp.int32, x.shape, 0)
    x_prev = jnp.where(rid == 0, prev_row, pltpu.roll(x, 1, axis=0))
    x_next = jnp.where(rid == tm - 1, next_row, pltpu.roll(x, tm - 1, axis=0))
    xx = 0.5 * (x_prev + x_next) - x
    mix = lambda p: (x + xx * mu_ref[p:p + 1, :]).astype(BF16)
    r = jnp.dot(mix(0), wrkv_ref[0], preferred_element_type=F32)
    k = jnp.dot(mix(1), wrkv_ref[1], preferred_element_type=F32)
    v = jnp.dot(mix(2), wrkv_ref[2], preferred_element_type=F32)
    hw = jnp.tanh(jnp.dot(mix(3), w1_ref[...], preferred_element_type=F32))
    lora_w = _bdot(hw, w2_ref[...])
    lora_a = _bdot(jnp.dot(mix(4), a1_ref[...], preferred_element_type=F32), a2_ref[...])
    hg = jnp.dot(mix(5), g1_ref[...], preferred_element_type=F32)
    g = _bdot(1.0 / (1.0 + jnp.exp(-hg)), g2_ref[...])
    k_k, k_a = vec_ref[4:5, :], vec_ref[5:6, :]
    bd = _head_ones()
    kq = k * k_k
    kks = []
    for s in range(D // LANES):
        ks = kq[:, s * LANES:(s + 1) * LANES]
        nrm = jnp.sqrt(_head_sum(ks * ks, bd))
        kks.append(ks / jnp.maximum(nrm, 1e-12))
    kk = jnp.concatenate(kks, axis=1)
    r_ref[...] = r.astype(r_ref.dtype)
    v_ref[...] = v.astype(v_ref.dtype)
    kk_ref[...] = kk.astype(kk_ref.dtype)
    g_ref[...] = g.astype(g_ref.dtype)
    for n in range(2):
        u = vec_ref[n:n + 1, :] + lora_w[:, n * D:(n + 1) * D]
        lw_ref[n] = -math.exp(-0.5) / (1.0 + jnp.exp(-u))
        a = 1.0 / (1.0 + jnp.exp(-(vec_ref[2 + n:3 + n, :] + lora_a[:, n * D:(n + 1) * D])))
        kd_ref[n] = (k * (1.0 + (a - 1.0) * k_a)).astype(kd_ref.dtype)
        bb_ref[n] = (-(kk * a)).astype(bb_ref.dtype)


def odd_pre(x, T, mu, wrkv, w1c, w2bd, a1c, a2bd, g1, g2, vec, tm=256):
    n, D = x.shape
    row = pl.BlockSpec((tm, D), lambda i: (i, 0))
    x8 = x.reshape(n // 8, 8, D)
    prev8 = pl.BlockSpec((1, 8, D), lambda i: (jnp.maximum(i * (tm // 8) - 1, 0), 0, 0))
    next8 = pl.BlockSpec((1, 8, D), lambda i: (jnp.minimum((i + 1) * (tm // 8), n // 8 - 1), 0, 0))
    row2 = pl.BlockSpec((2, tm, D), lambda i: (0, i, 0))
    o1 = jax.ShapeDtypeStruct((n, D), BF16)
    o2 = jax.ShapeDtypeStruct((2, n, D), BF16)
    return pl.pallas_call(
        functools.partial(_odd_pre_kernel, tiles_per_seq=T // tm), grid=(n // tm,),
        in_specs=[row, prev8, next8] + [_full_spec(a) for a in (mu, wrkv, w1c, w2bd, a1c, a2bd, g1, g2, vec)],
        out_specs=[row, row, row, row, row2, row2, row2],
        out_shape=[o1, o1, o1, o1, o2, o2, jax.ShapeDtypeStruct((2, n, D), F32)],
        compiler_params=_cparams("parallel"), name="rwkv7_pre",
    )(x, x8, x8, mu, wrkv, w1c, w2bd, a1c, a2bd, g1, g2, vec)


def _odd_post_kernel(of_ref, ob_ref, r_ref, v_ref, g_ref, kd_ref, x_ref, vec_ref, wout_ref, o_ref):
    D = D_MODEL
    bd = _head_ones()
    lnx_w, lnx_b, r_k = vec_ref[0:1, :], vec_ref[1:2, :], vec_ref[2:3, :]
    o = of_ref[...] + ob_ref[...]
    rk = r_ref[...].astype(F32) * r_k * (kd_ref[0].astype(F32) + kd_ref[1].astype(F32))
    v = v_ref[...].astype(F32)
    ys = []
    for s in range(D // LANES):
        sl = slice(s * LANES, (s + 1) * LANES)
        os_ = o[:, sl]
        oc = os_ - _head_sum(os_, bd) * (1.0 / HEAD_C)
        var = _head_sum(oc * oc, bd) * (1.0 / HEAD_C)
        on = oc * lax.rsqrt(var + LNX_EPS) * lnx_w[:, sl] + lnx_b[:, sl]
        ys.append(on + _head_sum(rk[:, sl], bd) * v[:, sl])
    y = jnp.concatenate(ys, axis=1) * g_ref[...].astype(F32)
    h = jnp.dot(y.astype(BF16), wout_ref[...], preferred_element_type=F32)
    o_ref[...] = _layer_norm_rows(DN_ALPHA * x_ref[...] + h, vec_ref[3:4, :], vec_ref[4:5, :])


def odd_post(o_f, o_b, r, v, g, kd, x, vec, w_out, tm=256):
    n, D = x.shape
    row = pl.BlockSpec((tm, D), lambda i: (i, 0))
    row2 = pl.BlockSpec((2, tm, D), lambda i: (0, i, 0))
    return pl.pallas_call(
        _odd_post_kernel, grid=(n // tm,),
        in_specs=[row, row, row, row, row, row2, row, _full_spec(vec), _full_spec(w_out)],
        out_specs=row,
        out_shape=jax.ShapeDtypeStruct((n, D), F32),
        compiler_params=_cparams("parallel"), name="rwkv7_post",
    )(o_f, o_b, r, v, g, kd, x, vec, w_out)


def _proj_ln_kernel(a_ref, w_ref, x_ref, ln_ref, o_ref):
    h = jnp.dot(a_ref[...].astype(BF16), w_ref[...], preferred_element_type=F32)
    o_ref[...] = _layer_norm_rows(DN_ALPHA * x_ref[...] + h, ln_ref[0:1, :], ln_ref[1:2, :])


def proj_ln(a, w, x, ln, tm=512):
    n, K = a.shape
    D = w.shape[1]
    return pl.pallas_call(
        _proj_ln_kernel, grid=(n // tm,),
        in_specs=[pl.BlockSpec((tm, K), lambda i: (i, 0)), pl.BlockSpec((K, D), lambda i: (0, 0)),
                  pl.BlockSpec((tm, D), lambda i: (i, 0)), pl.BlockSpec((2, D), lambda i: (0, 0))],
        out_specs=pl.BlockSpec((tm, D), lambda i: (i, 0)),
        out_shape=jax.ShapeDtypeStruct((n, D), F32),
        compiler_params=_cparams("parallel"), name="proj_res_ln",
    )(a, w, x, ln)


def _add_ln_kernel(h_ref, x_ref, ln_ref, o_ref):
    o_ref[...] = _layer_norm_rows(DN_ALPHA * x_ref[...] + h_ref[...], ln_ref[0:1, :], ln_ref[1:2, :])


def add_ln(h, x, ln, tm=512):
    n, D = x.shape
    row = pl.BlockSpec((tm, D), lambda i: (i, 0))
    return pl.pallas_call(
        _add_ln_kernel, grid=(n // tm,),
        in_specs=[row, row, pl.BlockSpec((2, D), lambda i: (0, 0))],
        out_specs=row, out_shape=jax.ShapeDtypeStruct((n, D), F32),
        compiler_params=_cparams("parallel"), name="res_ln",
    )(h, x, ln)


def _even_post_kernel(oaf_ref, oab_ref, obf_ref, obb_ref, ga_ref, gb_ref, x_ref, vec_ref, wout_ref, o_ref):
    silu = lambda t: t * (1.0 / (1.0 + jnp.exp(-t)))
    oa = oaf_ref[...] + oab_ref[...]
    ob = obf_ref[...] + obb_ref[...]
    ga, gb = ga_ref[...], gb_ref[...]
    na, nb = vec_ref[0:1, :], vec_ref[1:2, :]
    ys = []
    for h in range(H_A):
        sl = slice(h * HEAD_A, (h + 1) * HEAD_A)
        t = oa[:, sl]
        t = t * lax.rsqrt(jnp.mean(t * t, axis=-1, keepdims=True) + 1e-6)
        ys.append(t * na[:, sl] * silu(ga[:, sl]))
    for h in range(H_B):
        sl = slice(h * HEAD_B, (h + 1) * HEAD_B)
        t = ob[:, sl]
        t = t - jnp.mean(t, axis=-1, keepdims=True)
        t = t * lax.rsqrt(jnp.mean(t * t, axis=-1, keepdims=True) + 1e-6)
        ys.append(t * nb[:, sl] * silu(gb[:, sl]))
    y = jnp.concatenate(ys, axis=1)
    h_ = jnp.dot(y.astype(BF16), wout_ref[...], preferred_element_type=F32)
    o_ref[...] = _layer_norm_rows(DN_ALPHA * x_ref[...] + h_, vec_ref[2:3, :], vec_ref[3:4, :])


def even_post(oa_f, oa_b, ob_f, ob_b, proj, x, vec, w_out, tm=256):
    n, D = x.shape
    half = pl.BlockSpec((tm, D_A), lambda i: (i, 0))
    row = pl.BlockSpec((tm, D), lambda i: (i, 0))
    return pl.pallas_call(
        _even_post_kernel, grid=(n // tm,),
        in_specs=[half, half, half, half, pl.BlockSpec((tm, D_A), lambda i: (i, 4)),
                  pl.BlockSpec((tm, D_B), lambda i: (i, 8)), row,
                  pl.BlockSpec(vec.shape, lambda i: (0, 0)), pl.BlockSpec(w_out.shape, lambda i: (0, 0))],
        out_specs=row,
        out_shape=jax.ShapeDtypeStruct((n, D), F32),
        compiler_params=_cparams("parallel"), name="even_post",
    )(oa_f, oa_b, ob_f, ob_b, proj, proj, x, vec, w_out)


def _even_layer(xf, Bn, T, w_in, lb, norm_a, norm_b, w_out, ln):
    n, D = xf.shape
    proj = pmm(xf, w_in).reshape(Bn, T, -1)
    lbp = jnp.stack([jnp.log(lb), jnp.log1p(-lb), 1.0 - lb], axis=1)
    oa_f, oa_b = hgrn2_scan(proj, lbp)
    ob_f, ob_b = retention(proj)
    z = jnp.zeros((D - D_A,), F32)
    vec = jnp.concatenate([jnp.stack([jnp.concatenate([norm_a, z]), jnp.concatenate([norm_b, z])]), ln,
                           jnp.zeros((4, D), F32)], 0)
    fl = lambda t: t.reshape(n, -1)
    return even_post(fl(oa_f), fl(oa_b), fl(ob_f), fl(ob_b), fl(proj), xf, vec, w_out)


def _odd_weights(p, j):
    w1, w2, a1, a2 = p['od_w1'][j], p['od_w2'][j], p['od_a1'][j], p['od_a2'][j]
    z = jnp.zeros_like(w2[0])
    w2bd = jnp.concatenate([jnp.concatenate([w2[0], z], 1), jnp.concatenate([z, w2[1]], 1)], 0)
    a2bd = jnp.concatenate([jnp.concatenate([a2[0], z], 1), jnp.concatenate([z, a2[1]], 1)], 0)
    w0, a0 = p['od_w0'][j], p['od_a0'][j]
    zero = jnp.zeros_like(w0[0])
    vec_pre = jnp.stack([w0[0], w0[1], a0[0], a0[1], p['od_k_k'][j], p['od_k_a'][j], zero, zero])
    vec_post = jnp.stack([p['od_lnx_w'][j], p['od_lnx_b'][j], p['od_r_k'][j]])
    pre = (p['od_mu'][j], p['od_w_rkv'][j], jnp.concatenate([w1[0], w1[1]], 1), w2bd,
           jnp.concatenate([a1[0], a1[1]], 1), a2bd, p['od_g1'][j], p['od_g2'][j], vec_pre)
    return pre, vec_post


def _odd_layer(xf, Bn, T, pre, vec_post, w_out, ln):
    D = xf.shape[-1]
    r, v, kk, g, kd, bb, lw = odd_pre(xf, T, *pre)
    sh = lambda t: t.reshape(Bn, T, D)
    sh2 = lambda t: t.reshape(2, Bn, T, D)
    o_f, o_b = rwkv7_scan(sh(r), sh(v), sh(kk), sh2(lw), sh2(kd), sh2(bb))
    vec = jnp.concatenate([vec_post, ln, jnp.zeros((3, D), F32)], 0)
    return odd_post(o_f.reshape(-1, D), o_b.reshape(-1, D), r, v, g, kd, xf, vec, w_out)


def _cross_attn_layer(xf, Bn, T, mem, w_q, w_kv, w_out, ln):
    return cross_attn_layer(xf, T, _mm(mem, w_kv), w_q, w_out, ln)


def _moe_layer(xt, w_router, w_in, w_out, layer, ln):
    n, D = xt.shape
    cap = (CAP_FACTOR * n) // N_EXPERTS
    aff = jax.nn.softmax(pmm(xt, w_router), axis=-1)
    gate, idx = lax.top_k(aff.T, cap)
    xe = xt[idx]
    ye = moe_ffn(xe, w_in, w_out, gate[..., None], layer)
    y = jnp.zeros_like(xt).at[idx.reshape(-1)].add(ye.reshape(-1, D))
    return add_ln(y, xt, ln)


def _trunk(x, mem, p):
    Bn, T, D = x.shape
    xf = x.reshape(Bn * T, D)
    p_lb = jax.nn.softmax(p['ev_lb_logits'], axis=1)
    lb_all = jnp.clip(jnp.cumsum(p_lb, axis=1) - p_lb[:, :1], 0.0, 1.0)
    for layer in range(DEPTH):
        j = layer // 2
        ln = lambda s: jnp.stack([p['ln_w'][layer, s], p['ln_b'][layer, s]])
        if layer % 2 == 0:
            xf = _even_layer(xf, Bn, T, p['ev_w_in'][j], lb_all[:, j], p['ev_norm_a'][j], p['ev_norm_b'][j],
                             p['ev_w_out'][j], ln(0))
        else:
            pre, vec_post = _odd_weights(p, j)
            xf = _odd_layer(xf, Bn, T, pre, vec_post, p['od_w_out'][j], ln(0))
        xf = _cross_attn_layer(xf, Bn, T, mem, p['ca_w_q'][layer], p['ca_w_kv'][layer], p['ca_w_out'][layer], ln(1))
        xf = _moe_layer(xf, p['moe_router'][layer], p['moe_w_in'], p['moe_w_out'], layer, ln(2))
    return xf.reshape(Bn, T, D)


_MATMUL_WEIGHTS = ('ev_w_in', 'ev_w_out', 'od_w_rkv', 'od_w1', 'od_w2', 'od_a1', 'od_a2', 'od_g1', 'od_g2',
                   'od_w_out', 'ca_w_q', 'ca_w_kv', 'ca_w_out', 'moe_router')


def kernel(x_prompt, x_sample, mem_prompt, mem_sample, ev_w_in, ev_lb_logits, ev_norm_a, ev_norm_b, ev_w_out, od_mu, od_w_rkv, od_w0, od_w1, od_w2, od_a0, od_a1, od_a2, od_g1, od_g2, od_k_k, od_k_a, od_r_k, od_lnx_w, od_lnx_b, od_w_out, ca_w_q, ca_w_kv, ca_w_out, moe_router, moe_w_in, moe_w_out, ln_w, ln_b):
    params = dict(ev_w_in=ev_w_in, ev_lb_logits=ev_lb_logits, ev_norm_a=ev_norm_a, ev_norm_b=ev_norm_b,
                  ev_w_out=ev_w_out, od_mu=od_mu, od_w_rkv=od_w_rkv, od_w0=od_w0, od_w1=od_w1, od_w2=od_w2,
                  od_a0=od_a0, od_a1=od_a1, od_a2=od_a2, od_g1=od_g1, od_g2=od_g2, od_k_k=od_k_k,
                  od_k_a=od_k_a, od_r_k=od_r_k, od_lnx_w=od_lnx_w, od_lnx_b=od_lnx_b, od_w_out=od_w_out,
                  ca_w_q=ca_w_q, ca_w_kv=ca_w_kv, ca_w_out=ca_w_out, moe_router=moe_router,
                  moe_w_in=moe_w_in, moe_w_out=moe_w_out, ln_w=ln_w, ln_b=ln_b)
    for name in _MATMUL_WEIGHTS:
        params[name] = params[name].astype(BF16)
    y_prompt = _trunk(x_prompt, mem_prompt, params)
    y_sample = _trunk(x_sample, mem_sample, params)
    return (y_prompt, y_sample)
```

```python
import functools
import math

import jax
import jax.numpy as jnp
from jax import lax
from jax.experimental import pallas as pl
from jax.experimental.pallas import tpu as pltpu

F32 = jnp.float32
BF16 = jnp.bfloat16

D_MODEL = 1024
DEPTH = 4
D_A = D_MODEL // 2
HEAD_A = 128
H_A = D_A // HEAD_A
CHUNK_A = 32
D_B = D_MODEL // 2
HEAD_B = 128
H_B = D_B // HEAD_B
CHUNK_B = 128
ROPE_BASE = 10000.0
HEAD_C = 64
H_C = D_MODEL // HEAD_C
LNX_EPS = 64e-5
H_CA = 4
HEAD_CA = D_MODEL // H_CA
N_EXPERTS = 16
D_EXPERT = 2048
CAP_FACTOR = 2
DN_ALPHA = (2.0 * DEPTH) ** 0.25
LN_EPS = 1e-5

VMEM_LIMIT = 48 * 1024 * 1024


def _cparams(*sem):
    return pltpu.CompilerParams(dimension_semantics=sem, vmem_limit_bytes=VMEM_LIMIT)


def _mm_kernel(x_ref, w_ref, o_ref):
    o_ref[...] = jnp.dot(x_ref[...].astype(BF16), w_ref[...],
                         preferred_element_type=F32).astype(o_ref.dtype)


def _pick(n, pref):
    t = min(n, pref)
    while n % t:
        t //= 2
    return t


def pmm(x, w, tm=1024, tn=512):
    M, K = x.shape
    N = w.shape[1]
    tm = _pick(M, tm)
    tn = _pick(N, tn) if N % 128 == 0 else N
    return pl.pallas_call(
        _mm_kernel,
        grid=(M // tm, N // tn),
        in_specs=[pl.BlockSpec((tm, K), lambda i, j: (i, 0)),
                  pl.BlockSpec((K, tn), lambda i, j: (0, j))],
        out_specs=pl.BlockSpec((tm, tn), lambda i, j: (i, j)),
        out_shape=jax.ShapeDtypeStruct((M, N), F32),
        compiler_params=_cparams("parallel", "parallel"),
        name="dense_proj",
    )(x, w)


def _mm(x, w):
    lead = x.shape[:-1]
    return pmm(x.reshape(-1, x.shape[-1]), w).reshape(lead + (w.shape[1],))


def _ca_kernel(x_ref, wq_ref, k_ref, v_ref, wo_ref, ln_ref, o_ref):
    scale = HEAD_CA ** -0.5
    x = x_ref[...]
    q_all = jnp.dot(x.astype(BF16), wq_ref[...], preferred_element_type=F32).astype(BF16)
    outs = []
    for h in range(H_CA):
        sl = slice(h * HEAD_CA, (h + 1) * HEAD_CA)
        k = k_ref[0, :, sl].astype(BF16)
        v = v_ref[0, :, sl].astype(BF16)
        s = lax.dot_general(q_all[:, sl], k, (((1,), (1,)), ((), ())), preferred_element_type=F32) * scale
        m = jnp.max(s, axis=-1, keepdims=True)
        e = jnp.exp(s - m)
        p = e / jnp.sum(e, axis=-1, keepdims=True)
        outs.append(jnp.dot(p.astype(BF16), v, preferred_element_type=F32).astype(BF16))
    h_ = jnp.dot(jnp.concatenate(outs, axis=1), wo_ref[...], preferred_element_type=F32)
    y = DN_ALPHA * x + h_
    mu = jnp.mean(y, axis=-1, keepdims=True)
    yc = y - mu
    var = jnp.mean(yc * yc, axis=-1, keepdims=True)
    o_ref[...] = yc * lax.rsqrt(var + LN_EPS) * ln_ref[0:1, :] + ln_ref[1:2, :]


def cross_attn_layer(xf, T, kv, w_q, w_out, ln, tq=1024):
    n, D = xf.shape
    M = kv.shape[1]
    tq = _pick(T, tq)
    per = T // tq
    return pl.pallas_call(
        _ca_kernel,
        grid=(n // tq,),
        in_specs=[pl.BlockSpec((tq, D), lambda i: (i, 0)),
                  pl.BlockSpec((D, D), lambda i: (0, 0)),
                  pl.BlockSpec((1, M, D), lambda i: (i // per, 0, 0)),
                  pl.BlockSpec((1, M, D), lambda i: (i // per, 0, 1)),
                  pl.BlockSpec((D, D), lambda i: (0, 0)),
                  pl.BlockSpec((2, D), lambda i: (0, 0))],
        out_specs=pl.BlockSpec((tq, D), lambda i: (i, 0)),
        out_shape=jax.ShapeDtypeStruct((n, D), F32),
        compiler_params=_cparams("parallel"),
        name="cross_attn",
    )(xf, w_q, kv, kv, w_out, ln)


MOE_CHUNK = 2048


def _moe_kernel(x_ref, wg_ref, wu_ref, wo_ref, gate_ref, o_ref, *, nf, tm):
    f = pl.program_id(2)
    rows = pl.ds(pl.multiple_of(pl.program_id(3) * tm, tm), tm)
    x = x_ref[0].astype(BF16)
    hg = jnp.dot(x, wg_ref[0].astype(BF16), preferred_element_type=F32)
    hu = jnp.dot(x, wu_ref[0].astype(BF16), preferred_element_type=F32)
    a = (hg * jax.nn.sigmoid(hg) * hu).astype(BF16)
    part = jnp.dot(a, wo_ref[0].astype(BF16), preferred_element_type=F32)

    @pl.when(f == 0)
    def _():
        o_ref[0, rows, :] = part

    @pl.when(jnp.logical_and(f > 0, f < nf - 1))
    def _():
        o_ref[0, rows, :] += part

    @pl.when(f == nf - 1)
    def _():
        o_ref[0, rows, :] = (o_ref[0, rows, :] + part) * gate_ref[0]


def moe_ffn(xe, w_in, w_out, gate, layer, tm=1024, tf=512):
    E, cap, D = xe.shape
    Fh = w_out.shape[2]
    ch = min(cap, MOE_CHUNK)
    tm = _pick(ch, tm)
    nf = Fh // tf
    nm = ch // tm
    return pl.pallas_call(
        functools.partial(_moe_kernel, nf=nf, tm=tm),
        grid=(E, cap // ch, nf, nm),
        in_specs=[pl.BlockSpec((1, tm, D), lambda e, h, f, m: (e, h * nm + m, 0)),
                  pl.BlockSpec((None, 1, D, tf), lambda e, h, f, m: (layer, e, 0, f)),
                  pl.BlockSpec((None, 1, D, tf), lambda e, h, f, m: (layer, e, 0, nf + f)),
                  pl.BlockSpec((None, 1, tf, D), lambda e, h, f, m: (layer, e, f, 0)),
                  pl.BlockSpec((1, tm, 1), lambda e, h, f, m: (e, h * nm + m, 0))],
        out_specs=pl.BlockSpec((1, ch, D), lambda e, h, f, m: (e, h, 0)),
        out_shape=jax.ShapeDtypeStruct((E, cap, D), F32),
        compiler_params=_cparams("parallel", "parallel", "arbitrary", "arbitrary"),
        name="moe_ffn",
    )(xe, w_in, w_in, w_out, gate)


_NN = (((1,), (0,)), ((), ()))
_NT = (((1,), (1,)), ((), ()))
_TN = (((0,), (0,)), ((), ()))


def _bdot(a, b, dims=_NN):
    return lax.dot_general(a.astype(BF16), b.astype(BF16), dims, preferred_element_type=F32)


def _split3(x):
    hi = x.astype(BF16)
    r1 = x - hi.astype(F32)
    mid = r1.astype(BF16)
    lo = (r1 - mid.astype(F32)).astype(BF16)
    return hi, mid, lo


def _cumdot(cum, x):
    hi, mid, lo = _split3(x)
    return (lax.dot_general(cum, hi, _NN, preferred_element_type=F32)
            + lax.dot_general(cum, mid, _NN, preferred_element_type=F32)
            + lax.dot_general(cum, lo, _NN, preferred_element_type=F32))


RW_L = 64
RW_TT = 512
RW_W = 128


def _rwkv_kernel(rf, vf, kkf, lwf, kdf, bbf, rb, vb, kkb, lwb, kdb, bbb, of_ref, ob_ref,
                 mf_ref, mb_ref, om_ref, th_ref):
    L = RW_L
    nc = RW_TT // L
    S = 2 * L

    @pl.when(pl.program_id(2) == 0)
    def _():
        mf_ref[...] = jnp.zeros_like(mf_ref)
        mb_ref[...] = jnp.zeros_like(mb_ref)

    row = lax.broadcasted_iota(jnp.int32, (S, S), 0)
    col = lax.broadcasted_iota(jnp.int32, (S, S), 1)
    strict = {False: col < row, True: col > row}
    incl = {False: col <= row, True: col >= row}
    incl2 = {d: jnp.concatenate([incl[d], incl[d]], axis=1) for d in (False, True)}
    eye = (row == col).astype(F32)
    cr = lax.broadcasted_iota(jnp.int32, (L, L), 0)
    cc = lax.broadcasted_iota(jnp.int32, (L, L), 1)
    cum = {False: (cc <= cr).astype(BF16), True: (cc >= cr).astype(BF16)}
    lane = lax.broadcasted_iota(jnp.int32, (L, RW_W), 1)
    m0 = lane < HEAD_C

    def stack(x):
        return jnp.concatenate([jnp.where(m0, x, 0.0), jnp.where(m0, 0.0, x)], axis=0)

    refs = {False: (rf, vf, kkf, lwf, kdf, bbf), True: (rb, vb, kkb, lwb, kdb, bbb)}
    items = [(d, c) for c in range(nc) for d in (False, True)]
    ni = len(items)
    cs = range(ni)
    ld = lambda i, which: refs[items[i][0]][which][0, pl.ds(items[i][1] * L, L), :]
    lws = [ld(i, 3) for i in cs]
    gs = [_cumdot(cum[items[i][0]], lws[i]) for i in cs]
    gls = [(gs[i][0:1, :] if items[i][0] else gs[i][L - 1:L, :]) for i in cs]
    lhs, rhs_a, kbl, vss, rgs, ags = [], [], [], [], [], []
    for i in cs:
        g, gl = gs[i], gls[i]
        eg, eng, egl = jnp.exp(g), jnp.exp(-g), jnp.exp(gl - g)
        kd, bb = ld(i, 4), ld(i, 5)
        rg_s, ag_s = stack(ld(i, 0) * eg), stack(ld(i, 2) * jnp.exp(g - lws[i]))
        rgs.append(rg_s)
        ags.append(ag_s)
        lhs.append(jnp.concatenate([ag_s, rg_s], axis=0))
        rhs_a.append(jnp.concatenate([stack(kd * eng), stack(bb * eng)], axis=0))
        kbl.append(jnp.concatenate([stack(kd * egl), stack(bb * egl)], axis=0))
        vss.append(stack(ld(i, 1)))
    amats = [_bdot(lhs[i], rhs_a[i], _NT) for i in cs]
    aaks = [jnp.where(strict[items[i][0]], amats[i][0:S, 0:S], 0.0) for i in cs]
    ns = [jnp.where(strict[items[i][0]], amats[i][0:S, S:2 * S], 0.0) for i in cs]
    arkb = [jnp.where(incl2[items[i][0]], amats[i][S:2 * S, :], 0.0) for i in cs]
    ts = [eye + n for n in ns]
    ps = [_bdot(n, n) for n in ns]
    avs = [_bdot(aaks[i], vss[i]) for i in cs]
    for _ in range(4):
        pps = [_bdot(ps[i], jnp.concatenate([ps[i], ts[i]], axis=1)) for i in cs]
        ts = [ts[i] + pps[i][:, S:2 * S] for i in cs]
        ps = [pps[i][:, 0:S] for i in cs]
    ts = [ts[i] + _bdot(ps[i], ts[i]) for i in cs]
    w12s = [_bdot(ts[i], jnp.concatenate([ags[i], avs[i]], axis=1)) for i in cs]
    rhss = [jnp.concatenate([jnp.concatenate([jnp.zeros_like(vss[i]), vss[i]], axis=1), w12s[i]], axis=0) for i in cs]
    sts = [_bdot(kbl[i], rhss[i], _TN) for i in cs]
    ots = [_bdot(arkb[i], rhss[i]) for i in cs]
    for i in cs:
        phi = sts[i][:, 0:RW_W] + eye * jnp.exp(gls[i])
        om_ref[i] = jnp.concatenate([ots[i][:, 0:RW_W] + rgs[i], phi], axis=0)
        th_ref[i] = jnp.concatenate([ots[i][:, RW_W:2 * RW_W], sts[i][:, RW_W:2 * RW_W]], axis=0)

    m = {False: mf_ref[...], True: mb_ref[...]}
    outs = {False: of_ref, True: ob_ref}
    for j in range(nc):
        for d in (False, True):
            c = nc - 1 - j if d else j
            i = items.index((d, c))
            res = _bdot(om_ref[i], m[d]) + th_ref[i]
            outs[d][0, pl.ds(c * L, L), :] = res[0:L] + res[L:S]
            m[d] = res[S:2 * S]
    mf_ref[...] = m[False]
    mb_ref[...] = m[True]


def rwkv7_scan(r, v, kk, lw, kd, bb):
    B, T, D = r.shape
    nt = T // RW_TT
    f3 = pl.BlockSpec((1, RW_TT, RW_W), lambda b, h, i: (b, i, h))
    r3 = pl.BlockSpec((1, RW_TT, RW_W), lambda b, h, i: (b, nt - 1 - i, h))
    f4 = pl.BlockSpec((None, 1, RW_TT, RW_W), lambda b, h, i: (0, b, i, h))
    r4 = pl.BlockSpec((None, 1, RW_TT, RW_W), lambda b, h, i: (1, b, nt - 1 - i, h))
    nc = RW_TT // RW_L
    out = jax.ShapeDtypeStruct((B, T, D), F32)
    return pl.pallas_call(
        _rwkv_kernel,
        grid=(B, D // RW_W, nt),
        in_specs=[f3, f3, f3, f4, f4, f4, r3, r3, r3, r4, r4, r4],
        out_specs=[f3, r3],
        out_shape=[out, out],
        scratch_shapes=[pltpu.VMEM((RW_W, RW_W), F32), pltpu.VMEM((RW_W, RW_W), F32),
                        pltpu.VMEM((2 * nc, 4 * RW_L, RW_W), F32),
                        pltpu.VMEM((2 * nc, 4 * RW_L, RW_W), F32)],
        compiler_params=_cparams("parallel", "parallel", "arbitrary"),
        name="rwkv7",
    )(r, v, kk, lw, kd, bb, r, v, kk, lw, kd, bb)


HG_SUB = 16
HG_GRP = 128
HG_TT = 1024


def _hgrn2_group(q_ref, v_ref, z_ref, lb_ref, o_ref, st_ref, g0, reverse):
    C, G = HG_SUB, HG_GRP
    log_lb, log_1mlb, one_mlb = lb_ref[0, 0:1, :], lb_ref[0, 1:2, :], lb_ref[0, 2:3, :]
    row = lax.broadcasted_iota(jnp.int32, (G, G), 0)
    col = lax.broadcasted_iota(jnp.int32, (G, G), 1)
    same = (row // C) == (col // C)
    cum = (same & ((col >= row) if reverse else (col <= row))).astype(BF16)
    sub_h = lax.broadcasted_iota(jnp.int32, (C // 2, HEAD_A), 0)
    rows = pl.ds(pl.multiple_of(g0 * G, G), G)
    z = z_ref[0, rows, :]
    a_q = q_ref[0, rows, :]
    vv = v_ref[0, rows, :]
    ls = jnp.minimum(z, 0.0) - jnp.log1p(jnp.exp(-jnp.abs(z)))
    t2 = log_1mlb + ls
    mx = jnp.maximum(log_lb, t2)
    lf = mx + jnp.log1p(jnp.exp(-jnp.abs(log_lb - t2)))
    kk = one_mlb * (1.0 / (1.0 + jnp.exp(z)))
    qq = a_q * (1.0 / (1.0 + jnp.exp(-a_q)))
    b = _cumdot(cum, lf)
    st = st_ref[...]
    order = range(G // C - 1, -1, -1) if reverse else range(G // C)
    outs = [None] * (G // C)
    for j in order:
        sl = slice(j * C, (j + 1) * C)
        bj, qj, kj, vj = b[sl], qq[sl], kk[sl], vv[sl]
        e_in = 0 if reverse else C - 1
        b_end = bj[e_in:e_in + 1, :]
        H = C // 2
        acc = [None, None]
        for s in range(C):
            bs, ks, vs = bj[s:s + 1, :], kj[s:s + 1, :], vj[s:s + 1, :]
            hs = s // H
            full = (hs - 1) if reverse else (hs + 1)
            for h in (0, 1):
                if h != hs and h != full:
                    continue
                rows_h = slice(h * H, (h + 1) * H)
                diff = bj[rows_h] - bs
                if h == hs:
                    keep = (sub_h <= s - hs * H) if reverse else (sub_h >= s - hs * H)
                    diff = jnp.where(keep, diff, -1e30)
                sc = jnp.sum(jnp.exp(diff) * (qj[rows_h] * ks), axis=-1, keepdims=True)
                acc[h] = sc * vs if acc[h] is None else acc[h] + sc * vs
        outs[j] = jnp.concatenate(acc, axis=0) + _bdot(qj * jnp.exp(bj), st, _NT)
        st = st * jnp.exp(b_end) + _bdot(vj, kj * jnp.exp(b_end - bj), _TN)
    st_ref[...] = st
    o_ref[0, rows, :] = jnp.concatenate(outs, axis=0)


def _hgrn2_kernel(qf_ref, vf_ref, zf_ref, qb_ref, vb_ref, zb_ref, lbf_ref, lbb_ref, of_ref, ob_ref, stf_ref, stb_ref):
    @pl.when(pl.program_id(2) == 0)
    def _():
        stf_ref[...] = jnp.zeros_like(stf_ref)
        stb_ref[...] = jnp.zeros_like(stb_ref)

    ng = HG_TT // HG_GRP

    def body(gi, carry):
        _hgrn2_group(qf_ref, vf_ref, zf_ref, lbf_ref, of_ref, stf_ref, gi, False)
        _hgrn2_group(qb_ref, vb_ref, zb_ref, lbb_ref, ob_ref, stb_ref, ng - 1 - gi, True)
        return carry

    lax.fori_loop(0, ng, body, 0)


def hgrn2_scan(proj, lbp):
    B, T, _ = proj.shape
    nt = T // HG_TT
    blk = (1, HG_TT, HEAD_A)
    fmap = lambda off: (lambda b, h, i: (b, i, off + h))
    rmap = lambda off: (lambda b, h, i: (b, nt - 1 - i, off + h))
    out = jax.ShapeDtypeStruct((B, T, D_A), F32)
    return pl.pallas_call(
        _hgrn2_kernel,
        grid=(B, H_A, nt),
        in_specs=[pl.BlockSpec(blk, fmap(0)), pl.BlockSpec(blk, fmap(H_A)), pl.BlockSpec(blk, fmap(2 * H_A)),
                  pl.BlockSpec(blk, rmap(0)), pl.BlockSpec(blk, rmap(H_A)), pl.BlockSpec(blk, rmap(3 * H_A)),
                  pl.BlockSpec((1, 3, HEAD_A), lambda b, h, i: (0, 0, h)),
                  pl.BlockSpec((1, 3, HEAD_A), lambda b, h, i: (1, 0, h))],
        out_specs=[pl.BlockSpec(blk, fmap(0)), pl.BlockSpec(blk, rmap(0))],
        out_shape=[out, out],
        scratch_shapes=[pltpu.VMEM((HEAD_A, HEAD_A), F32), pltpu.VMEM((HEAD_A, HEAD_A), F32)],
        compiler_params=_cparams("parallel", "parallel", "arbitrary"),
        name="hgrn2",
    )(proj, proj, proj, proj, proj, proj, lbp, lbp)


RT_TT = 256


def rope_tables(T):
    theta = 1.0 / jnp.power(ROPE_BASE, jnp.linspace(0.0, 1.0, HEAD_B // 2, dtype=F32))
    ang = jnp.arange(T, dtype=F32)[:, None] * theta[None, :]
    cos = jnp.repeat(jnp.cos(ang), 2, axis=1)
    sin = jnp.repeat(jnp.sin(ang), 2, axis=1)
    sign = jnp.where(jnp.arange(HEAD_B) % 2 == 0, -1.0, 1.0).astype(F32)
    return cos, sin * sign[None, :]


def _rope(t, cos, sin):
    lane = lax.broadcasted_iota(jnp.int32, t.shape, 1)
    nxt = pltpu.roll(t, t.shape[1] - 1, axis=1)
    prv = pltpu.roll(t, 1, axis=1)
    return t * cos + jnp.where(lane % 2 == 0, nxt, prv) * sin


def _ret_kernel(lg_ref, cosf_ref, sinf_ref, cosb_ref, sinb_ref, qf_ref, kf_ref, vf_ref, qb_ref, kb_ref, vb_ref,
                of_ref, ob_ref, stf_ref, stb_ref):
    Tt = RT_TT

    @pl.when(pl.program_id(2) == 0)
    def _():
        stf_ref[...] = jnp.zeros_like(stf_ref)
        stb_ref[...] = jnp.zeros_like(stb_ref)

    lgw = lg_ref[0]
    lg = lgw[:, 0:HEAD_B]
    pos = lax.broadcasted_iota(jnp.int32, (Tt, 1), 0).astype(F32)
    scale = HEAD_B ** -0.5

    cos, sin = cosf_ref[...], sinf_ref[...]
    q = _rope(qf_ref[0], cos, sin)
    k = _rope(kf_ref[0], cos, sin) * scale
    v = vf_ref[0]
    row = lax.broadcasted_iota(jnp.int32, (Tt, Tt), 0)
    col = lax.broadcasted_iota(jnp.int32, (Tt, Tt), 1)
    dmat = jnp.exp(lgw * jnp.abs(row - col).astype(F32))
    st = stf_ref[...]
    o = _bdot(_bdot(q, k, _NT) * dmat, v)
    of_ref[0] = o + _bdot(q * jnp.exp(lg * (pos + 1.0)), st)
    stf_ref[...] = st * jnp.exp(lg * Tt) + _bdot(k * jnp.exp(lg * (Tt - 1.0 - pos)), v, _TN)

    cos, sin = cosb_ref[...], sinb_ref[...]
    q = _rope(qb_ref[0], cos, sin)
    k = _rope(kb_ref[0], cos, sin) * scale
    st = stb_ref[...]
    ob_ref[0] = _bdot(q * jnp.exp(lg * (Tt - pos)), st)
    stb_ref[...] = st * jnp.exp(lg * Tt) + _bdot(k * jnp.exp(lg * pos), vb_ref[0], _TN)


def retention(proj):
    B, T, _ = proj.shape
    nt = T // RT_TT
    blk = (1, RT_TT, HEAD_B)
    qoff = 5 * D_A // HEAD_B
    koff, voff = qoff + H_B, qoff + 2 * H_B
    cos, sin = rope_tables(T)
    lgam = jnp.log1p(-jnp.power(2.0, -5.0 - jnp.arange(H_B, dtype=F32)))
    lgam = jnp.broadcast_to(lgam[:, None, None], (H_B, 1, RT_TT))
    lspec = pl.BlockSpec((1, 1, RT_TT), lambda b, h, i: (h, 0, 0))
    fwd = lambda i: i
    rev = lambda i: nt - 1 - i
    tab = lambda tix: pl.BlockSpec((RT_TT, HEAD_B), lambda b, h, i: (tix(i), 0))
    col = lambda tix, off: pl.BlockSpec(blk, lambda b, h, i: (b, tix(i), off + h))
    out = jax.ShapeDtypeStruct((B, T, D_B), F32)
    return pl.pallas_call(
        _ret_kernel, grid=(B, H_B, nt),
        in_specs=[lspec, tab(fwd), tab(fwd), tab(rev), tab(rev),
                  col(fwd, qoff), col(fwd, koff), col(fwd, voff), col(rev, qoff), col(rev, koff), col(rev, voff)],
        out_specs=[col(fwd, 0), col(rev, 0)],
        out_shape=[out, out],
        scratch_shapes=[pltpu.VMEM((HEAD_B, HEAD_B), F32), pltpu.VMEM((HEAD_B, HEAD_B), F32)],
        compiler_params=_cparams("parallel", "parallel", "arbitrary"), name="retention",
    )(lgam, cos, sin, cos, sin, proj, proj, proj, proj, proj, proj)


LANES = 128


def _head_ones():
    r = lax.broadcasted_iota(jnp.int32, (LANES, LANES), 0) // HEAD_C
    c = lax.broadcasted_iota(jnp.int32, (LANES, LANES), 1) // HEAD_C
    return (r == c).astype(BF16)


def _head_sum(x, bd):
    hi = x.astype(BF16)
    lo = (x - hi.astype(F32)).astype(BF16)
    return (lax.dot_general(hi, bd, _NN, preferred_element_type=F32)
            + lax.dot_general(lo, bd, _NN, preferred_element_type=F32))


def _layer_norm_rows(y, w, b):
    mu = jnp.mean(y, axis=-1, keepdims=True)
    yc = y - mu
    var = jnp.mean(yc * yc, axis=-1, keepdims=True)
    return yc * lax.rsqrt(var + LN_EPS) * w + b


def _full_spec(a):
    return pl.BlockSpec(a.shape, lambda i: (0,) * a.ndim)


def _odd_pre_kernel(x_ref, xp_ref, xn_ref, mu_ref, wrkv_ref, w1_ref, w2_ref, a1_ref, a2_ref, g1_ref, g2_ref,
                    vec_ref, r_ref, v_ref, kk_ref, g_ref, kd_ref, bb_ref, lw_ref, *, tiles_per_seq):
    D = D_MODEL
    x = x_ref[...]
    tm = x.shape[0]
    t_in_seq = pl.program_id(0) % tiles_per_seq
    prev_row = jnp.where(t_in_seq == 0, 0.0, xp_ref[0, 7:8, :])
    next_row = jnp.where(t_in_seq == tiles_per_seq - 1, 0.0, xn_ref[0, 0:1, :])
    rid = lax.broadcasted_iota(jnp.int32, x.shape, 0)
    x_prev = jnp.where(rid == 0, prev_row, pltpu.roll(x, 1, axis=0))
    x_next = jnp.where(rid == tm - 1, next_row, pltpu.roll(x, tm - 1, axis=0))
    xx = 0.5 * (x_prev + x_next) - x
    mix = lambda p: (x + xx * mu_ref[p:p + 1, :]).astype(BF16)
    r = jnp.dot(mix(0), wrkv_ref[0], preferred_element_type=F32)
    k = jnp.dot(mix(1), wrkv_ref[1], preferred_element_type=F32)
    v = jnp.dot(mix(2), wrkv_ref[2], preferred_element_type=F32)
    hw = jnp.tanh(jnp.dot(mix(3), w1_ref[...], preferred_element_type=F32))
    lora_w = _bdot(hw, w2_ref[...])
    lora_a = _bdot(jnp.dot(mix(4), a1_ref[...], preferred_element_type=F32), a2_ref[...])
    hg = jnp.dot(mix(5), g1_ref[...], preferred_element_type=F32)
    g = _bdot(1.0 / (1.0 + jnp.exp(-hg)), g2_ref[...])
    k_k, k_a = vec_ref[4:5, :], vec_ref[5:6, :]
    bd = _head_ones()
    kq = k * k_k
    kks = []
    for s in range(D // LANES):
        ks = kq[:, s * LANES:(s + 1) * LANES]
        nrm = jnp.sqrt(_head_sum(ks * ks, bd))
        kks.append(ks / jnp.maximum(nrm, 1e-12))
    kk = jnp.concatenate(kks, axis=1)
    r_ref[...] = r.astype(r_ref.dtype)
    v_ref[...] = v.astype(v_ref.dtype)
    kk_ref[...] = kk.astype(kk_ref.dtype)
    g_ref[...] = g.astype(g_ref.dtype)
    for n in range(2):
        u = vec_ref[n:n + 1, :] + lora_w[:, n * D:(n + 1) * D]
        lw_ref[n] = -math.exp(-0.5) / (1.0 + jnp.exp(-u))
        a = 1.0 / (1.0 + jnp.exp(-(vec_ref[2 + n:3 + n, :] + lora_a[:, n * D:(n + 1) * D])))
        kd_ref[n] = (k * (1.0 + (a - 1.0) * k_a)).astype(kd_ref.dtype)
        bb_ref[n] = (-(kk * a)).astype(bb_ref.dtype)


def odd_pre(x, T, mu, wrkv, w1c, w2bd, a1c, a2bd, g1, g2, vec, tm=256):
    n, D = x.shape
    row = pl.BlockSpec((tm, D), lambda i: (i, 0))
    x8 = x.reshape(n // 8, 8, D)
    prev8 = pl.BlockSpec((1, 8, D), lambda i: (jnp.maximum(i * (tm // 8) - 1, 0), 0, 0))
    next8 = pl.BlockSpec((1, 8, D), lambda i: (jnp.minimum((i + 1) * (tm // 8), n // 8 - 1), 0, 0))
    row2 = pl.BlockSpec((2, tm, D), lambda i: (0, i, 0))
    o1 = jax.ShapeDtypeStruct((n, D), BF16)
    o2 = jax.ShapeDtypeStruct((2, n, D), BF16)
    return pl.pallas_call(
        functools.partial(_odd_pre_kernel, tiles_per_seq=T // tm), grid=(n // tm,),
        in_specs=[row, prev8, next8] + [_full_spec(a) for a in (mu, wrkv, w1c, w2bd, a1c, a2bd, g1, g2, vec)],
        out_specs=[row, row, row, row, row2, row2, row2],
        out_shape=[o1, o1, o1, o1, o2, o2, jax.ShapeDtypeStruct((2, n, D), F32)],
        compiler_params=_cparams("parallel"), name="rwkv7_pre",
    )(x, x8, x8, mu, wrkv, w1c, w2bd, a1c, a2bd, g1, g2, vec)


def _odd_post_kernel(of_ref, ob_ref, r_ref, v_ref, g_ref, kd_ref, x_ref, vec_ref, wout_ref, o_ref):
    D = D_MODEL
    bd = _head_ones()
    lnx_w, lnx_b, r_k = vec_ref[0:1, :], vec_ref[1:2, :], vec_ref[2:3, :]
    o = of_ref[...] + ob_ref[...]
    rk = r_ref[...].astype(F32) * r_k * (kd_ref[0].astype(F32) + kd_ref[1].astype(F32))
    v = v_ref[...].astype(F32)
    ys = []
    for s in range(D // LANES):
        sl = slice(s * LANES, (s + 1) * LANES)
        os_ = o[:, sl]
        oc = os_ - _head_sum(os_, bd) * (1.0 / HEAD_C)
        var = _head_sum(oc * oc, bd) * (1.0 / HEAD_C)
        on = oc * lax.rsqrt(var + LNX_EPS) * lnx_w[:, sl] + lnx_b[:, sl]
        ys.append(on + _head_sum(rk[:, sl], bd) * v[:, sl])
    y = jnp.concatenate(ys, axis=1) * g_ref[...].astype(F32)
    h = jnp.dot(y.astype(BF16), wout_ref[...], preferred_element_type=F32)
    o_ref[...] = _layer_norm_rows(DN_ALPHA * x_ref[...] + h, vec_ref[3:4, :], vec_ref[4:5, :])


def odd_post(o_f, o_b, r, v, g, kd, x, vec, w_out, tm=256):
    n, D = x.shape
    row = pl.BlockSpec((tm, D), lambda i: (i, 0))
    row2 = pl.BlockSpec((2, tm, D), lambda i: (0, i, 0))
    return pl.pallas_call(
        _odd_post_kernel, grid=(n // tm,),
        in_specs=[row, row, row, row, row, row2, row, _full_spec(vec), _full_spec(w_out)],
        out_specs=row,
        out_shape=jax.ShapeDtypeStruct((n, D), F32),
        compiler_params=_cparams("parallel"), name="rwkv7_post",
    )(o_f, o_b, r, v, g, kd, x, vec, w_out)


def _proj_ln_kernel(a_ref, w_ref, x_ref, ln_ref, o_ref):
    h = jnp.dot(a_ref[...].astype(BF16), w_ref[...], preferred_element_type=F32)
    o_ref[...] = _layer_norm_rows(DN_ALPHA * x_ref[...] + h, ln_ref[0:1, :], ln_ref[1:2, :])


def proj_ln(a, w, x, ln, tm=512):
    n, K = a.shape
    D = w.shape[1]
    return pl.pallas_call(
        _proj_ln_kernel, grid=(n // tm,),
        in_specs=[pl.BlockSpec((tm, K), lambda i: (i, 0)), pl.BlockSpec((K, D), lambda i: (0, 0)),
                  pl.BlockSpec((tm, D), lambda i: (i, 0)), pl.BlockSpec((2, D), lambda i: (0, 0))],
        out_specs=pl.BlockSpec((tm, D), lambda i: (i, 0)),
        out_shape=jax.ShapeDtypeStruct((n, D), F32),
        compiler_params=_cparams("parallel"), name="proj_res_ln",
    )(a, w, x, ln)


def _add_ln_kernel(h_ref, x_ref, ln_ref, o_ref):
    o_ref[...] = _layer_norm_rows(DN_ALPHA * x_ref[...] + h_ref[...], ln_ref[0:1, :], ln_ref[1:2, :])


def add_ln(h, x, ln, tm=512):
    n, D = x.shape
    row = pl.BlockSpec((tm, D), lambda i: (i, 0))
    return pl.pallas_call(
        _add_ln_kernel, grid=(n // tm,),
        in_specs=[row, row, pl.BlockSpec((2, D), lambda i: (0, 0))],
        out_specs=row, out_shape=jax.ShapeDtypeStruct((n, D), F32),
        compiler_params=_cparams("parallel"), name="res_ln",
    )(h, x, ln)


def _even_post_kernel(oaf_ref, oab_ref, obf_ref, obb_ref, ga_ref, gb_ref, x_ref, vec_ref, wout_ref, o_ref):
    silu = lambda t: t * (1.0 / (1.0 + jnp.exp(-t)))
    oa = oaf_ref[...] + oab_ref[...]
    ob = obf_ref[...] + obb_ref[...]
    ga, gb = ga_ref[...], gb_ref[...]
    na, nb = vec_ref[0:1, :], vec_ref[1:2, :]
    ys = []
    for h in range(H_A):
        sl = slice(h * HEAD_A, (h + 1) * HEAD_A)
        t = oa[:, sl]
        t = t * lax.rsqrt(jnp.mean(t * t, axis=-1, keepdims=True) + 1e-6)
        ys.append(t * na[:, sl] * silu(ga[:, sl]))
    for h in range(H_B):
        sl = slice(h * HEAD_B, (h + 1) * HEAD_B)
        t = ob[:, sl]
        t = t - jnp.mean(t, axis=-1, keepdims=True)
        t = t * lax.rsqrt(jnp.mean(t * t, axis=-1, keepdims=True) + 1e-6)
        ys.append(t * nb[:, sl] * silu(gb[:, sl]))
    y = jnp.concatenate(ys, axis=1)
    h_ = jnp.dot(y.astype(BF16), wout_ref[...], preferred_element_type=F32)
    o_ref[...] = _layer_norm_rows(DN_ALPHA * x_ref[...] + h_, vec_ref[2:3, :], vec_ref[3:4, :])


def even_post(oa_f, oa_b, ob_f, ob_b, proj, x, vec, w_out, tm=256):
    n, D = x.shape
    half = pl.BlockSpec((tm, D_A), lambda i: (i, 0))
    row = pl.BlockSpec((tm, D), lambda i: (i, 0))
    return pl.pallas_call(
        _even_post_kernel, grid=(n // tm,),
        in_specs=[half, half, half, half, pl.BlockSpec((tm, D_A), lambda i: (i, 4)),
                  pl.BlockSpec((tm, D_B), lambda i: (i, 8)), row,
                  pl.BlockSpec(vec.shape, lambda i: (0, 0)), pl.BlockSpec(w_out.shape, lambda i: (0, 0))],
        out_specs=row,
        out_shape=jax.ShapeDtypeStruct((n, D), F32),
        compiler_params=_cparams("parallel"), name="even_post",
    )(oa_f, oa_b, ob_f, ob_b, proj, proj, x, vec, w_out)


def _even_layer(xf, Bn, T, w_in, lb, norm_a, norm_b, w_out, ln):
    n, D = xf.shape
    proj = pmm(xf, w_in).reshape(Bn, T, -1)
    lbp = jnp.stack([jnp.log(lb), jnp.log1p(-lb), 1.0 - lb], axis=1)
    oa_f, oa_b = hgrn2_scan(proj, lbp)
    ob_f, ob_b = retention(proj)
    z = jnp.zeros((D - D_A,), F32)
    vec = jnp.concatenate([jnp.stack([jnp.concatenate([norm_a, z]), jnp.concatenate([norm_b, z])]), ln,
                           jnp.zeros((4, D), F32)], 0)
    fl = lambda t: t.reshape(n, -1)
    return even_post(fl(oa_f), fl(oa_b), fl(ob_f), fl(ob_b), fl(proj), xf, vec, w_out)


def _odd_weights(p, j):
    w1, w2, a1, a2 = p['od_w1'][j], p['od_w2'][j], p['od_a1'][j], p['od_a2'][j]
    z = jnp.zeros_like(w2[0])
    w2bd = jnp.concatenate([jnp.concatenate([w2[0], z], 1), jnp.concatenate([z, w2[1]], 1)], 0)
    a2bd = jnp.concatenate([jnp.concatenate([a2[0], z], 1), jnp.concatenate([z, a2[1]], 1)], 0)
    w0, a0 = p['od_w0'][j], p['od_a0'][j]
    zero = jnp.zeros_like(w0[0])
    vec_pre = jnp.stack([w0[0], w0[1], a0[0], a0[1], p['od_k_k'][j], p['od_k_a'][j], zero, zero])
    vec_post = jnp.stack([p['od_lnx_w'][j], p['od_lnx_b'][j], p['od_r_k'][j]])
    pre = (p['od_mu'][j], p['od_w_rkv'][j], jnp.concatenate([w1[0], w1[1]], 1), w2bd,
           jnp.concatenate([a1[0], a1[1]], 1), a2bd, p['od_g1'][j], p['od_g2'][j], vec_pre)
    return pre, vec_post


def _odd_layer(xf, Bn, T, pre, vec_post, w_out, ln):
    D = xf.shape[-1]
    r, v, kk, g, kd, bb, lw = odd_pre(xf, T, *pre)
    sh = lambda t: t.reshape(Bn, T, D)
    sh2 = lambda t: t.reshape(2, Bn, T, D)
    o_f, o_b = rwkv7_scan(sh(r), sh(v), sh(kk), sh2(lw), sh2(kd), sh2(bb))
    vec = jnp.concatenate([vec_post, ln, jnp.zeros((3, D), F32)], 0)
    return odd_post(o_f.reshape(-1, D), o_b.reshape(-1, D), r, v, g, kd, xf, vec, w_out)


def _cross_attn_layer(xf, Bn, T, mem, w_q, w_kv, w_out, ln):
    return cross_attn_layer(xf, T, _mm(mem, w_kv), w_q, w_out, ln)


def _moe_layer(xt, w_router, w_in, w_out, layer, ln):
    n, D = xt.shape
    cap = (CAP_FACTOR * n) // N_EXPERTS
    aff = jax.nn.softmax(pmm(xt, w_router), axis=-1)
    gate, idx = lax.top_k(aff.T, cap)
    xe = xt[idx]
    ye = moe_ffn(xe, w_in, w_out, gate[..., None], layer)
    y = jnp.zeros_like(xt).at[idx.reshape(-1)].add(ye.reshape(-1, D))
    return add_ln(y, xt, ln)


def _trunk(x, mem, p):
    Bn, T, D = x.shape
    xf = x.reshape(Bn * T, D)
    p_lb = jax.nn.softmax(p['ev_lb_logits'], axis=1)
    lb_all = jnp.clip(jnp.cumsum(p_lb, axis=1) - p_lb[:, :1], 0.0, 1.0)
    for layer in range(DEPTH):
        j = layer // 2
        ln = lambda s: jnp.stack([p['ln_w'][layer, s], p['ln_b'][layer, s]])
        if layer % 2 == 0:
            xf = _even_layer(xf, Bn, T, p['ev_w_in'][j], lb_all[:, j], p['ev_norm_a'][j], p['ev_norm_b'][j],
                             p['ev_w_out'][j], ln(0))
        else:
            pre, vec_post = _odd_weights(p, j)
            xf = _odd_layer(xf, Bn, T, pre, vec_post, p['od_w_out'][j], ln(0))
        xf = _cross_attn_layer(xf, Bn, T, mem, p['ca_w_q'][layer], p['ca_w_kv'][layer], p['ca_w_out'][layer], ln(1))
        xf = _moe_layer(xf, p['moe_router'][layer], p['moe_w_in'], p['moe_w_out'], layer, ln(2))
    return xf.reshape(Bn, T, D)


_MATMUL_WEIGHTS = ('ev_w_in', 'ev_w_out', 'od_w_rkv', 'od_w1', 'od_w2', 'od_a1', 'od_a2', 'od_g1', 'od_g2',
                   'od_w_out', 'ca_w_q', 'ca_w_kv', 'ca_w_out', 'moe_router')


def kernel(x_prompt, x_sample, mem_prompt, mem_sample, ev_w_in, ev_lb_logits, ev_norm_a, ev_norm_b, ev_w_out, od_mu, od_w_rkv, od_w0, od_w1, od_w2, od_a0, od_a1, od_a2, od_g1, od_g2, od_k_k, od_k_a, od_r_k, od_lnx_w, od_lnx_b, od_w_out, ca_w_q, ca_w_kv, ca_w_out, moe_router, moe_w_in, moe_w_out, ln_w, ln_b):
    params = dict(ev_w_in=ev_w_in, ev_lb_logits=ev_lb_logits, ev_norm_a=ev_norm_a, ev_norm_b=ev_norm_b,
                  ev_w_out=ev_w_out, od_mu=od_mu, od_w_rkv=od_w_rkv, od_w0=od_w0, od_w1=od_w1, od_w2=od_w2,
                  od_a0=od_a0, od_a1=od_a1, od_a2=od_a2, od_g1=od_g1, od_g2=od_g2, od_k_k=od_k_k,
                  od_k_a=od_k_a, od_r_k=od_r_k, od_lnx_w=od_lnx_w, od_lnx_b=od_lnx_b, od_w_out=od_w_out,
                  ca_w_q=ca_w_q, ca_w_kv=ca_w_kv, ca_w_out=ca_w_out, moe_router=moe_router,
                  moe_w_in=moe_w_in, moe_w_out=moe_w_out, ln_w=ln_w, ln_b=ln_b)
    for name in _MATMUL_WEIGHTS:
        params[name] = params[name].astype(BF16)
    y_prompt = _trunk(x_prompt, mem_prompt, params)
    y_sample = _trunk(x_sample, mem_sample, params)
    return (y_prompt, y_sample)
```
